```python
import math
import jax
import jax.numpy as jnp
from jax import lax
import numpy as np

D_MODEL = 4096
BATCH = 4
SEQ = 2048
DEPTH = 2
DEC_BATCH = 8
DEC_SEQ = 8
PAST_LEN = 16384
PAGE_SIZE = 128

HEAD_DIM = 128
N_HEADS = D_MODEL // 256
KV_HEADS = N_HEADS // 2
Q_W = N_HEADS * HEAD_DIM
KV_W = KV_HEADS * HEAD_DIM
MOBA_BLOCK = 256
MOBA_TOPK = 3
Q_CHUNK = 32
S_W = D_MODEL // 2
S5_GROUP = 16
N_GROUPS = S_W // S5_GROUP
S5_STATE = 64
IN_W = Q_W + 2 * KV_W + S_W + 2 * D_MODEL
D_FF = 256 * ((8 * D_MODEL // 3 + 255) // 256)
CONV_W = 3
EPS = 1e-6

kernel_name = 'hybrid_moba_s5_convffn_step'


def rmsnorm(x, g):
    xf = x.astype(jnp.float32)
    y = xf * lax.rsqrt(jnp.mean(xf * xf, axis=-1, keepdims=True) + EPS)
    return (y * g.astype(jnp.float32)).astype(x.dtype)


def alibi_slopes():
    return 2.0 ** (-8.0 * jnp.arange(1, N_HEADS + 1, dtype=jnp.float32) / N_HEADS)


def moba_attention(q, k_all, v_all, pos0):
    nb, T, H, Dh = q.shape
    L = k_all.shape[1]
    n_full = L // MOBA_BLOCK
    n_blk = -(-L // MOBA_BLOCK)
    pad = n_blk * MOBA_BLOCK - L
    padw = ((0, 0), (0, pad), (0, 0), (0, 0))
    kb = jnp.pad(k_all, padw).reshape(nb, n_blk, MOBA_BLOCK, KV_HEADS, Dh)
    vb = jnp.pad(v_all, padw).reshape(nb, n_blk, MOBA_BLOCK, KV_HEADS, Dh)
    q_pos = pos0 + jnp.arange(T, dtype=jnp.int32)
    cur = q_pos // MOBA_BLOCK
    kv_of_h = jnp.arange(H, dtype=jnp.int32) // (H // KV_HEADS)
    cur_b = jnp.broadcast_to(cur[None, :, None, None], (nb, T, H, 1))
    n_sel = min(MOBA_TOPK, n_full)
    if n_sel > 0:
        k_mean = jnp.mean(kb[:, :n_full].astype(jnp.float32), axis=2)
        gate = jnp.einsum('bthd,bnhd->bthn', q.astype(jnp.float32), k_mean[:, :, kv_of_h])
        is_past = jnp.arange(n_full, dtype=jnp.int32)[None, None, None, :] < cur[None, :, None, None]
        gate = jnp.where(is_past, gate, -jnp.inf)
        _, sel = lax.top_k(gate, n_sel)
        sel_ok = jnp.broadcast_to(jnp.arange(n_sel, dtype=jnp.int32)[None, None, None, :] < cur[None, :, None, None], (nb, T, H, n_sel))
        sel = jnp.where(sel_ok, sel.astype(jnp.int32), cur_b)
        idx = jnp.concatenate([sel, cur_b], axis=-1)
        slot_ok = jnp.concatenate([sel_ok, jnp.ones((nb, T, H, 1), dtype=bool)], axis=-1)
    else:
        idx = cur_b
        slot_ok = jnp.ones((nb, T, H, 1), dtype=bool)
    n_slot = idx.shape[-1]
    qc_len = math.gcd(T, Q_CHUNK)
    n_c = T // qc_len
    slopes = alibi_slopes()
    scale = HEAD_DIM ** -0.5
    offs = jnp.arange(MOBA_BLOCK, dtype=jnp.int32)
    kv_idx = kv_of_h[None, :, None]

    def attend(args):
        b, qc, ic, oc, pc = args
        kg = kb[b, ic, :, kv_idx]
        vg = vb[b, ic, :, kv_idx]
        s = jnp.einsum('qhd,qhjkd->qhjk', qc, kg, preferred_element_type=jnp.float32) * scale
        kpos = ic[..., None] * MOBA_BLOCK + offs
        dist = (pc[:, None, None, None] - kpos).astype(jnp.float32)
        s = s - slopes[None, :, None, None] * dist
        ok = oc[..., None] & (kpos <= pc[:, None, None, None])
        s = jnp.where(ok, s, -jnp.inf)
        p = jax.nn.softmax(s.reshape(qc_len, H, n_slot * MOBA_BLOCK), axis=-1).reshape(s.shape)
        o = jnp.einsum('qhjk,qhjkd->qhd', p.astype(vg.dtype), vg, preferred_element_type=jnp.float32)
        return o.astype(qc.dtype)

    xs = (jnp.repeat(jnp.arange(nb, dtype=jnp.int32), n_c),
          q.reshape(nb * n_c, qc_len, H, Dh),
          idx.reshape(nb * n_c, qc_len, H, n_slot),
          slot_ok.reshape(nb * n_c, qc_len, H, n_slot),
          jnp.tile(q_pos.reshape(n_c, qc_len), (nb, 1)))
    out = lax.map(attend, xs)
    return out.reshape(nb, T, H * Dh)


def complex_affine_combine(e1, e2):
    a1r, a1i, b1r, b1i = e1
    a2r, a2i, b2r, b2i = e2
    return (a1r * a2r - a1i * a2i, a1r * a2i + a1i * a2r,
            a2r * b1r - a2i * b1i + b2r, a2r * b1i + a2i * b1r + b2i)


def s5_branch(u, h0_re, h0_im, lam_re, lam_im, log_dt, b_re, b_im, c_re, c_im, d_skip, w_glu):
    f32 = jnp.float32
    nb, T, _ = u.shape
    dt = jnp.exp(log_dt.astype(f32))[:, None]
    lr = lam_re.astype(f32)
    li = lam_im.astype(f32)
    mag = jnp.exp(lr * dt)
    a_re = mag * jnp.cos(li * dt)
    a_im = mag * jnp.sin(li * dt)
    den = lr * lr + li * li
    q_re = ((a_re - 1.0) * lr + a_im * li) / den
    q_im = (a_im * lr - (a_re - 1.0) * li) / den
    br = b_re.astype(f32)
    bi = b_im.astype(f32)
    bbar_re = q_re[..., None] * br - q_im[..., None] * bi
    bbar_im = q_re[..., None] * bi + q_im[..., None] * br
    uf = u.astype(f32)
    ug = uf.reshape(nb, T, N_GROUPS, S5_GROUP)
    bu_re = jnp.einsum('btgi,gpi->btgp', ug, bbar_re)
    bu_im = jnp.einsum('btgi,gpi->btgp', ug, bbar_im)
    e_re = jnp.concatenate([h0_re.astype(f32)[:, None], bu_re], axis=1)
    e_im = jnp.concatenate([h0_im.astype(f32)[:, None], bu_im], axis=1)
    a_re_s = jnp.broadcast_to(a_re[None, None], (1, T + 1, N_GROUPS, S5_STATE))
    a_im_s = jnp.broadcast_to(a_im[None, None], (1, T + 1, N_GROUPS, S5_STATE))
    _, _, h_re, h_im = lax.associative_scan(complex_affine_combine, (a_re_s, a_im_s, e_re, e_im), axis=1)
    h_re = h_re[:, 1:]
    h_im = h_im[:, 1:]
    y = (jnp.einsum('btgp,gip->btgi', h_re, c_re.astype(f32))
         - jnp.einsum('btgp,gip->btgi', h_im, c_im.astype(f32)))
    y = y.reshape(nb, T, S_W) + d_skip.astype(f32) * uf
    y = jax.nn.gelu(y)
    y = y * jax.nn.sigmoid(y @ w_glu.astype(f32))
    return y.astype(u.dtype), h_re[:, -1], h_im[:, -1]


def layer(x, c, k_past, v_past, h0_re, h0_im, conv_prev, pos0, lw):
    (w_ada, b_ada, norm_attn, w_in, w_attn_proj, lam_re, lam_im, log_dt, ssm_b_re, ssm_b_im,
     ssm_c_re, ssm_c_im, ssm_d, w_glu, w_ssm_proj, w_out, norm_ffn, w_up, w_conv, b_conv, w_down) = lw
    nb, T, _ = x.shape
    mod = jax.nn.silu(c) @ w_ada + b_ada
    sh1, sc1, g1, sh2, sc2, g2 = jnp.split(mod[:, None, :].astype(x.dtype), 6, axis=-1)
    h = rmsnorm(x, norm_attn) * (1 + sc1) + sh1
    z = h @ w_in
    cuts = [Q_W, Q_W + KV_W, Q_W + 2 * KV_W, Q_W + 2 * KV_W + S_W, Q_W + 2 * KV_W + S_W + D_MODEL]
    q, k, v, u, ga, gs = jnp.split(z, cuts, axis=-1)
    q = q.reshape(nb, T, N_HEADS, HEAD_DIM)
    k = k.reshape(nb, T, KV_HEADS, HEAD_DIM)
    v = v.reshape(nb, T, KV_HEADS, HEAD_DIM)
    k_all = k if k_past is None else jnp.concatenate([k_past.astype(k.dtype), k], axis=1)
    v_all = v if v_past is None else jnp.concatenate([v_past.astype(v.dtype), v], axis=1)
    attn = moba_attention(q, k_all, v_all, pos0)
    if h0_re is None:
        h0_re = jnp.zeros((nb, N_GROUPS, S5_STATE), jnp.float32)
        h0_im = jnp.zeros((nb, N_GROUPS, S5_STATE), jnp.float32)
    ssm, hT_re, hT_im = s5_branch(u, h0_re, h0_im, lam_re, lam_im, log_dt, ssm_b_re, ssm_b_im,
                                  ssm_c_re, ssm_c_im, ssm_d, w_glu)
    merged = jax.nn.sigmoid(ga) * (attn @ w_attn_proj) + jax.nn.sigmoid(gs) * (ssm @ w_ssm_proj)
    x = x + g1 * (merged @ w_out)
    h2 = rmsnorm(x, norm_ffn) * (1 + sc2) + sh2
    up = h2 @ w_up
    if conv_prev is None:
        conv_prev = jnp.zeros((nb, CONV_W - 1, up.shape[-1]), up.dtype)
    buf = jnp.concatenate([conv_prev.astype(up.dtype), up], axis=1)
    conv = b_conv + sum(w_conv[j] * buf[:, j:j + T] for j in range(CONV_W))
    a_ff, g_ff = jnp.split(conv, 2, axis=-1)
    x = x + g2 * ((jax.nn.silu(g_ff) * a_ff) @ w_down)
    return x, k, v, hT_re, hT_im, buf[:, T:]


def setup_inputs(seed: int = 0) -> dict:
    key = jax.random.key(seed)
    ks = jax.random.split(key, 40)
    f32 = jnp.float32
    D = D_MODEL
    n_pages = PAST_LEN // PAGE_SIZE
    n_used = DEC_BATCH * n_pages
    n_pool = n_used + n_used // 4

    def nrm(k, shape, s):
        return jax.random.normal(k, shape, f32) * s

    n = jnp.arange(S5_STATE, dtype=f32)
    return {
        'x_prompt': nrm(ks[0], (BATCH, SEQ, D), 1.0),
        'x_sample': nrm(ks[1], (DEC_BATCH, DEC_SEQ, D), 1.0),
        'c_prompt': nrm(ks[2], (BATCH, D), 1.0),
        'c_sample': nrm(ks[3], (DEC_BATCH, D), 1.0),
        'cache_k': nrm(ks[4], (DEPTH, n_pool, PAGE_SIZE, KV_HEADS, HEAD_DIM), 1.0),
        'cache_v': nrm(ks[5], (DEPTH, n_pool, PAGE_SIZE, KV_HEADS, HEAD_DIM), 1.0),
        'state_ssm_re': nrm(ks[6], (DEPTH, DEC_BATCH, N_GROUPS, S5_STATE), 0.5),
        'state_ssm_im': nrm(ks[7], (DEPTH, DEC_BATCH, N_GROUPS, S5_STATE), 0.5),
        'state_conv': nrm(ks[8], (DEPTH, DEC_BATCH, CONV_W - 1, 2 * D_FF), 1.0),
        'page_table': jax.random.permutation(ks[9], n_pool)[:n_used].reshape(DEC_BATCH, n_pages).astype(jnp.int32),
        'w_ada': nrm(ks[10], (DEPTH, D, 6 * D), 0.5 * D ** -0.5),
        'b_ada': nrm(ks[11], (DEPTH, 6 * D), 0.02),
        'norm_attn': 1.0 + nrm(ks[12], (DEPTH, D), 0.02),
        'w_in': nrm(ks[13], (DEPTH, D, IN_W), D ** -0.5),
        'w_attn_proj': nrm(ks[14], (DEPTH, Q_W, D), Q_W ** -0.5),
        'lam_re': -0.5 + nrm(ks[15], (DEPTH, N_GROUPS, S5_STATE), 0.01),
        'lam_im': math.pi * n + nrm(ks[16], (DEPTH, N_GROUPS, S5_STATE), 0.01),
        'log_dt': jax.random.uniform(ks[17], (DEPTH, N_GROUPS), f32, math.log(1e-3), math.log(1e-1)),
        'ssm_b_re': nrm(ks[18], (DEPTH, N_GROUPS, S5_STATE, S5_GROUP), (2 * S5_GROUP) ** -0.5),
        'ssm_b_im': nrm(ks[19], (DEPTH, N_GROUPS, S5_STATE, S5_GROUP), (2 * S5_GROUP) ** -0.5),
        'ssm_c_re': nrm(ks[20], (DEPTH, N_GROUPS, S5_GROUP, S5_STATE), (2 * S5_STATE) ** -0.5),
        'ssm_c_im': nrm(ks[21], (DEPTH, N_GROUPS, S5_GROUP, S5_STATE), (2 * S5_STATE) ** -0.5),
        'ssm_d': nrm(ks[22], (DEPTH, S_W), 1.0),
        'w_glu': nrm(ks[23], (DEPTH, S_W, S_W), S_W ** -0.5),
        'w_ssm_proj': nrm(ks[24], (DEPTH, S_W, D), S_W ** -0.5),
        'w_out': nrm(ks[25], (DEPTH, D, D), D ** -0.5),
        'norm_ffn': 1.0 + nrm(ks[26], (DEPTH, D), 0.02),
        'w_up': nrm(ks[27], (DEPTH, D, 2 * D_FF), D ** -0.5),
        'w_conv': nrm(ks[28], (DEPTH, CONV_W, 2 * D_FF), CONV_W ** -0.5),
        'b_conv': nrm(ks[29], (DEPTH, 2 * D_FF), 0.02),
        'w_down': nrm(ks[30], (DEPTH, D_FF, D), D_FF ** -0.5),
        'norm_final': 1.0 + nrm(ks[31], (D,), 0.02),
    }


def reference(x_prompt, x_sample, c_prompt, c_sample, cache_k, cache_v, state_ssm_re, state_ssm_im,
              state_conv, page_table, w_ada, b_ada, norm_attn, w_in, w_attn_proj, lam_re, lam_im,
              log_dt, ssm_b_re, ssm_b_im, ssm_c_re, ssm_c_im, ssm_d, w_glu, w_ssm_proj, w_out,
              norm_ffn, w_up, w_conv, b_conv, w_down, norm_final):
    n_seq, n_pages = page_table.shape
    past_len = n_pages * PAGE_SIZE
    xp = x_prompt
    xs = x_sample
    kp, vp, ks, vs = [], [], [], []
    hpr, hpi, hsr, hsi = [], [], [], []
    cp, cs = [], []
    for l in range(DEPTH):
        lw = (w_ada[l], b_ada[l], norm_attn[l], w_in[l], w_attn_proj[l], lam_re[l], lam_im[l],
              log_dt[l], ssm_b_re[l], ssm_b_im[l], ssm_c_re[l], ssm_c_im[l], ssm_d[l], w_glu[l],
              w_ssm_proj[l], w_out[l], norm_ffn[l], w_up[l], w_conv[l], b_conv[l], w_down[l])
        xp, k1, v1, r1, i1, c1 = layer(xp, c_prompt, None, None, None, None, None, 0, lw)
        k_past = cache_k[l, page_table].reshape(n_seq, past_len, KV_HEADS, HEAD_DIM)
        v_past = cache_v[l, page_table].reshape(n_seq, past_len, KV_HEADS, HEAD_DIM)
        xs, k2, v2, r2, i2, c2 = layer(xs, c_sample, k_past, v_past, state_ssm_re[l], state_ssm_im[l],
                                       state_conv[l], past_len, lw)
        kp.append(k1)
        vp.append(v1)
        ks.append(k2)
        vs.append(v2)
        hpr.append(r1)
        hpi.append(i1)
        hsr.append(r2)
        hsi.append(i2)
        cp.append(c1)
        cs.append(c2)
    y_prompt = rmsnorm(xp, norm_final)
    y_sample = rmsnorm(xs, norm_final)
    return (y_prompt, y_sample, jnp.stack(kp), jnp.stack(vp), jnp.stack(ks), jnp.stack(vs),
            jnp.stack(hpr), jnp.stack(hpi), jnp.stack(hsr), jnp.stack(hsi), jnp.stack(cp), jnp.stack(cs))
```

```python
import functools
import math

import jax
import jax.numpy as jnp
from jax import lax
from jax.experimental import pallas as pl
from jax.experimental.pallas import tpu as pltpu

F32 = jnp.float32
BF16 = jnp.bfloat16

V7X_LANES = 128
V7X_SUBLANES = 8
V7X_VMEM_BYTES = 64 * 2**20
VMEM_BUDGET = V7X_VMEM_BYTES - 8 * 2**20

HEAD_DIM = 128
MOBA_BLOCK = 256
MOBA_TOPK = 3
S5_GROUP = 16
S5_STATE = 64
GROUPS_PER_CHUNK = 8
CHUNK_IN = GROUPS_PER_CHUNK * S5_GROUP
CHUNK_ST = GROUPS_PER_CHUNK * S5_STATE
N_SEG = V7X_SUBLANES
EPS = 1e-6
NEG_INF = float("-inf")


def _pick(dim, pref, align):
    t = min(pref, dim)
    t -= t % align
    while t >= align:
        if dim % t == 0:
            return t
        t -= align
    return dim


def _params(sem, vmem_bytes):
    limit = int(min(max(vmem_bytes * 5 // 4 + (4 << 20), 32 << 20), VMEM_BUDGET))
    return pltpu.CompilerParams(dimension_semantics=sem, vmem_limit_bytes=limit)


def _ada_kernel(c_ref, w_ref, b_ref, o_ref):
    s = jax.nn.silu(c_ref[...]).astype(BF16)
    w = w_ref[...].astype(BF16)
    o_ref[...] = jnp.dot(s, w, preferred_element_type=F32) + b_ref[...]


def _adaln(c_all, w_ada, b_ada):
    depth, d, n = w_ada.shape
    rows = c_all.shape[0]
    tn = _pick(n, 512, V7X_LANES)
    vmem = 2 * d * tn * 4 + d * tn * 2 + 4 * rows * (d + tn) * 4
    return pl.pallas_call(
        _ada_kernel,
        grid=(depth, n // tn),
        in_specs=[
            pl.BlockSpec((rows, d), lambda l, j: (0, 0)),
            pl.BlockSpec((None, d, tn), lambda l, j: (l, 0, j)),
            pl.BlockSpec((None, 1, tn), lambda l, j: (l, 0, j)),
        ],
        out_specs=pl.BlockSpec((None, rows, tn), lambda l, j: (l, 0, j)),
        out_shape=jax.ShapeDtypeStruct((depth, rows, n), F32),
        compiler_params=_params(("arbitrary", "arbitrary"), vmem),
        name="adaln",
    )(c_all, w_ada, b_ada.reshape(depth, 1, n))


def _norm_kernel(x_ref, g_ref, *rest, modulate):
    if modulate:
        sc_ref, sh_ref, o_ref = rest
    else:
        (o_ref,) = rest
    x = x_ref[...]
    y = x * lax.rsqrt(jnp.mean(x * x, axis=-1, keepdims=True) + EPS)
    y = y * g_ref[...]
    if modulate:
        y = y * (1.0 + sc_ref[...]) + sh_ref[...]
    o_ref[...] = y.astype(o_ref.dtype)


def _row_param_spec(p, tr, tn, col_blocked):
    shared = p.shape[1] == 1
    rows = 1 if shared else tr
    if col_blocked:
        if shared:
            return pl.BlockSpec((None, rows, tn), lambda b, i, j: (b, 0, j))
        return pl.BlockSpec((None, rows, tn), lambda b, i, j: (b, i, j))
    if shared:
        return pl.BlockSpec((None, rows, tn), lambda b, i: (b, 0, 0))
    return pl.BlockSpec((None, rows, tn), lambda b, i: (b, i, 0))


def _rmsnorm(x3, gamma, scale=None, shift=None, out_dtype=BF16):
    nblk, rb, d = x3.shape
    tr = _pick(rb, 256, V7X_SUBLANES)
    modulate = scale is not None
    in_specs = [
        pl.BlockSpec((None, tr, d), lambda b, i: (b, i, 0)),
        pl.BlockSpec((1, d), lambda b, i: (0, 0)),
    ]
    args = [x3, gamma.reshape(1, d)]
    if modulate:
        in_specs += [_row_param_spec(scale, tr, d, False), _row_param_spec(shift, tr, d, False)]
        args += [scale, shift]
    vmem = 2 * tr * d * (4 + 4) + 6 * tr * d * 4
    return pl.pallas_call(
        functools.partial(_norm_kernel, modulate=modulate),
        grid=(nblk, rb // tr),
        in_specs=in_specs,
        out_specs=pl.BlockSpec((None, tr, d), lambda b, i: (b, i, 0)),
        out_shape=jax.ShapeDtypeStruct((nblk, rb, d), out_dtype),
        compiler_params=_params(("arbitrary", "arbitrary"), vmem),
        name="rmsnorm_mod",
    )(*args)


def _mm_plain_kernel(a_ref, w_ref, o_ref):
    o_ref[...] = jnp.dot(a_ref[...], w_ref[...], preferred_element_type=F32).astype(o_ref.dtype)


def _mm_plain(a3, w, col0, ncols, out_dtype, tm_pref=1024, tn_pref=1024):
    nblk, rb, k = a3.shape
    tm = _pick(rb, tm_pref, V7X_SUBLANES)
    tn = _pick(math.gcd(ncols, col0) if col0 else ncols, tn_pref, V7X_LANES)
    cb = col0 // tn
    osz = jnp.dtype(out_dtype).itemsize
    vmem = 2 * (tm * k * 2 + k * tn * 2 + tm * tn * osz) + tm * tn * 4
    return pl.pallas_call(
        _mm_plain_kernel,
        grid=(nblk, rb // tm, ncols // tn),
        in_specs=[
            pl.BlockSpec((None, tm, k), lambda b, i, j: (b, i, 0)),
            pl.BlockSpec((k, tn), lambda b, i, j: (0, j + cb)),
        ],
        out_specs=pl.BlockSpec((None, tm, tn), lambda b, i, j: (b, i, j)),
        out_shape=jax.ShapeDtypeStruct((nblk, rb, ncols), out_dtype),
        compiler_params=_params(("arbitrary", "arbitrary", "arbitrary"), vmem),
        name="proj",
    )(a3, w)


def _mm_glu_kernel(y_ref, w_ref, yt_ref, o_ref, ybf_ref):
    @pl.when(pl.program_id(2) == 0)
    def _():
        ybf_ref[...] = y_ref[...].astype(BF16)

    acc = jnp.dot(ybf_ref[...], w_ref[...], preferred_element_type=F32)
    o_ref[...] = (yt_ref[...] * jax.nn.sigmoid(acc)).astype(o_ref.dtype)


def _mm_glu(y3, w):
    nblk, rb, k = y3.shape
    n = w.shape[1]
    tm = _pick(rb, 1024, V7X_SUBLANES)
    tn = _pick(n, 512, V7X_LANES)
    vmem = 2 * (tm * k * 4 + k * tn * 2 + tm * tn * 4 + tm * tn * 2) + tm * k * 2 + tm * tn * 4
    return pl.pallas_call(
        _mm_glu_kernel,
        grid=(nblk, rb // tm, n // tn),
        in_specs=[
            pl.BlockSpec((None, tm, k), lambda b, i, j: (b, i, 0)),
            pl.BlockSpec((k, tn), lambda b, i, j: (0, j)),
            pl.BlockSpec((None, tm, tn), lambda b, i, j: (b, i, j)),
        ],
        out_specs=pl.BlockSpec((None, tm, tn), lambda b, i, j: (b, i, j)),
        out_shape=jax.ShapeDtypeStruct((nblk, rb, n), BF16),
        scratch_shapes=[pltpu.VMEM((tm, k), BF16)],
        compiler_params=_params(("arbitrary", "arbitrary", "arbitrary"), vmem),
        name="ssm_glu",
    )(y3, w, y3)


def _mm_merge_kernel(a_ref, s_ref, wa_ref, ws_ref, ga_ref, gs_ref, o_ref):
    pa = jnp.dot(a_ref[...], wa_ref[...], preferred_element_type=F32)
    ps = jnp.dot(s_ref[...], ws_ref[...], preferred_element_type=F32)
    o = jax.nn.sigmoid(ga_ref[...]) * pa + jax.nn.sigmoid(gs_ref[...]) * ps
    o_ref[...] = o.astype(o_ref.dtype)


def _mm_merge(attn3, ssm3, w_attn, w_ssm, gates3, d):
    nblk, rb, ka = attn3.shape
    ks = ssm3.shape[2]
    tm = _pick(rb, 1024, V7X_SUBLANES)
    tn = _pick(d, 512, V7X_LANES)
    gs_off = d // tn
    vmem = 2 * (tm * (ka + ks) * 2 + (ka + ks) * tn * 2 + 2 * tm * tn * 4 + tm * tn * 2) + 3 * tm * tn * 4
    return pl.pallas_call(
        _mm_merge_kernel,
        grid=(nblk, rb // tm, d // tn),
        in_specs=[
            pl.BlockSpec((None, tm, ka), lambda b, i, j: (b, i, 0)),
            pl.BlockSpec((None, tm, ks), lambda b, i, j: (b, i, 0)),
            pl.BlockSpec((ka, tn), lambda b, i, j: (0, j)),
            pl.BlockSpec((ks, tn), lambda b, i, j: (0, j)),
            pl.BlockSpec((None, tm, tn), lambda b, i, j: (b, i, j)),
            pl.BlockSpec((None, tm, tn), lambda b, i, j: (b, i, j + gs_off)),
        ],
        out_specs=pl.BlockSpec((None, tm, tn), lambda b, i, j: (b, i, j)),
        out_shape=jax.ShapeDtypeStruct((nblk, rb, d), BF16),
        compiler_params=_params(("arbitrary", "arbitrary", "arbitrary"), vmem),
        name="mixer_merge",
    )(attn3, ssm3, w_attn, w_ssm, gates3, gates3)


def _mm_resid_kernel(a_ref, w_ref, x_ref, g_ref, o_ref):
    acc = jnp.dot(a_ref[...], w_ref[...], preferred_element_type=F32)
    o_ref[...] = x_ref[...] + g_ref[...] * acc


def _mm_resid(a3, w, x3, gate, tm_pref, tn_pref):
    nblk, rb, k = a3.shape
    n = w.shape[1]
    tm = _pick(rb, tm_pref, V7X_SUBLANES)
    tn = _pick(n, tn_pref, V7X_LANES)
    vmem = 2 * (tm * k * 2 + k * tn * 2 + 2 * tm * tn * 4) + tm * tn * 4
    return pl.pallas_call(
        _mm_resid_kernel,
        grid=(nblk, rb // tm, n // tn),
        in_specs=[
            pl.BlockSpec((None, tm, k), lambda b, i, j: (b, i, 0)),
            pl.BlockSpec((k, tn), lambda b, i, j: (0, j)),
            pl.BlockSpec((None, tm, tn), lambda b, i, j: (b, i, j)),
            _row_param_spec(gate, tm, tn, True),
        ],
        out_specs=pl.BlockSpec((None, tm, tn), lambda b, i, j: (b, i, j)),
        out_shape=jax.ShapeDtypeStruct((nblk, rb, n), F32),
        compiler_params=_params(("arbitrary", "arbitrary", "arbitrary"), vmem),
        name="proj_residual",
    )(a3, w, x3, gate)


def _ffn_up_kernel(*refs, fresh, seq, tm):
    if fresh:
        (h_ref, halo_ref, wa_ref, wg_ref, cwa_ref, cwg_ref, cba_ref, cbg_ref,
         act_ref, sa_ref, sg_ref, lhs_ref) = refs

        @pl.when(pl.program_id(2) == 0)
        def _():
            first = pl.program_id(1) == 0
            halo = halo_ref[...]
            lhs_ref[0:V7X_SUBLANES, :] = jnp.where(first, jnp.zeros_like(halo), halo)
            lhs_ref[V7X_SUBLANES:, :] = h_ref[...]

        lhs = lhs_ref[...]
    else:
        (h_ref, wa_ref, wg_ref, cwa_ref, cwg_ref, cba_ref, cbg_ref,
         p1a_ref, p2a_ref, p1g_ref, p2g_ref, act_ref, sa_ref, sg_ref) = refs
        lhs = h_ref[...]

    def conv(w_ref, cw_ref, cb_ref, p1_ref, p2_ref, s_ref):
        up = jnp.dot(lhs, w_ref[...], preferred_element_type=F32)
        r1 = pltpu.roll(up, 1, 0)
        r2 = pltpu.roll(up, 2, 0)
        if fresh:
            s_ref[...] = up[tm:, :]
            up, r1, r2 = up[V7X_SUBLANES:], r1[V7X_SUBLANES:], r2[V7X_SUBLANES:]
        else:
            s_ref[...] = up
            tloc = lax.broadcasted_iota(jnp.int32, up.shape, 0) % seq
            r1 = jnp.where(tloc < 1, p1_ref[...], r1)
            r2 = jnp.where(tloc < 2, p2_ref[...], r2)
        cw = cw_ref[...]
        return cb_ref[...] + (cw[0:1] * r2 + cw[1:2] * r1 + cw[2:3] * up)

    if fresh:
        a = conv(wa_ref, cwa_ref, cba_ref, None, None, sa_ref)
        g = conv(wg_ref, cwg_ref, cbg_ref, None, None, sg_ref)
    else:
        a = conv(wa_ref, cwa_ref, cba_ref, p1a_ref, p2a_ref, sa_ref)
        g = conv(wg_ref, cwg_ref, cbg_ref, p1g_ref, p2g_ref, sg_ref)
    act_ref[...] = (jax.nn.silu(g) * a).astype(act_ref.dtype)


def _ffn_up(h3, w_up, w_conv, b_conv, d_ff, prev=None, seq=None):
    nblk, rb, k = h3.shape
    fresh = prev is None
    tn = _pick(d_ff, 512, V7X_LANES)
    goff = d_ff // tn
    cw = w_conv
    cb = b_conv.reshape(1, 2 * d_ff)
    w_specs = [
        pl.BlockSpec((k, tn), lambda b, i, j: (0, j)),
        pl.BlockSpec((k, tn), lambda b, i, j: (0, j + goff)),
        pl.BlockSpec((cw.shape[0], tn), lambda b, i, j: (0, j)),
        pl.BlockSpec((cw.shape[0], tn), lambda b, i, j: (0, j + goff)),
        pl.BlockSpec((1, tn), lambda b, i, j: (0, j)),
        pl.BlockSpec((1, tn), lambda b, i, j: (0, j + goff)),
    ]
    w_args = [w_up, w_up, cw, cw, cb, cb]
    if fresh:
        tm = _pick(rb, 1024, V7X_SUBLANES)
        hb = tm // V7X_SUBLANES
        in_specs = [
            pl.BlockSpec((None, tm, k), lambda b, i, j: (b, i, 0)),
            pl.BlockSpec((None, V7X_SUBLANES, k), lambda b, i, j: (b, jnp.maximum(i * hb - 1, 0), 0)),
        ] + w_specs
        args = [h3, h3] + w_args
        srows = V7X_SUBLANES
        scratch = [pltpu.VMEM((tm + V7X_SUBLANES, k), BF16)]
        state_spec = pl.BlockSpec((None, None, srows, tn), lambda b, i, j: (b, i, 0, j))
    else:
        tm = rb
        p1, p2 = prev
        in_specs = [pl.BlockSpec((None, tm, k), lambda b, i, j: (b, i, 0))] + w_specs + [
            pl.BlockSpec((None, tm, tn), lambda b, i, j: (b, i, j)),
            pl.BlockSpec((None, tm, tn), lambda b, i, j: (b, i, j)),
            pl.BlockSpec((None, tm, tn), lambda b, i, j: (b, i, j + goff)),
            pl.BlockSpec((None, tm, tn), lambda b, i, j: (b, i, j + goff)),
        ]
        args = [h3] + w_args + [p1, p2, p1, p2]
        srows = rb
        scratch = []
        state_spec = pl.BlockSpec((None, None, srows, tn), lambda b, i, j: (b, i, 0, j))
    vmem = (3 * (tm + 8) * k * 2 + 4 * k * tn * 2 + 10 * (tm + 8) * tn * 4
            + 2 * tm * tn * 2 + 8 * srows * tn * 4)
    return pl.pallas_call(
        functools.partial(_ffn_up_kernel, fresh=fresh, seq=seq, tm=tm),
        grid=(nblk, rb // tm, d_ff // tn),
        in_specs=in_specs,
        out_specs=[
            pl.BlockSpec((None, tm, tn), lambda b, i, j: (b, i, j)),
            state_spec,
            state_spec,
        ],
        out_shape=[
            jax.ShapeDtypeStruct((nblk, rb, d_ff), BF16),
            jax.ShapeDtypeStruct((nblk, rb // tm, srows, d_ff), F32),
            jax.ShapeDtypeStruct((nblk, rb // tm, srows, d_ff), F32),
        ],
        scratch_shapes=scratch,
        compiler_params=_params(("arbitrary", "arbitrary", "arbitrary"), vmem),
        name="convffn_up",
    )(*args)


def _beats(other, gate, other_is_lower):
    return jnp.where(other > gate, 1.0, 0.0) + jnp.where(other == gate, 1.0, 0.0) * other_is_lower


def _topk_mask(gate, n_valid, blk_iota):
    nblk = gate.shape[1]
    cnt = jnp.zeros_like(gate)
    for m in range(nblk):
        col = gate[:, m:m + 1]
        beats = _beats(col, gate, jnp.where(blk_iota > m, 1.0, 0.0))
        cnt = cnt + beats * jnp.where(m < n_valid, 1.0, 0.0)
    return jnp.where(cnt < MOBA_TOPK, 1.0, 0.0) * jnp.where(blk_iota < n_valid, 1.0, 0.0)


def _attn_fresh_kernel(slope_ref, q_ref, k_ref, v_ref, o_ref,
                       kbf_ref, vbf_ref, kmean_ref, m_ref, l_ref, acc_ref, *, nblk):
    g = pl.program_id(1)
    c = pl.program_id(2)
    blk = MOBA_BLOCK
    scale = HEAD_DIM ** -0.5

    @pl.when(c == 0)
    def _():
        k = k_ref[...]
        kbf_ref[...] = k.astype(BF16)
        vbf_ref[...] = v_ref[...].astype(BF16)
        for n in range(nblk):
            kmean_ref[n:n + 1, :] = jnp.mean(k[n * blk:(n + 1) * blk, :], axis=0, keepdims=True)

    row = lax.broadcasted_iota(jnp.int32, (blk, blk), 0)
    col = lax.broadcasted_iota(jnp.int32, (blk, blk), 1)
    d0 = (row - col).astype(F32)
    blk_iota = lax.broadcasted_iota(jnp.int32, (blk, nblk), 1)
    cur0 = pl.multiple_of(c * blk, blk)

    for hh in range(2):
        slope = slope_ref[2 * g + hh]
        q = q_ref[:, hh * HEAD_DIM:(hh + 1) * HEAD_DIM]
        qbf = q.astype(BF16)
        gate = lax.dot_general(q, kmean_ref[...], (((1,), (1,)), ((), ())),
                               precision=lax.Precision.HIGHEST, preferred_element_type=F32)
        sel = _topk_mask(gate, c, blk_iota)

        kb = kbf_ref[pl.ds(cur0, blk), :]
        s = lax.dot_general(qbf, kb, (((1,), (1,)), ((), ())), preferred_element_type=F32) * scale
        s = s - slope * d0
        s = jnp.where(d0 >= 0.0, s, NEG_INF)
        m0 = jnp.max(s, axis=-1, keepdims=True)
        p = jnp.exp(s - m0)
        m_ref[...] = m0
        l_ref[...] = jnp.sum(p, axis=-1, keepdims=True)
        acc_ref[...] = jnp.dot(p.astype(BF16), vbf_ref[pl.ds(cur0, blk), :], preferred_element_type=F32)

        for n in range(nblk - 1):
            @pl.when(n < c)
            def _(n=n):
                kb = kbf_ref[n * blk:(n + 1) * blk, :]
                s = lax.dot_general(qbf, kb, (((1,), (1,)), ((), ())), preferred_element_type=F32) * scale
                dist = d0 + ((c - n) * blk).astype(F32)
                s = s - slope * dist
                s = jnp.where(sel[:, n:n + 1] > 0.5, s, NEG_INF)
                m_old = m_ref[...]
                m_new = jnp.maximum(m_old, jnp.max(s, axis=-1, keepdims=True))
                alpha = jnp.exp(m_old - m_new)
                p = jnp.exp(s - m_new)
                l_ref[...] = alpha * l_ref[...] + jnp.sum(p, axis=-1, keepdims=True)
                acc_ref[...] = alpha * acc_ref[...] + jnp.dot(
                    p.astype(BF16), vbf_ref[n * blk:(n + 1) * blk, :], preferred_element_type=F32)
                m_ref[...] = m_new

        o = acc_ref[...] / l_ref[...]
        o_ref[:, hh * HEAD_DIM:(hh + 1) * HEAD_DIM] = o.astype(o_ref.dtype)


def _attn_fresh(z3, slopes, n_heads, kv_heads):
    b, t, _ = z3.shape
    assert n_heads == 2 * kv_heads and t % MOBA_BLOCK == 0
    nblk = t // MOBA_BLOCK
    q_w = n_heads * HEAD_DIM
    kcol = q_w // HEAD_DIM
    vcol = (q_w + kv_heads * HEAD_DIM) // HEAD_DIM
    blk = MOBA_BLOCK
    vmem = (2 * (blk * 2 * HEAD_DIM * 4 + 2 * t * HEAD_DIM * 4 + blk * 2 * HEAD_DIM * 2)
            + 2 * t * HEAD_DIM * 2 + 16 * blk * blk * 4)
    return pl.pallas_call(
        functools.partial(_attn_fresh_kernel, nblk=nblk),
        grid=(b, kv_heads, nblk),
        in_specs=[
            pl.BlockSpec(memory_space=pltpu.SMEM),
            pl.BlockSpec((None, blk, 2 * HEAD_DIM), lambda bb, g, c: (bb, c, g)),
            pl.BlockSpec((None, t, HEAD_DIM), lambda bb, g, c: (bb, 0, kcol + g)),
            pl.BlockSpec((None, t, HEAD_DIM), lambda bb, g, c: (bb, 0, vcol + g)),
        ],
        out_specs=pl.BlockSpec((None, blk, 2 * HEAD_DIM), lambda bb, g, c: (bb, c, g)),
        out_shape=jax.ShapeDtypeStruct((b, t, q_w), BF16),
        scratch_shapes=[
            pltpu.VMEM((t, HEAD_DIM), BF16),
            pltpu.VMEM((t, HEAD_DIM), BF16),
            pltpu.VMEM((nblk, HEAD_DIM), F32),
            pltpu.VMEM((blk, 1), F32),
            pltpu.VMEM((blk, 1), F32),
            pltpu.VMEM((blk, HEAD_DIM), F32),
        ],
        compiler_params=_params(("arbitrary", "arbitrary", "arbitrary"), vmem),
        name="moba_fresh",
    )(slopes, z3, z3, z3)


def _paged_scores_kernel(pt_ref, wq_ref, p0_ref, p1_ref, sc_ref, km_ref, *, page):
    n = pl.program_id(1)
    wq = wq_ref[...]
    k0 = p0_ref[...]
    k1 = p1_ref[...]
    sc_ref[0:page, :] = jnp.dot(k0.astype(BF16), wq, preferred_element_type=F32)
    sc_ref[page:2 * page, :] = jnp.dot(k1.astype(BF16), wq, preferred_element_type=F32)
    tot = jnp.sum(k0, axis=0, keepdims=True) + jnp.sum(k1, axis=0, keepdims=True)
    km_ref[pl.ds(n, 1), :] = tot * (1.0 / (2 * page))


def _paged_scores(page_table, wq_bf, cache, layer):
    s, n_pages = page_table.shape
    _, n_pool, page, kvh, hd = cache.shape
    assert 2 * page == MOBA_BLOCK
    kv_w = kvh * hd
    cache4 = cache.reshape(cache.shape[0], n_pool, page, kv_w)
    nblk = n_pages // 2
    past = n_pages * page
    vmem = 2 * (kv_w * V7X_LANES * 2 + 2 * page * kv_w * 4 + MOBA_BLOCK * V7X_LANES * 4
                + nblk * kv_w * 4) + 4 * page * kv_w * 4
    grid_spec = pltpu.PrefetchScalarGridSpec(
        num_scalar_prefetch=1,
        grid=(s, nblk),
        in_specs=[
            pl.BlockSpec((None, kv_w, V7X_LANES), lambda i, n, pt: (i, 0, 0)),
            pl.BlockSpec((None, None, page, kv_w), lambda i, n, pt: (layer, pt[i, 2 * n], 0, 0)),
            pl.BlockSpec((None, None, page, kv_w), lambda i, n, pt: (layer, pt[i, 2 * n + 1], 0, 0)),
        ],
        out_specs=[
            pl.BlockSpec((None, MOBA_BLOCK, V7X_LANES), lambda i, n, pt: (i, n, 0)),
            pl.BlockSpec((None, nblk, kv_w), lambda i, n, pt: (i, 0, 0)),
        ],
    )
    return pl.pallas_call(
        functools.partial(_paged_scores_kernel, page=page),
        grid_spec=grid_spec,
        out_shape=[
            jax.ShapeDtypeStruct((s, past, V7X_LANES), F32),
            jax.ShapeDtypeStruct((s, nblk, kv_w), F32),
        ],
        compiler_params=_params(("arbitrary", "arbitrary"), vmem),
        name="moba_paged_scores",
    )(page_table, wq_bf, cache4, cache4)


def _paged_attend_kernel(pt_ref, wq_ref, wqbf_ref, sc_ref, km_ref, knew_ref, vnew_ref,
                         slope_ref, tcol_ref, p0_ref, p1_ref, o_ref,
                         sel_ref, prob_ref, m_ref, linv_ref, pcur_ref, sq_ref, vpad_ref, acc_ref,
                         *, page, nblk, tnew, past):
    n = pl.program_id(1)
    blk = MOBA_BLOCK
    scale = HEAD_DIM ** -0.5
    lanes = V7X_LANES
    slope = slope_ref[...]
    tcol = tcol_ref[...]
    koff = lax.broadcasted_iota(jnp.int32, (blk, lanes), 0).astype(F32)

    def block_scores(i):
        r0 = pl.multiple_of(i * blk, blk)
        raw = sc_ref[pl.ds(r0, blk), :]
        dist = (tcol + (past - i * blk).astype(F32)) - koff
        s = raw * scale - slope * dist
        return jnp.where(sel_ref[pl.ds(i, 1), :] > 0.5, s, NEG_INF)

    @pl.when(n == 0)
    def _():
        gate = jnp.dot(km_ref[...], wq_ref[...], precision=lax.Precision.HIGHEST,
                       preferred_element_type=F32)
        riota = lax.broadcasted_iota(jnp.int32, (nblk, lanes), 0)
        cnt = jnp.zeros((nblk, lanes), F32)
        for m in range(nblk):
            cnt = cnt + _beats(gate[m:m + 1, :], gate, jnp.where(riota > m, 1.0, 0.0))
        sel_ref[...] = jnp.where(cnt < MOBA_TOPK, 1.0, 0.0)

        s_cur = jnp.dot(knew_ref[...].astype(BF16), wqbf_ref[...], preferred_element_type=F32)
        off = lax.broadcasted_iota(jnp.int32, (tnew, lanes), 0).astype(F32)
        s_cur = s_cur * scale - slope * (tcol - off)
        s_cur = jnp.where(off <= tcol, s_cur, NEG_INF)
        m0 = jnp.max(s_cur, axis=0, keepdims=True)

        def max_body(i, m):
            return jnp.maximum(m, jnp.max(block_scores(i), axis=0, keepdims=True))

        mx = lax.fori_loop(0, nblk, max_body, m0)
        m_ref[...] = mx
        p_cur = jnp.exp(s_cur - mx)

        def sum_body(i, l):
            p = jnp.exp(block_scores(i) - mx)
            prob_ref[pl.ds(pl.multiple_of(i * blk, blk), blk), :] = p
            return l + jnp.sum(p, axis=0, keepdims=True)

        l = lax.fori_loop(0, nblk, sum_body, jnp.sum(p_cur, axis=0, keepdims=True))
        linv = 1.0 / l
        linv_ref[...] = linv
        sq_ref[...] = jnp.zeros_like(sq_ref)
        sq_ref[0:tnew, :] = p_cur * linv
        vpad_ref[...] = jnp.zeros_like(vpad_ref)
        vpad_ref[0:tnew, :] = vnew_ref[...].astype(BF16)
        acc_ref[...] = jnp.dot(sq_ref[...].T.astype(BF16), vpad_ref[...], preferred_element_type=F32)

    r0 = pl.multiple_of(n * blk, blk)
    linv = linv_ref[...]
    pa = (prob_ref[pl.ds(r0, page), :] * linv).T.astype(BF16)
    pb = (prob_ref[pl.ds(r0 + page, page), :] * linv).T.astype(BF16)
    acc_ref[...] += (jnp.dot(pa, p0_ref[...].astype(BF16), preferred_element_type=F32)
                     + jnp.dot(pb, p1_ref[...].astype(BF16), preferred_element_type=F32))

    @pl.when(n == nblk - 1)
    def _():
        rows = o_ref.shape[1]
        for g in range(o_ref.shape[0]):
            o_ref[g] = acc_ref[g * rows:(g + 1) * rows, g * HEAD_DIM:(g + 1) * HEAD_DIM].astype(o_ref.dtype)


def _paged_attend(page_table, wq, wq_bf, scores, kmean, k_new, v_new, slope_col, t_col, cache, layer):
    s, n_pages = page_table.shape
    _, n_pool, page, kvh, hd = cache.shape
    kv_w = kvh * hd
    cache4 = cache.reshape(cache.shape[0], n_pool, page, kv_w)
    nblk = n_pages // 2
    past = n_pages * page
    tnew = k_new.shape[1]
    rows = 2 * tnew
    vmem = (2 * (kv_w * V7X_LANES * 6 + past * V7X_LANES * 4 + nblk * kv_w * 4 + 2 * page * kv_w * 4)
            + past * V7X_LANES * 4 + 3 * V7X_LANES * kv_w * 4 + 16 * MOBA_BLOCK * V7X_LANES * 4)
    grid_spec = pltpu.PrefetchScalarGridSpec(
        num_scalar_prefetch=1,
        grid=(s, nblk),
        in_specs=[
            pl.BlockSpec((None, kv_w, V7X_LANES), lambda i, n, pt: (i, 0, 0)),
            pl.BlockSpec((None, kv_w, V7X_LANES), lambda i, n, pt: (i, 0, 0)),
            pl.BlockSpec((None, past, V7X_LANES), lambda i, n, pt: (i, 0, 0)),
            pl.BlockSpec((None, nblk, kv_w), lambda i, n, pt: (i, 0, 0)),
            pl.BlockSpec((None, tnew, kv_w), lambda i, n, pt: (i, 0, 0)),
            pl.BlockSpec((None, tnew, kv_w), lambda i, n, pt: (i, 0, 0)),
            pl.BlockSpec((1, V7X_LANES), lambda i, n, pt: (0, 0)),
            pl.BlockSpec((1, V7X_LANES), lambda i, n, pt: (0, 0)),
            pl.BlockSpec((None, None, page, kv_w), lambda i, n, pt: (layer, pt[i, 2 * n], 0, 0)),
            pl.BlockSpec((None, None, page, kv_w), lambda i, n, pt: (layer, pt[i, 2 * n + 1], 0, 0)),
        ],
        out_specs=pl.BlockSpec((None, kvh, rows, hd), lambda i, n, pt: (i, 0, 0, 0)),
        scratch_shapes=[
            pltpu.VMEM((nblk, V7X_LANES), F32),
            pltpu.VMEM((past, V7X_LANES), F32),
            pltpu.VMEM((1, V7X_LANES), F32),
            pltpu.VMEM((1, V7X_LANES), F32),
            pltpu.VMEM((tnew, V7X_LANES), F32),
            pltpu.VMEM((V7X_LANES, V7X_LANES), F32),
            pltpu.VMEM((V7X_LANES, kv_w), BF16),
            pltpu.VMEM((V7X_LANES, kv_w), F32),
        ],
    )
    return pl.pallas_call(
        functools.partial(_paged_attend_kernel, page=page, nblk=nblk, tnew=tnew, past=past),
        grid_spec=grid_spec,
        out_shape=jax.ShapeDtypeStruct((s, kvh, rows, hd), BF16),
        compiler_params=_params(("arbitrary", "arbitrary"), vmem),
        name="moba_paged_attend",
    )(page_table, wq, wq_bf, scores, kmean, k_new, v_new, slope_col, t_col, cache4, cache4)


def _ssm_prep_kernel(lr_ref, li_ref, ldt_ref, brt_ref, bit_ref, are_ref, aim_ref, bbr_ref, bbi_ref):
    lr = lr_ref[...]
    li = li_ref[...]
    dt = jnp.exp(ldt_ref[...])
    mag = jnp.exp(lr * dt)
    a_re = mag * jnp.cos(li * dt)
    a_im = mag * jnp.sin(li * dt)
    den = lr * lr + li * li
    q_re = ((a_re - 1.0) * lr + a_im * li) / den
    q_im = (a_im * lr - (a_re - 1.0) * li) / den
    are_ref[...] = a_re
    aim_ref[...] = a_im
    br = brt_ref[...]
    bi = bit_ref[...]
    bbr_ref[...] = q_re * br - q_im * bi
    bbi_ref[...] = q_re * bi + q_im * br


def _ssm_prep(lam_re, lam_im, log_dt, b_re, b_im):
    depth, g, p = lam_re.shape
    i = b_re.shape[3]
    brt = jnp.swapaxes(b_re, 2, 3)
    bit = jnp.swapaxes(b_im, 2, 3)
    vec = pl.BlockSpec((None, g, 1, p), lambda l: (l, 0, 0, 0))
    mat = pl.BlockSpec((None, g, i, p), lambda l: (l, 0, 0, 0))
    return pl.pallas_call(
        _ssm_prep_kernel,
        grid=(depth,),
        in_specs=[vec, vec, pl.BlockSpec((None, g, 1, 1), lambda l: (l, 0, 0, 0)), mat, mat],
        out_specs=[vec, vec, mat, mat],
        out_shape=[
            jax.ShapeDtypeStruct((depth, g, 1, p), F32),
            jax.ShapeDtypeStruct((depth, g, 1, p), F32),
            jax.ShapeDtypeStruct((depth, g, i, p), F32),
            jax.ShapeDtypeStruct((depth, g, i, p), F32),
        ],
        compiler_params=_params(("arbitrary",), 32 << 20),
        name="s5_discretise",
    )(lam_re.reshape(depth, g, 1, p), lam_im.reshape(depth, g, 1, p),
      log_dt.reshape(depth, g, 1, 1), brt, bit)


def _ssm_scan_kernel(u_ref, ire_ref, iim_ref, are_ref, aim_ref, bre_ref, bim_ref,
                     cre_ref, cim_ref, d_ref, *rest, with_y, n_chunks, tau_b):
    if with_y:
        y_ref, ere_ref, eim_ref, hre_ref, him_ref, bure_ref, buim_ref = rest
    else:
        ere_ref, eim_ref, hre_ref, him_ref, bure_ref, buim_ref = rest
    k = pl.program_id(1)

    @pl.when(k == 0)
    def _():
        hre_ref[...] = ire_ref[...]
        him_ref[...] = iim_ref[...]

    def chunk(c, carry):
        cu = pl.multiple_of(c * CHUNK_IN, CHUNK_IN)
        cs = pl.multiple_of(c * CHUNK_ST, CHUNK_ST)
        u_c = u_ref[:, pl.ds(cu, CHUNK_IN)]
        ub = u_c.astype(BF16)
        bure_ref[...] = jnp.dot(ub, bre_ref[c], preferred_element_type=F32)
        buim_ref[...] = jnp.dot(ub, bim_ref[c], preferred_element_type=F32)
        ar = jnp.broadcast_to(are_ref[:, pl.ds(cs, CHUNK_ST)], (N_SEG, CHUNK_ST))
        ai = jnp.broadcast_to(aim_ref[:, pl.ds(cs, CHUNK_ST)], (N_SEG, CHUNK_ST))
        hr = hre_ref[:, pl.ds(cs, CHUNK_ST)]
        hi = him_ref[:, pl.ds(cs, CHUNK_ST)]
        for t in range(tau_b):
            rs = slice(t * N_SEG, (t + 1) * N_SEG)
            nr = ar * hr - ai * hi + bure_ref[rs, :]
            ni = ar * hi + ai * hr + buim_ref[rs, :]
            hr, hi = nr, ni
            if with_y:
                bure_ref[rs, :] = hr
                buim_ref[rs, :] = hi
        hre_ref[:, pl.ds(cs, CHUNK_ST)] = hr
        him_ref[:, pl.ds(cs, CHUNK_ST)] = hi
        if with_y:
            ych = (jnp.dot(bure_ref[...].astype(BF16), cre_ref[c], preferred_element_type=F32)
                   - jnp.dot(buim_ref[...].astype(BF16), cim_ref[c], preferred_element_type=F32))
            y = ych + d_ref[:, pl.ds(cu, CHUNK_IN)] * u_c
            y_ref[:, pl.ds(cu, CHUNK_IN)] = jax.nn.gelu(y)
        return carry

    lax.fori_loop(0, n_chunks, chunk, 0)

    @pl.when(k == pl.num_programs(1) - 1)
    def _():
        ere_ref[...] = hre_ref[...]
        eim_ref[...] = him_ref[...]


def _ssm_scan(u3, init_re, init_im, a_re, a_im, bre, bim, cre, cim, d_skip, with_y):
    nb, lr, s_w = u3.shape
    ns = a_re.shape[1]
    n_chunks = bre.shape[0]
    steps = lr // N_SEG
    tau_b = _pick(steps, 32, 1)
    rb = tau_b * N_SEG
    st_spec = pl.BlockSpec((None, N_SEG, ns), lambda b, k: (b, 0, 0))
    full = lambda shape: pl.BlockSpec(shape, lambda b, k: (0,) * len(shape))
    u_spec = pl.BlockSpec((None, rb, s_w), lambda b, k: (b, k, 0))
    out_specs = [st_spec, st_spec]
    out_shape = [jax.ShapeDtypeStruct((nb, N_SEG, ns), F32)] * 2
    if with_y:
        out_specs = [u_spec] + out_specs
        out_shape = [jax.ShapeDtypeStruct((nb, lr, s_w), F32)] + out_shape
    vmem = (4 * rb * s_w * 4 + 8 * N_SEG * ns * 4 + 4 * ns * 4
            + 2 * 4 * n_chunks * CHUNK_IN * CHUNK_ST * 2 + 8 * rb * CHUNK_ST * 4)
    return pl.pallas_call(
        functools.partial(_ssm_scan_kernel, with_y=with_y, n_chunks=n_chunks, tau_b=tau_b),
        grid=(nb, steps // tau_b),
        in_specs=[u_spec, st_spec, st_spec, full((1, ns)), full((1, ns)),
                  full(bre.shape), full(bim.shape), full(cre.shape), full(cim.shape),
                  full((1, s_w))],
        out_specs=out_specs,
        out_shape=out_shape,
        scratch_shapes=[
            pltpu.VMEM((N_SEG, ns), F32),
            pltpu.VMEM((N_SEG, ns), F32),
            pltpu.VMEM((rb, CHUNK_ST), F32),
            pltpu.VMEM((rb, CHUNK_ST), F32),
        ],
        compiler_params=_params(("arbitrary", "arbitrary"), vmem),
        name="s5_scan" if with_y else "s5_segment_ends",
    )(u3, init_re, init_im, a_re, a_im, bre, bim, cre, cim, d_skip)


def _ssm_carry_kernel(ere_ref, eim_ref, are_ref, aim_ref, ire_ref, iim_ref, fre_ref, fim_ref, *, seg_len):
    br = are_ref[...]
    bi = aim_ref[...]
    pr = jnp.ones_like(br)
    pi = jnp.zeros_like(br)
    e = seg_len
    while e:
        if e & 1:
            pr, pi = pr * br - pi * bi, pr * bi + pi * br
        br, bi = br * br - bi * bi, 2.0 * br * bi
        e >>= 1
    hr = jnp.zeros_like(pr)
    hi = jnp.zeros_like(pr)
    for j in range(N_SEG):
        ire_ref[j:j + 1, :] = hr
        iim_ref[j:j + 1, :] = hi
        er = ere_ref[j:j + 1, :]
        ei = eim_ref[j:j + 1, :]
        hr, hi = pr * hr - pi * hi + er, pr * hi + pi * hr + ei
    fre_ref[...] = hr
    fim_ref[...] = hi


def _ssm_carry(end_re, end_im, a_re, a_im, seg_len):
    nb, _, ns = end_re.shape
    st = pl.BlockSpec((None, N_SEG, ns), lambda b: (b, 0, 0))
    vec = pl.BlockSpec((1, ns), lambda b: (0, 0))
    fin = pl.BlockSpec((None, 1, ns), lambda b: (b, 0, 0))
    return pl.pallas_call(
        functools.partial(_ssm_carry_kernel, seg_len=seg_len),
        grid=(nb,),
        in_specs=[st, st, vec, vec],
        out_specs=[st, st, fin, fin],
        out_shape=[jax.ShapeDtypeStruct((nb, N_SEG, ns), F32)] * 2
        + [jax.ShapeDtypeStruct((nb, 1, ns), F32)] * 2,
        compiler_params=_params(("arbitrary",), 32 << 20),
        name="s5_segment_carry",
    )(end_re, end_im, a_re, a_im)


def _block_diag_in(bbt):
    g, i, p = bbt.shape
    nc = g // GROUPS_PER_CHUNK
    eye = jnp.eye(GROUPS_PER_CHUNK, dtype=bbt.dtype)
    x = bbt.reshape(nc, GROUPS_PER_CHUNK, i, p)
    out = x[:, :, :, None, :] * eye[None, :, None, :, None]
    return out.reshape(nc, GROUPS_PER_CHUNK * i, GROUPS_PER_CHUNK * p).astype(BF16)


def _block_diag_out(c):
    g, i, p = c.shape
    nc = g // GROUPS_PER_CHUNK
    eye = jnp.eye(GROUPS_PER_CHUNK, dtype=c.dtype)
    x = jnp.swapaxes(c.reshape(nc, GROUPS_PER_CHUNK, i, p), 2, 3)
    out = x[:, :, :, None, :] * eye[None, :, None, :, None]
    return out.reshape(nc, GROUPS_PER_CHUNK * p, GROUPS_PER_CHUNK * i).astype(BF16)


def kernel(x_prompt, x_sample, c_prompt, c_sample, cache_k, cache_v, state_ssm_re, state_ssm_im,
           state_conv, page_table, w_ada, b_ada, norm_attn, w_in, w_attn_proj, lam_re, lam_im,
           log_dt, ssm_b_re, ssm_b_im, ssm_c_re, ssm_c_im, ssm_d, w_glu, w_ssm_proj, w_out,
           norm_ffn, w_up, w_conv, b_conv, w_down, norm_final):
    depth = w_ada.shape[0]
    nb, t, d = x_prompt.shape
    ns_seq, ts, _ = x_sample.shape
    q_w = w_attn_proj.shape[1]
    n_heads = q_w // HEAD_DIM
    kv_heads = cache_k.shape[3]
    kv_w = kv_heads * HEAD_DIM
    s_w = ssm_d.shape[1]
    n_groups = lam_re.shape[1]
    n_state = n_groups * S5_STATE
    d_ff = w_down.shape[1]
    n_pages = page_table.shape[1]
    page = cache_k.shape[2]
    past = n_pages * page
    assert ns_seq == N_SEG and t % (N_SEG * N_SEG) == 0 and n_groups % GROUPS_PER_CHUNK == 0
    n_col = kv_heads * 2 * ts
    assert n_col <= V7X_LANES and n_heads == 2 * kv_heads
    assert ssm_b_re.shape[2:] == (S5_STATE, S5_GROUP)
    u_col = q_w + 2 * kv_w
    g_col = u_col + s_w

    n_c = nb + ns_seq
    c_rows = -(-n_c // V7X_SUBLANES) * V7X_SUBLANES
    c_all = jnp.concatenate([c_prompt, c_sample, jnp.zeros((c_rows - n_c, d), F32)], axis=0)
    mod = _adaln(c_all, w_ada, b_ada)

    a_re_all, a_im_all, bbr_all, bbi_all = _ssm_prep(lam_re, lam_im, log_dt, ssm_b_re, ssm_b_im)
    slopes = 2.0 ** (-8.0 * jnp.arange(1, n_heads + 1, dtype=F32) / n_heads)

    col = jnp.arange(V7X_LANES)
    col_head = jnp.minimum(2 * (col // (2 * ts)) + (col // ts) % 2, n_heads - 1)
    slope_col = slopes[col_head].reshape(1, V7X_LANES)
    t_col = (col % ts).astype(F32).reshape(1, V7X_LANES)

    xp = x_prompt
    xs = x_sample.reshape(1, ns_seq * ts, d)
    seg = t // N_SEG
    keep = w_conv.shape[1] - 1
    assert keep == 2 and past % MOBA_BLOCK == 0 and ts <= MOBA_BLOCK
    outs ={k: [] for k in ("kp", "vp", "ks", "vs", "hpr", "hpi", "hsr", "hsi", "cp", "cs")}

    for l in range(depth):
        w_in_bf = w_in[l].astype(BF16)
        w_attn_bf = w_attn_proj[l].astype(BF16)
        w_glu_bf = w_glu[l].astype(BF16)
        w_ssm_bf = w_ssm_proj[l].astype(BF16)
        w_out_bf = w_out[l].astype(BF16)
        w_up_bf = w_up[l].astype(BF16)
        w_down_bf = w_down[l].astype(BF16)
        a_re = a_re_all[l].reshape(1, n_state)
        a_im = a_im_all[l].reshape(1, n_state)
        bre = _block_diag_in(bbr_all[l])
        bim = _block_diag_in(bbi_all[l])
        cre = _block_diag_out(ssm_c_re[l])
        cim = _block_diag_out(ssm_c_im[l])
        d_skip = ssm_d[l].reshape(1, s_w)
        mod_l = mod[l].reshape(c_rows, 6, d)

        def mods_for(lo, n, rep):
            m = mod_l[lo:lo + n]
            if rep == 1:
                return [m[:, i].reshape(n, 1, d) for i in range(6)]
            return [jnp.repeat(m[:, i], rep, axis=0).reshape(1, n * rep, d) for i in range(6)]

        sh1, sc1, g1, sh2, sc2, g2 = mods_for(0, nb, 1)
        h = _rmsnorm(xp, norm_attn[l], sc1, sh1)
        z = _mm_plain(h, w_in_bf, 0, g_col, F32)
        gates = _mm_plain(h, w_in_bf, g_col, 2 * d, F32)
        attn = _attn_fresh(z, slopes, n_heads, kv_heads)
        u = z[:, :, u_col:g_col]
        u_perm = u.reshape(nb, N_SEG, seg, s_w).swapaxes(1, 2).reshape(nb, t, s_w)
        zero_st = jnp.zeros((nb, N_SEG, n_state), F32)
        end_re, end_im = _ssm_scan(u_perm, zero_st, zero_st, a_re, a_im, bre, bim, cre, cim, d_skip, False)
        ini_re, ini_im, fin_re, fin_im = _ssm_carry(end_re, end_im, a_re, a_im, seg)
        y_perm, _, _ = _ssm_scan(u_perm, ini_re, ini_im, a_re, a_im, bre, bim, cre, cim, d_skip, True)
        y = y_perm.reshape(nb, seg, N_SEG, s_w).swapaxes(1, 2).reshape(nb, t, s_w)
        ssm = _mm_glu(y, w_glu_bf)
        merged = _mm_merge(attn, ssm, w_attn_bf, w_ssm_bf, gates, d)
        xp = _mm_resid(merged, w_out_bf, xp, g1, 1024, 512)
        h2 = _rmsnorm(xp, norm_ffn[l], sc2, sh2)
        act, tail_a, tail_g = _ffn_up(h2, w_up_bf, w_conv[l], b_conv[l], d_ff)
        xp = _mm_resid(act, w_down_bf, xp, g2, 512, 256)
        outs["kp"].append(z[:, :, q_w:q_w + kv_w].reshape(nb, t, kv_heads, HEAD_DIM))
        outs["vp"].append(z[:, :, q_w + kv_w:u_col].reshape(nb, t, kv_heads, HEAD_DIM))
        outs["hpr"].append(fin_re.reshape(nb, n_groups, S5_STATE))
        outs["hpi"].append(fin_im.reshape(nb, n_groups, S5_STATE))
        keep = w_conv.shape[1] - 1
        outs["cp"].append(jnp.concatenate([tail_a[:, -1], tail_g[:, -1]], axis=-1)[:, V7X_SUBLANES - keep:, :])

        sh1, sc1, g1, sh2, sc2, g2 = mods_for(nb, ns_seq, ts)
        h = _rmsnorm(xs, norm_attn[l], sc1, sh1)
        z = _mm_plain(h, w_in_bf, 0, g_col, F32)
        gates = _mm_plain(h, w_in_bf, g_col, 2 * d, F32)
        zs = z.reshape(ns_seq, ts, g_col)
        q_s = zs[:, :, :q_w].reshape(ns_seq, ts, kv_heads, 2, HEAD_DIM)
        k_new = zs[:, :, q_w:q_w + kv_w]
        v_new = zs[:, :, q_w + kv_w:u_col]
        eye = jnp.eye(kv_heads, dtype=F32)
        wq = (jnp.transpose(q_s, (0, 2, 4, 3, 1))[:, :, :, None, :, :]
              * eye[None, :, None, :, None, None]).reshape(ns_seq, kv_w, n_col)
        wq = jnp.pad(wq, ((0, 0), (0, 0), (0, V7X_LANES - n_col)))
        wq_bf = wq.astype(BF16)
        scores, kmean = _paged_scores(page_table, wq_bf, cache_k, l)
        o_s = _paged_attend(page_table, wq, wq_bf, scores, kmean, k_new, v_new,
                            slope_col, t_col, cache_v, l)
        attn = jnp.transpose(o_s.reshape(ns_seq, kv_heads, 2, ts, HEAD_DIM), (0, 3, 1, 2, 4))
        attn = attn.reshape(1, ns_seq * ts, q_w)
        u = zs[:, :, u_col:g_col]
        u_perm = jnp.swapaxes(u, 0, 1).reshape(1, ts * ns_seq, s_w)
        st_re = state_ssm_re[l].reshape(1, ns_seq, n_state)
        st_im = state_ssm_im[l].reshape(1, ns_seq, n_state)
        y_perm, e_re, e_im = _ssm_scan(u_perm, st_re, st_im, a_re, a_im, bre, bim, cre, cim, d_skip, True)
        y = jnp.swapaxes(y_perm.reshape(ts, ns_seq, s_w), 0, 1).reshape(1, ns_seq * ts, s_w)
        ssm = _mm_glu(y, w_glu_bf)
        merged = _mm_merge(attn, ssm, w_attn_bf, w_ssm_bf, gates, d)
        xs = _mm_resid(merged, w_out_bf, xs, g1, 1024, 1024)
        h2 = _rmsnorm(xs, norm_ffn[l], sc2, sh2)
        keep = w_conv.shape[1] - 1
        cprev = state_conv[l]
        prev1 = jnp.concatenate([cprev[:, keep - 1:keep], jnp.zeros((ns_seq, ts - 1, 2 * d_ff), F32)], axis=1)
        prev2 = jnp.concatenate([cprev[:, keep - 2:keep], jnp.zeros((ns_seq, ts - 2, 2 * d_ff), F32)], axis=1)
        prev1 = prev1.reshape(1, ns_seq * ts, 2 * d_ff)
        prev2 = prev2.reshape(1, ns_seq * ts, 2 * d_ff)
        act, up_a, up_g = _ffn_up(h2, w_up_bf, w_conv[l], b_conv[l], d_ff, prev=(prev1, prev2), seq=ts)
        xs = _mm_resid(act, w_down_bf, xs, g2, 512, 512)
        outs["ks"].append(k_new.reshape(ns_seq, ts, kv_heads, HEAD_DIM))
        outs["vs"].append(v_new.reshape(ns_seq, ts, kv_heads, HEAD_DIM))
        outs["hsr"].append(e_re.reshape(ns_seq, n_groups, S5_STATE))
        outs["hsi"].append(e_im.reshape(ns_seq, n_groups, S5_STATE))
        up_full = jnp.concatenate([up_a, up_g], axis=-1).reshape(ns_seq, ts, 2 * d_ff)
        outs["cs"].append(up_full[:, ts - keep:, :])

    y_prompt = _rmsnorm(xp, norm_final, out_dtype=F32)
    y_sample = _rmsnorm(xs, norm_final, out_dtype=F32).reshape(ns_seq, ts, d)
    st = lambda k: jnp.stack(outs[k])
    return (y_prompt, y_sample, st("kp"), st("vp"), st("ks"), st("vs"),
            st("hpr"), st("hpi"), st("hsr"), st("hsi"), st("cp"), st("cs"))
```

```python
import functools
import math

import jax
import jax.numpy as jnp
from jax import lax
from jax.experimental import pallas as pl
from jax.experimental.pallas import tpu as pltpu

F32 = jnp.float32
BF16 = jnp.bfloat16

V7X_LANES = 128
V7X_SUBLANES = 8
V7X_VMEM_BYTES = 64 * 2**20
VMEM_BUDGET = V7X_VMEM_BYTES - 8 * 2**20

HEAD_DIM = 128
MOBA_BLOCK = 256
MOBA_TOPK = 3
S5_GROUP = 16
S5_STATE = 64
GROUPS_PER_CHUNK = 8
CHUNK_IN = GROUPS_PER_CHUNK * S5_GROUP
CHUNK_ST = GROUPS_PER_CHUNK * S5_STATE
N_SEG = V7X_SUBLANES
PAGES_PER_STEP = 8
EPS = 1e-6
NEG_INF = float("-inf")
MASK_BIAS = -1e30


def _pick(dim, pref, align):
    t = min(pref, dim)
    t -= t % align
    while t >= align:
        if dim % t == 0:
            return t
        t -= align
    return dim


def _params(sem, vmem_bytes):
    limit = int(min(max(vmem_bytes * 5 // 4 + (4 << 20), 32 << 20), VMEM_BUDGET))
    return pltpu.CompilerParams(dimension_semantics=sem, vmem_limit_bytes=limit)


def _ada_kernel(c_ref, w_ref, b_ref, o_ref):
    s = jax.nn.silu(c_ref[...]).astype(BF16)
    w = w_ref[...].astype(BF16)
    o_ref[...] = jnp.dot(s, w, preferred_element_type=F32) + b_ref[...]


def _adaln(c_all, w_ada, b_ada):
    depth, d, n = w_ada.shape
    rows = c_all.shape[0]
    tn = _pick(n, 512, V7X_LANES)
    vmem = 2 * d * tn * 4 + d * tn * 2 + 4 * rows * (d + tn) * 4
    return pl.pallas_call(
        _ada_kernel,
        grid=(depth, n // tn),
        in_specs=[
            pl.BlockSpec((rows, d), lambda l, j: (0, 0)),
            pl.BlockSpec((None, d, tn), lambda l, j: (l, 0, j)),
            pl.BlockSpec((None, 1, tn), lambda l, j: (l, 0, j)),
        ],
        out_specs=pl.BlockSpec((None, rows, tn), lambda l, j: (l, 0, j)),
        out_shape=jax.ShapeDtypeStruct((depth, rows, n), F32),
        compiler_params=_params(("arbitrary", "arbitrary"), vmem),
        name="adaln",
    )(c_all, w_ada, b_ada.reshape(depth, 1, n))


def _norm_kernel(x_ref, g_ref, *rest, modulate):
    if modulate:
        sc_ref, sh_ref, o_ref = rest
    else:
        (o_ref,) = rest
    x = x_ref[...]
    y = x * lax.rsqrt(jnp.mean(x * x, axis=-1, keepdims=True) + EPS)
    y = y * g_ref[...]
    if modulate:
        y = y * (1.0 + sc_ref[...]) + sh_ref[...]
    o_ref[...] = y.astype(o_ref.dtype)


def _row_param_spec(p, tr, tn, col_blocked):
    shared = p.shape[1] == 1
    rows = 1 if shared else tr
    if col_blocked:
        if shared:
            return pl.BlockSpec((None, rows, tn), lambda b, i, j: (b, 0, j))
        return pl.BlockSpec((None, rows, tn), lambda b, i, j: (b, i, j))
    if shared:
        return pl.BlockSpec((None, rows, tn), lambda b, i: (b, 0, 0))
    return pl.BlockSpec((None, rows, tn), lambda b, i: (b, i, 0))


def _rmsnorm(x3, gamma, scale=None, shift=None, out_dtype=BF16):
    nblk, rb, d = x3.shape
    tr = _pick(rb, 256, V7X_SUBLANES)
    modulate = scale is not None
    in_specs = [
        pl.BlockSpec((None, tr, d), lambda b, i: (b, i, 0)),
        pl.BlockSpec((1, d), lambda b, i: (0, 0)),
    ]
    args = [x3, gamma.reshape(1, d)]
    if modulate:
        in_specs += [_row_param_spec(scale, tr, d, False), _row_param_spec(shift, tr, d, False)]
        args += [scale, shift]
    vmem = 2 * tr * d * (4 + 4) + 6 * tr * d * 4
    return pl.pallas_call(
        functools.partial(_norm_kernel, modulate=modulate),
        grid=(nblk, rb // tr),
        in_specs=in_specs,
        out_specs=pl.BlockSpec((None, tr, d), lambda b, i: (b, i, 0)),
        out_shape=jax.ShapeDtypeStruct((nblk, rb, d), out_dtype),
        compiler_params=_params(("arbitrary", "arbitrary"), vmem),
        name="rmsnorm_mod",
    )(*args)


def _mm_plain_kernel(a_ref, w_ref, o_ref):
    o_ref[...] = jnp.dot(a_ref[...], w_ref[...], preferred_element_type=F32).astype(o_ref.dtype)


def _mm_plain(a3, w, col0, ncols, out_dtype, tm_pref=1024, tn_pref=1024):
    nblk, rb, k = a3.shape
    tm = _pick(rb, tm_pref, V7X_SUBLANES)
    tn = _pick(math.gcd(ncols, col0) if col0 else ncols, tn_pref, V7X_LANES)
    cb = col0 // tn
    osz = jnp.dtype(out_dtype).itemsize
    vmem = 2 * (tm * k * 2 + k * tn * 2 + tm * tn * osz) + tm * tn * 4
    return pl.pallas_call(
        _mm_plain_kernel,
        grid=(nblk, rb // tm, ncols // tn),
        in_specs=[
            pl.BlockSpec((None, tm, k), lambda b, i, j: (b, i, 0)),
            pl.BlockSpec((k, tn), lambda b, i, j: (0, j + cb)),
        ],
        out_specs=pl.BlockSpec((None, tm, tn), lambda b, i, j: (b, i, j)),
        out_shape=jax.ShapeDtypeStruct((nblk, rb, ncols), out_dtype),
        compiler_params=_params(("arbitrary", "arbitrary", "arbitrary"), vmem),
        name="proj",
    )(a3, w)


def _mm_glu_kernel(y_ref, w_ref, yt_ref, o_ref, ybf_ref):
    @pl.when(pl.program_id(2) == 0)
    def _():
        ybf_ref[...] = y_ref[...].astype(BF16)

    acc = jnp.dot(ybf_ref[...], w_ref[...], preferred_element_type=F32)
    o_ref[...] = (yt_ref[...] * jax.nn.sigmoid(acc)).astype(o_ref.dtype)


def _mm_glu(y3, w):
    nblk, rb, k = y3.shape
    n = w.shape[1]
    tm = _pick(rb, 1024, V7X_SUBLANES)
    tn = _pick(n, 512, V7X_LANES)
    vmem = 2 * (tm * k * 4 + k * tn * 2 + tm * tn * 4 + tm * tn * 2) + tm * k * 2 + tm * tn * 4
    return pl.pallas_call(
        _mm_glu_kernel,
        grid=(nblk, rb // tm, n // tn),
        in_specs=[
            pl.BlockSpec((None, tm, k), lambda b, i, j: (b, i, 0)),
            pl.BlockSpec((k, tn), lambda b, i, j: (0, j)),
            pl.BlockSpec((None, tm, tn), lambda b, i, j: (b, i, j)),
        ],
        out_specs=pl.BlockSpec((None, tm, tn), lambda b, i, j: (b, i, j)),
        out_shape=jax.ShapeDtypeStruct((nblk, rb, n), BF16),
        scratch_shapes=[pltpu.VMEM((tm, k), BF16)],
        compiler_params=_params(("arbitrary", "arbitrary", "arbitrary"), vmem),
        name="ssm_glu",
    )(y3, w, y3)


def _mm_merge_kernel(a_ref, s_ref, wa_ref, ws_ref, ga_ref, gs_ref, o_ref):
    pa = jnp.dot(a_ref[...], wa_ref[...], preferred_element_type=F32)
    ps = jnp.dot(s_ref[...], ws_ref[...], preferred_element_type=F32)
    o = jax.nn.sigmoid(ga_ref[...]) * pa + jax.nn.sigmoid(gs_ref[...]) * ps
    o_ref[...] = o.astype(o_ref.dtype)


def _mm_merge(attn3, ssm3, w_attn, w_ssm, gates3, d):
    nblk, rb, ka = attn3.shape
    ks = ssm3.shape[2]
    tm = _pick(rb, 1024, V7X_SUBLANES)
    tn = _pick(d, 512, V7X_LANES)
    gs_off = d // tn
    vmem = 2 * (tm * (ka + ks) * 2 + (ka + ks) * tn * 2 + 2 * tm * tn * 4 + tm * tn * 2) + 3 * tm * tn * 4
    return pl.pallas_call(
        _mm_merge_kernel,
        grid=(nblk, rb // tm, d // tn),
        in_specs=[
            pl.BlockSpec((None, tm, ka), lambda b, i, j: (b, i, 0)),
            pl.BlockSpec((None, tm, ks), lambda b, i, j: (b, i, 0)),
            pl.BlockSpec((ka, tn), lambda b, i, j: (0, j)),
            pl.BlockSpec((ks, tn), lambda b, i, j: (0, j)),
            pl.BlockSpec((None, tm, tn), lambda b, i, j: (b, i, j)),
            pl.BlockSpec((None, tm, tn), lambda b, i, j: (b, i, j + gs_off)),
        ],
        out_specs=pl.BlockSpec((None, tm, tn), lambda b, i, j: (b, i, j)),
        out_shape=jax.ShapeDtypeStruct((nblk, rb, d), BF16),
        compiler_params=_params(("arbitrary", "arbitrary", "arbitrary"), vmem),
        name="mixer_merge",
    )(attn3, ssm3, w_attn, w_ssm, gates3, gates3)


def _mm_resid_kernel(a_ref, w_ref, x_ref, g_ref, o_ref):
    acc = jnp.dot(a_ref[...], w_ref[...], preferred_element_type=F32)
    o_ref[...] = x_ref[...] + g_ref[...] * acc


def _mm_resid(a3, w, x3, gate, tm_pref, tn_pref, lhs_buffers=2):
    nblk, rb, k = a3.shape
    n = w.shape[1]
    tm = _pick(rb, tm_pref, V7X_SUBLANES)
    tn = _pick(n, tn_pref, V7X_LANES)
    vmem = lhs_buffers * tm * k * 2 + 2 * (k * tn * 2 + 2 * tm * tn * 4) + tm * tn * 4
    return pl.pallas_call(
        _mm_resid_kernel,
        grid=(nblk, rb // tm, n // tn),
        in_specs=[
            pl.BlockSpec((None, tm, k), lambda b, i, j: (b, i, 0), pipeline_mode=pl.Buffered(lhs_buffers)),
            pl.BlockSpec((k, tn), lambda b, i, j: (0, j)),
            pl.BlockSpec((None, tm, tn), lambda b, i, j: (b, i, j)),
            _row_param_spec(gate, tm, tn, True),
        ],
        out_specs=pl.BlockSpec((None, tm, tn), lambda b, i, j: (b, i, j)),
        out_shape=jax.ShapeDtypeStruct((nblk, rb, n), F32),
        compiler_params=_params(("arbitrary", "arbitrary", "arbitrary"), vmem),
        name="proj_residual",
    )(a3, w, x3, gate)


def _ffn_up_kernel(*refs, fresh, seq, tm):
    halo_rows = V7X_SUBLANES
    if fresh:
        (h_ref, halo_ref, wa_ref, wg_ref, cwa_ref, cwg_ref, cba_ref, cbg_ref,
         act_ref, sa_ref, sg_ref, lhs_ref) = refs

        @pl.when(pl.program_id(2) == 0)
        def _():
            first = pl.program_id(1) == 0
            halo = halo_ref[...]
            lhs_ref[0:halo_rows, :] = jnp.where(first, jnp.zeros_like(halo), halo)
            lhs_ref[halo_rows:, :] = h_ref[...]
    else:
        (h_ref, wa_ref, wg_ref, cwa_ref, cwg_ref, cba_ref, cbg_ref,
         p1a_ref, p2a_ref, p1g_ref, p2g_ref, act_ref, sa_ref, sg_ref) = refs

    def conv(up, r1, r2, cw_ref, cb_ref):
        cw = cw_ref[...]
        return cb_ref[...] + (cw[0:1] * r2 + cw[1:2] * r1 + cw[2:3] * up)

    if not fresh:
        lhs = h_ref[...]
        parts = []
        for w_ref, cw_ref, cb_ref, p1_ref, p2_ref, s_ref in (
                (wa_ref, cwa_ref, cba_ref, p1a_ref, p2a_ref, sa_ref),
                (wg_ref, cwg_ref, cbg_ref, p1g_ref, p2g_ref, sg_ref)):
            up = jnp.dot(lhs, w_ref[...], preferred_element_type=F32)
            s_ref[...] = up
            tloc = lax.broadcasted_iota(jnp.int32, up.shape, 0) % seq
            r1 = jnp.where(tloc < 1, p1_ref[...], pltpu.roll(up, 1, 0))
            r2 = jnp.where(tloc < 2, p2_ref[...], pltpu.roll(up, 2, 0))
            parts.append(conv(up, r1, r2, cw_ref, cb_ref))
        act_ref[...] = (jax.nn.silu(parts[1]) * parts[0]).astype(act_ref.dtype)
        return

    lhs = lhs_ref[...]
    parts = []
    for w_ref, cw_ref, cb_ref, s_ref in ((wa_ref, cwa_ref, cba_ref, sa_ref), (wg_ref, cwg_ref, cbg_ref, sg_ref)):
        up = jnp.dot(lhs, w_ref[...], preferred_element_type=F32)
        s_ref[...] = up[tm:, :]
        r1 = pltpu.roll(up, 1, 0)[halo_rows:]
        r2 = pltpu.roll(up, 2, 0)[halo_rows:]
        parts.append(conv(up[halo_rows:], r1, r2, cw_ref, cb_ref))
    act_ref[...] = (jax.nn.silu(parts[1]) * parts[0]).astype(act_ref.dtype)


def _ffn_up(h3, w_up, w_conv, b_conv, d_ff, prev=None, seq=None):
    nblk, rb, k = h3.shape
    fresh = prev is None
    tn = _pick(d_ff, 512, V7X_LANES)
    goff = d_ff // tn
    cw = w_conv
    cb = b_conv.reshape(1, 2 * d_ff)
    w_specs = [
        pl.BlockSpec((k, tn), lambda b, i, j: (0, j)),
        pl.BlockSpec((k, tn), lambda b, i, j: (0, j + goff)),
        pl.BlockSpec((cw.shape[0], tn), lambda b, i, j: (0, j)),
        pl.BlockSpec((cw.shape[0], tn), lambda b, i, j: (0, j + goff)),
        pl.BlockSpec((1, tn), lambda b, i, j: (0, j)),
        pl.BlockSpec((1, tn), lambda b, i, j: (0, j + goff)),
    ]
    w_args = [w_up, w_up, cw, cw, cb, cb]
    if fresh:
        tm = _pick(rb, 1024, V7X_SUBLANES)
        hb = tm // V7X_SUBLANES
        in_specs = [
            pl.BlockSpec((None, tm, k), lambda b, i, j: (b, i, 0)),
            pl.BlockSpec((None, V7X_SUBLANES, k), lambda b, i, j: (b, jnp.maximum(i * hb - 1, 0), 0)),
        ] + w_specs
        args = [h3, h3] + w_args
        srows = V7X_SUBLANES
        scratch = [pltpu.VMEM((tm + V7X_SUBLANES, k), BF16)]
        state_spec = pl.BlockSpec((None, None, srows, tn), lambda b, i, j: (b, i, 0, j))
    else:
        tm = rb
        p1, p2 = prev
        in_specs = [pl.BlockSpec((None, tm, k), lambda b, i, j: (b, i, 0))] + w_specs + [
            pl.BlockSpec((None, tm, tn), lambda b, i, j: (b, i, j)),
            pl.BlockSpec((None, tm, tn), lambda b, i, j: (b, i, j)),
            pl.BlockSpec((None, tm, tn), lambda b, i, j: (b, i, j + goff)),
            pl.BlockSpec((None, tm, tn), lambda b, i, j: (b, i, j + goff)),
        ]
        args = [h3] + w_args + [p1, p2, p1, p2]
        srows = rb
        scratch = []
        state_spec = pl.BlockSpec((None, None, srows, tn), lambda b, i, j: (b, i, 0, j))
    vmem = (3 * (tm + 8) * k * 2 + 4 * k * tn * 2 + 10 * (tm + 8) * tn * 4
            + 2 * tm * tn * 2 + 8 * srows * tn * 4)
    return pl.pallas_call(
        functools.partial(_ffn_up_kernel, fresh=fresh, seq=seq, tm=tm),
        grid=(nblk, rb // tm, d_ff // tn),
        in_specs=in_specs,
        out_specs=[
            pl.BlockSpec((None, tm, tn), lambda b, i, j: (b, i, j)),
            state_spec,
            state_spec,
        ],
        out_shape=[
            jax.ShapeDtypeStruct((nblk, rb, d_ff), BF16),
            jax.ShapeDtypeStruct((nblk, rb // tm, srows, d_ff), F32),
            jax.ShapeDtypeStruct((nblk, rb // tm, srows, d_ff), F32),
        ],
        scratch_shapes=scratch,
        compiler_params=_params(("arbitrary", "arbitrary", "arbitrary"), vmem),
        name="convffn_up",
    )(*args)


def _beats(other, gate, other_is_lower):
    return jnp.where(other > gate, 1.0, 0.0) + jnp.where(other == gate, 1.0, 0.0) * other_is_lower


def _attn_fresh_body(cc, q_ref, o_ref, kaug_ref, vbf_ref, kmean_ref, sd_ref, *, nblk):
    blk = MOBA_BLOCK
    scale = HEAD_DIM ** -0.5
    w = (cc + 1) * blk
    q2 = q_ref[...]
    qs = jnp.concatenate([q2[:, :HEAD_DIM], q2[:, HEAD_DIM:]], axis=0)
    if cc > 0:
        gate_t = lax.dot_general(kmean_ref[...], qs, (((1,), (1,)), ((), ())),
                                 precision=lax.Precision.HIGHEST, preferred_element_type=F32)
        riota = lax.broadcasted_iota(jnp.int32, gate_t.shape, 0)
        cnt = jnp.zeros_like(gate_t)
        for m in range(cc):
            cnt = cnt + _beats(gate_t[m:m + 1, :], gate_t, jnp.where(riota > m, 1.0, 0.0))
        keep = jnp.where(cnt < MOBA_TOPK, 1.0, 0.0) + jnp.where(riota >= cc, 1.0, 0.0)
        bias_t = jnp.where(keep > 0.5, 0.0, MASK_BIAS)
        bias_t = jnp.concatenate(
            [bias_t, jnp.zeros((HEAD_DIM - nblk, 2 * blk), F32)], axis=0)
        bias = bias_t.T.astype(BF16)
    else:
        bias = jnp.zeros((2 * blk, HEAD_DIM), BF16)
    q_aug = jnp.concatenate([qs.astype(BF16), bias], axis=1)
    s = lax.dot_general(q_aug, kaug_ref[0:w, :], (((1,), (1,)), ((), ())),
                        preferred_element_type=F32)
    row = lax.broadcasted_iota(jnp.int32, (blk, blk), 0)
    col = lax.broadcasted_iota(jnp.int32, (blk, blk), 1)
    causal = row >= col
    for hh in range(2):
        sh = s[hh * blk:(hh + 1) * blk, :] * scale - sd_ref[hh, :, 0:w]
        diag = jnp.where(causal, sh[:, cc * blk:], NEG_INF)
        sh = jnp.concatenate([sh[:, :cc * blk], diag], axis=1) if cc > 0 else diag
        m = jnp.max(sh, axis=-1, keepdims=True)
        p = jnp.exp(sh - m)
        l = jnp.sum(p, axis=-1, keepdims=True)
        o = jnp.dot(p.astype(BF16), vbf_ref[0:w, :], preferred_element_type=F32) / l
        o_ref[:, hh * HEAD_DIM:(hh + 1) * HEAD_DIM] = o.astype(o_ref.dtype)


def _attn_fresh_kernel(slope_ref, q_ref, k_ref, v_ref, o_ref,
                       kaug_ref, vbf_ref, kmean_ref, sd_ref, *, nblk):
    g = pl.program_id(1)
    c = pl.program_id(2)
    blk = MOBA_BLOCK
    t = nblk * blk

    @pl.when(c == 0)
    def _():
        k = k_ref[...]
        kaug_ref[:, 0:HEAD_DIM] = k.astype(BF16)
        key_blk = lax.broadcasted_iota(jnp.int32, (t, HEAD_DIM), 0) // blk
        lane = lax.broadcasted_iota(jnp.int32, (t, HEAD_DIM), 1)
        kaug_ref[:, HEAD_DIM:] = jnp.where(key_blk == lane, 1.0, 0.0).astype(BF16)
        vbf_ref[...] = v_ref[...].astype(BF16)
        for n in range(nblk):
            kmean_ref[n:n + 1, :] = jnp.mean(k[n * blk:(n + 1) * blk, :], axis=0, keepdims=True)
        d0 = (lax.broadcasted_iota(jnp.int32, (blk, t), 0)
              - lax.broadcasted_iota(jnp.int32, (blk, t), 1)).astype(F32)
        for hh in range(2):
            sd_ref[hh] = slope_ref[2 * g + hh] * d0

    for cc in range(nblk):
        pl.when(c == cc)(functools.partial(
            _attn_fresh_body, cc, q_ref, o_ref, kaug_ref, vbf_ref, kmean_ref, sd_ref, nblk=nblk))


def _attn_fresh(q3, k3, v3, slopes):
    b, t, q_w = q3.shape
    kv_heads = k3.shape[2] // HEAD_DIM
    assert q_w == 2 * kv_heads * HEAD_DIM and t % MOBA_BLOCK == 0
    nblk = t // MOBA_BLOCK
    assert nblk <= HEAD_DIM
    blk = MOBA_BLOCK
    vmem = (2 * (blk * 2 * HEAD_DIM * 4 + 2 * t * HEAD_DIM * 4 + blk * 2 * HEAD_DIM * 2)
            + 3 * t * HEAD_DIM * 2 + 2 * blk * t * 4 + 8 * 2 * blk * t * 4)
    return pl.pallas_call(
        functools.partial(_attn_fresh_kernel, nblk=nblk),
        grid=(b, kv_heads, nblk),
        in_specs=[
            pl.BlockSpec(memory_space=pltpu.SMEM),
            pl.BlockSpec((None, blk, 2 * HEAD_DIM), lambda bb, g, c: (bb, c, g)),
            pl.BlockSpec((None, t, HEAD_DIM), lambda bb, g, c: (bb, 0, g)),
            pl.BlockSpec((None, t, HEAD_DIM), lambda bb, g, c: (bb, 0, g)),
        ],
        out_specs=pl.BlockSpec((None, blk, 2 * HEAD_DIM), lambda bb, g, c: (bb, c, g)),
        out_shape=jax.ShapeDtypeStruct((b, t, q_w), BF16),
        scratch_shapes=[
            pltpu.VMEM((t, 2 * HEAD_DIM), BF16),
            pltpu.VMEM((t, HEAD_DIM), BF16),
            pltpu.VMEM((nblk, HEAD_DIM), F32),
            pltpu.VMEM((2, blk, t), F32),
        ],
        compiler_params=_params(("arbitrary", "arbitrary", "arbitrary"), vmem),
        name="moba_fresh",
    )(slopes, q3, k3, v3)


def _page_heads_on_lanes(p_ref, page, kvh):
    return jnp.concatenate([p_ref[pl.ds(g, page, stride=kvh), :] for g in range(kvh)], axis=-1)


def _page_specs(layer, npg, page_rows, hd):
    def spec(j):
        return pl.BlockSpec((None, None, page_rows, hd),
                            lambda i, n, pt: (layer, pt[i, n * npg + j], 0, 0))
    return [spec(j) for j in range(npg)]


def _paged_scores_kernel(pt_ref, wq_ref, *refs, page, kvh, npg):
    pages = refs[:npg]
    sc_ref, km_ref = refs[npg:]
    wq = wq_ref[...]
    hd = pages[0].shape[1]
    sums = []
    for j, p_ref in enumerate(pages):
        kcat = _page_heads_on_lanes(p_ref, page, kvh)
        sc_ref[j * page:(j + 1) * page, :] = jnp.dot(kcat.astype(BF16), wq, preferred_element_type=F32)
        sums.append(jnp.sum(p_ref[...].reshape(page, kvh, hd), axis=0))
    for b in range(npg // 2):
        km_ref[b] = (sums[2 * b] + sums[2 * b + 1]) * (1.0 / (2 * page))


def _paged_scores(page_table, wq_bf, cache, layer, npg):
    s, n_pages = page_table.shape
    depth, n_pool, page, kvh, hd = cache.shape
    assert 2 * page == MOBA_BLOCK and n_pages % npg == 0 and npg % 2 == 0
    kv_w = kvh * hd
    cache_rows = cache.reshape(depth, n_pool, page * kvh, hd)
    nblk = n_pages // 2
    past = n_pages * page
    vmem = 2 * (kv_w * V7X_LANES * 2 + npg * page * kv_w * 4 + npg * page * V7X_LANES * 4) + 6 * page * kv_w * 4
    grid_spec = pltpu.PrefetchScalarGridSpec(
        num_scalar_prefetch=1,
        grid=(s, n_pages // npg),
        in_specs=[pl.BlockSpec((None, kv_w, V7X_LANES), lambda i, n, pt: (i, 0, 0))]
        + _page_specs(layer, npg, page * kvh, hd),
        out_specs=[
            pl.BlockSpec((None, npg * page, V7X_LANES), lambda i, n, pt: (i, n, 0)),
            pl.BlockSpec((None, npg // 2, kvh, hd), lambda i, n, pt: (i, n, 0, 0)),
        ],
    )
    return pl.pallas_call(
        functools.partial(_paged_scores_kernel, page=page, kvh=kvh, npg=npg),
        grid_spec=grid_spec,
        out_shape=[
            jax.ShapeDtypeStruct((s, past, V7X_LANES), F32),
            jax.ShapeDtypeStruct((s, nblk, kvh, hd), F32),
        ],
        compiler_params=_params(("arbitrary", "arbitrary"), vmem),
        name="moba_paged_scores",
    )(page_table, wq_bf, *([cache_rows] * npg))


def _paged_attend_kernel(pt_ref, wq_ref, wqbf_ref, sc_ref, km_ref, knew_ref, vnew_ref,
                         slope_ref, tcol_ref, *refs, page, kvh, npg, nblk, tnew, past):
    pages = refs[:npg]
    o_ref, sel_ref, prob_ref, linv_ref, sq_ref, vpad_ref, acc_ref = refs[npg:]
    n = pl.program_id(1)
    blk = MOBA_BLOCK
    scale = HEAD_DIM ** -0.5
    lanes = V7X_LANES
    slope = slope_ref[...]
    tcol = tcol_ref[...]
    koff = lax.broadcasted_iota(jnp.int32, (blk, lanes), 0).astype(F32)

    def block_scores(i):
        r0 = pl.multiple_of(i * blk, blk)
        raw = sc_ref[pl.ds(r0, blk), :]
        dist = (tcol + (past - i * blk).astype(F32)) - koff
        s = raw * scale - slope * dist
        return jnp.where(sel_ref[pl.ds(i, 1), :] > 0.5, s, NEG_INF)

    @pl.when(n == 0)
    def _():
        gate = jnp.zeros((nblk, lanes), F32)
        for g in range(kvh):
            gate = gate + jnp.dot(km_ref[pl.ds(g, nblk, stride=kvh), :],
                                  wq_ref[g * HEAD_DIM:(g + 1) * HEAD_DIM, :],
                                  precision=lax.Precision.HIGHEST, preferred_element_type=F32)
        riota = lax.broadcasted_iota(jnp.int32, (nblk, lanes), 0)
        cnt = jnp.zeros((nblk, lanes), F32)
        for m in range(nblk):
            cnt = cnt + _beats(gate[m:m + 1, :], gate, jnp.where(riota > m, 1.0, 0.0))
        sel_ref[...] = jnp.where(cnt < MOBA_TOPK, 1.0, 0.0)

        s_cur = jnp.dot(knew_ref[...].astype(BF16), wqbf_ref[...], preferred_element_type=F32)
        off = lax.broadcasted_iota(jnp.int32, (tnew, lanes), 0).astype(F32)
        s_cur = s_cur * scale - slope * (tcol - off)
        s_cur = jnp.where(off <= tcol, s_cur, NEG_INF)
        m0 = jnp.max(s_cur, axis=0, keepdims=True)

        def max_body(i, m):
            return jnp.maximum(m, jnp.max(block_scores(i), axis=0, keepdims=True))

        mx = lax.fori_loop(0, nblk, max_body, m0)
        p_cur = jnp.exp(s_cur - mx)

        def sum_body(i, l):
            p = jnp.exp(block_scores(i) - mx)
            prob_ref[pl.ds(pl.multiple_of(i * blk, blk), blk), :] = p
            return l + jnp.sum(p, axis=0, keepdims=True)

        l = lax.fori_loop(0, nblk, sum_body, jnp.sum(p_cur, axis=0, keepdims=True))
        linv = 1.0 / l
        linv_ref[...] = linv
        sq_ref[...] = jnp.zeros_like(sq_ref)
        sq_ref[0:tnew, :] = p_cur * linv
        vpad_ref[...] = jnp.zeros_like(vpad_ref)
        vpad_ref[0:tnew, :] = vnew_ref[...].astype(BF16)
        acc_ref[...] = jnp.dot(sq_ref[...].T.astype(BF16), vpad_ref[...], preferred_element_type=F32)

    linv = linv_ref[...]
    acc = acc_ref[...]
    for j, p_ref in enumerate(pages):
        r0 = pl.multiple_of((n * npg + j) * page, page)
        pt_j = (prob_ref[pl.ds(r0, page), :] * linv).T.astype(BF16)
        vcat = _page_heads_on_lanes(p_ref, page, kvh).astype(BF16)
        acc = acc + jnp.dot(pt_j, vcat, preferred_element_type=F32)
    acc_ref[...] = acc

    @pl.when(n == pl.num_programs(1) - 1)
    def _():
        rows = o_ref.shape[1]
        for g in range(o_ref.shape[0]):
            o_ref[g] = acc_ref[g * rows:(g + 1) * rows, g * HEAD_DIM:(g + 1) * HEAD_DIM].astype(o_ref.dtype)


def _paged_attend(page_table, wq, wq_bf, scores, kmean, k_new, v_new, slope_col, t_col, cache, layer, npg):
    s, n_pages = page_table.shape
    depth, n_pool, page, kvh, hd = cache.shape
    kv_w = kvh * hd
    cache_rows = cache.reshape(depth, n_pool, page * kvh, hd)
    nblk = n_pages // 2
    past = n_pages * page
    tnew = k_new.shape[1]
    rows = 2 * tnew
    kmean_rows = kmean.reshape(s, nblk * kvh, hd)
    vmem = (2 * (kv_w * V7X_LANES * 6 + past * V7X_LANES * 4 + nblk * kv_w * 4 + npg * page * kv_w * 4)
            + past * V7X_LANES * 4 + 3 * V7X_LANES * kv_w * 4 + 16 * MOBA_BLOCK * V7X_LANES * 4
            + 4 * page * kv_w * 4)
    grid_spec = pltpu.PrefetchScalarGridSpec(
        num_scalar_prefetch=1,
        grid=(s, n_pages // npg),
        in_specs=[
            pl.BlockSpec((None, kv_w, V7X_LANES), lambda i, n, pt: (i, 0, 0)),
            pl.BlockSpec((None, kv_w, V7X_LANES), lambda i, n, pt: (i, 0, 0)),
            pl.BlockSpec((None, past, V7X_LANES), lambda i, n, pt: (i, 0, 0)),
            pl.BlockSpec((None, nblk * kvh, hd), lambda i, n, pt: (i, 0, 0)),
            pl.BlockSpec((None, tnew, kv_w), lambda i, n, pt: (i, 0, 0)),
            pl.BlockSpec((None, tnew, kv_w), lambda i, n, pt: (i, 0, 0)),
            pl.BlockSpec((1, V7X_LANES), lambda i, n, pt: (0, 0)),
            pl.BlockSpec((1, V7X_LANES), lambda i, n, pt: (0, 0)),
        ] + _page_specs(layer, npg, page * kvh, hd),
        out_specs=pl.BlockSpec((None, kvh, rows, hd), lambda i, n, pt: (i, 0, 0, 0)),
        scratch_shapes=[
            pltpu.VMEM((nblk, V7X_LANES), F32),
            pltpu.VMEM((past, V7X_LANES), F32),
            pltpu.VMEM((1, V7X_LANES), F32),
            pltpu.VMEM((V7X_LANES, V7X_LANES), F32),
            pltpu.VMEM((V7X_LANES, kv_w), BF16),
            pltpu.VMEM((V7X_LANES, kv_w), F32),
        ],
    )
    return pl.pallas_call(
        functools.partial(_paged_attend_kernel, page=page, kvh=kvh, npg=npg, nblk=nblk, tnew=tnew, past=past),
        grid_spec=grid_spec,
        out_shape=jax.ShapeDtypeStruct((s, kvh, rows, hd), BF16),
        compiler_params=_params(("arbitrary", "arbitrary"), vmem),
        name="moba_paged_attend",
    )(page_table, wq, wq_bf, scores, kmean_rows, k_new, v_new, slope_col, t_col, *([cache_rows] * npg))


def _ssm_prep_kernel(lr_ref, li_ref, ldt_ref, brt_ref, bit_ref, are_ref, aim_ref, bbr_ref, bbi_ref):
    lr = lr_ref[...]
    li = li_ref[...]
    dt = jnp.exp(ldt_ref[...])
    mag = jnp.exp(lr * dt)
    a_re = mag * jnp.cos(li * dt)
    a_im = mag * jnp.sin(li * dt)
    den = lr * lr + li * li
    q_re = ((a_re - 1.0) * lr + a_im * li) / den
    q_im = (a_im * lr - (a_re - 1.0) * li) / den
    are_ref[...] = a_re
    aim_ref[...] = a_im
    br = brt_ref[...]
    bi = bit_ref[...]
    bbr_ref[...] = q_re * br - q_im * bi
    bbi_ref[...] = q_re * bi + q_im * br


def _ssm_prep(lam_re, lam_im, log_dt, b_re, b_im):
    depth, g, p = lam_re.shape
    i = b_re.shape[3]
    brt = jnp.swapaxes(b_re, 2, 3)
    bit = jnp.swapaxes(b_im, 2, 3)
    vec = pl.BlockSpec((None, g, 1, p), lambda l: (l, 0, 0, 0))
    mat = pl.BlockSpec((None, g, i, p), lambda l: (l, 0, 0, 0))
    return pl.pallas_call(
        _ssm_prep_kernel,
        grid=(depth,),
        in_specs=[vec, vec, pl.BlockSpec((None, g, 1, 1), lambda l: (l, 0, 0, 0)), mat, mat],
        out_specs=[vec, vec, mat, mat],
        out_shape=[
            jax.ShapeDtypeStruct((depth, g, 1, p), F32),
            jax.ShapeDtypeStruct((depth, g, 1, p), F32),
            jax.ShapeDtypeStruct((depth, g, i, p), F32),
            jax.ShapeDtypeStruct((depth, g, i, p), F32),
        ],
        compiler_params=_params(("arbitrary",), 32 << 20),
        name="s5_discretise",
    )(lam_re.reshape(depth, g, 1, p), lam_im.reshape(depth, g, 1, p),
      log_dt.reshape(depth, g, 1, 1), brt, bit)


def _ssm_scan_kernel(u_ref, ire_ref, iim_ref, are_ref, aim_ref, bre_ref, bim_ref,
                     cre_ref, cim_ref, d_ref, *rest, with_y, n_chunks, tau_b):
    if with_y:
        y_ref, ere_ref, eim_ref, hre_ref, him_ref, bure_ref, buim_ref = rest
    else:
        ere_ref, eim_ref, hre_ref, him_ref, bure_ref, buim_ref = rest
    k = pl.program_id(1)

    @pl.when(k == 0)
    def _():
        hre_ref[...] = ire_ref[...]
        him_ref[...] = iim_ref[...]

    def chunk(c, carry):
        cu = pl.multiple_of(c * CHUNK_IN, CHUNK_IN)
        cs = pl.multiple_of(c * CHUNK_ST, CHUNK_ST)
        u_c = u_ref[:, pl.ds(cu, CHUNK_IN)]
        ub = u_c.astype(BF16)
        bure_ref[...] = jnp.dot(ub, bre_ref[c], preferred_element_type=F32)
        buim_ref[...] = jnp.dot(ub, bim_ref[c], preferred_element_type=F32)
        ar = jnp.broadcast_to(are_ref[:, pl.ds(cs, CHUNK_ST)], (N_SEG, CHUNK_ST))
        ai = jnp.broadcast_to(aim_ref[:, pl.ds(cs, CHUNK_ST)], (N_SEG, CHUNK_ST))
        hr = hre_ref[:, pl.ds(cs, CHUNK_ST)]
        hi = him_ref[:, pl.ds(cs, CHUNK_ST)]
        for t in range(tau_b):
            rs = slice(t * N_SEG, (t + 1) * N_SEG)
            nr = ar * hr - ai * hi + bure_ref[rs, :]
            ni = ar * hi + ai * hr + buim_ref[rs, :]
            hr, hi = nr, ni
            if with_y:
                bure_ref[rs, :] = hr
                buim_ref[rs, :] = hi
        hre_ref[:, pl.ds(cs, CHUNK_ST)] = hr
        him_ref[:, pl.ds(cs, CHUNK_ST)] = hi
        if with_y:
            ych = (jnp.dot(bure_ref[...].astype(BF16), cre_ref[c], preferred_element_type=F32)
                   - jnp.dot(buim_ref[...].astype(BF16), cim_ref[c], preferred_element_type=F32))
            y = ych + d_ref[:, pl.ds(cu, CHUNK_IN)] * u_c
            y_ref[:, pl.ds(cu, CHUNK_IN)] = jax.nn.gelu(y)
        return carry

    lax.fori_loop(0, n_chunks, chunk, 0)

    @pl.when(k == pl.num_programs(1) - 1)
    def _():
        ere_ref[...] = hre_ref[...]
        eim_ref[...] = him_ref[...]


def _ssm_scan(u3, init_re, init_im, a_re, a_im, bre, bim, cre, cim, d_skip, with_y):
    nb, lr, s_w = u3.shape
    ns = a_re.shape[1]
    n_chunks = bre.shape[0]
    steps = lr // N_SEG
    tau_b = _pick(steps, 32, 1)
    rb = tau_b * N_SEG
    st_spec = pl.BlockSpec((None, N_SEG, ns), lambda b, k: (b, 0, 0))
    full = lambda shape: pl.BlockSpec(shape, lambda b, k: (0,) * len(shape))
    u_spec = pl.BlockSpec((None, rb, s_w), lambda b, k: (b, k, 0))
    out_specs = [st_spec, st_spec]
    out_shape = [jax.ShapeDtypeStruct((nb, N_SEG, ns), F32)] * 2
    if with_y:
        out_specs = [u_spec] + out_specs
        out_shape = [jax.ShapeDtypeStruct((nb, lr, s_w), F32)] + out_shape
    vmem = (4 * rb * s_w * 4 + 8 * N_SEG * ns * 4 + 4 * ns * 4
            + 2 * 4 * n_chunks * CHUNK_IN * CHUNK_ST * 2 + 8 * rb * CHUNK_ST * 4)
    return pl.pallas_call(
        functools.partial(_ssm_scan_kernel, with_y=with_y, n_chunks=n_chunks, tau_b=tau_b),
        grid=(nb, steps // tau_b),
        in_specs=[u_spec, st_spec, st_spec, full((1, ns)), full((1, ns)),
                  full(bre.shape), full(bim.shape), full(cre.shape), full(cim.shape),
                  full((1, s_w))],
        out_specs=out_specs,
        out_shape=out_shape,
        scratch_shapes=[
            pltpu.VMEM((N_SEG, ns), F32),
            pltpu.VMEM((N_SEG, ns), F32),
            pltpu.VMEM((rb, CHUNK_ST), F32),
            pltpu.VMEM((rb, CHUNK_ST), F32),
        ],
        compiler_params=_params(("arbitrary", "arbitrary"), vmem),
        name="s5_scan" if with_y else "s5_segment_ends",
    )(u3, init_re, init_im, a_re, a_im, bre, bim, cre, cim, d_skip)


def _ssm_carry_kernel(ere_ref, eim_ref, are_ref, aim_ref, ire_ref, iim_ref, fre_ref, fim_ref, *, seg_len):
    br = are_ref[...]
    bi = aim_ref[...]
    pr = jnp.ones_like(br)
    pi = jnp.zeros_like(br)
    e = seg_len
    while e:
        if e & 1:
            pr, pi = pr * br - pi * bi, pr * bi + pi * br
        br, bi = br * br - bi * bi, 2.0 * br * bi
        e >>= 1
    hr = jnp.zeros_like(pr)
    hi = jnp.zeros_like(pr)
    for j in range(N_SEG):
        ire_ref[j:j + 1, :] = hr
        iim_ref[j:j + 1, :] = hi
        er = ere_ref[j:j + 1, :]
        ei = eim_ref[j:j + 1, :]
        hr, hi = pr * hr - pi * hi + er, pr * hi + pi * hr + ei
    fre_ref[...] = hr
    fim_ref[...] = hi


def _ssm_carry(end_re, end_im, a_re, a_im, seg_len):
    nb, _, ns = end_re.shape
    st = pl.BlockSpec((None, N_SEG, ns), lambda b: (b, 0, 0))
    vec = pl.BlockSpec((1, ns), lambda b: (0, 0))
    fin = pl.BlockSpec((None, 1, ns), lambda b: (b, 0, 0))
    return pl.pallas_call(
        functools.partial(_ssm_carry_kernel, seg_len=seg_len),
        grid=(nb,),
        in_specs=[st, st, vec, vec],
        out_specs=[st, st, fin, fin],
        out_shape=[jax.ShapeDtypeStruct((nb, N_SEG, ns), F32)] * 2
        + [jax.ShapeDtypeStruct((nb, 1, ns), F32)] * 2,
        compiler_params=_params(("arbitrary",), 32 << 20),
        name="s5_segment_carry",
    )(end_re, end_im, a_re, a_im)


def _block_diag_in(bbt):
    g, i, p = bbt.shape
    nc = g // GROUPS_PER_CHUNK
    eye = jnp.eye(GROUPS_PER_CHUNK, dtype=bbt.dtype)
    x = bbt.reshape(nc, GROUPS_PER_CHUNK, i, p)
    out = x[:, :, :, None, :] * eye[None, :, None, :, None]
    return out.reshape(nc, GROUPS_PER_CHUNK * i, GROUPS_PER_CHUNK * p).astype(BF16)


def _block_diag_out(c):
    g, i, p = c.shape
    nc = g // GROUPS_PER_CHUNK
    eye = jnp.eye(GROUPS_PER_CHUNK, dtype=c.dtype)
    x = jnp.swapaxes(c.reshape(nc, GROUPS_PER_CHUNK, i, p), 2, 3)
    out = x[:, :, :, None, :] * eye[None, :, None, :, None]
    return out.reshape(nc, GROUPS_PER_CHUNK * p, GROUPS_PER_CHUNK * i).astype(BF16)


def kernel(x_prompt, x_sample, c_prompt, c_sample, cache_k, cache_v, state_ssm_re, state_ssm_im,
           state_conv, page_table, w_ada, b_ada, norm_attn, w_in, w_attn_proj, lam_re, lam_im,
           log_dt, ssm_b_re, ssm_b_im, ssm_c_re, ssm_c_im, ssm_d, w_glu, w_ssm_proj, w_out,
           norm_ffn, w_up, w_conv, b_conv, w_down, norm_final):
    depth = w_ada.shape[0]
    nb, t, d = x_prompt.shape
    ns_seq, ts, _ = x_sample.shape
    q_w = w_attn_proj.shape[1]
    n_heads = q_w // HEAD_DIM
    kv_heads = cache_k.shape[3]
    kv_w = kv_heads * HEAD_DIM
    s_w = ssm_d.shape[1]
    n_groups = lam_re.shape[1]
    n_state = n_groups * S5_STATE
    d_ff = w_down.shape[1]
    n_pages = page_table.shape[1]
    page = cache_k.shape[2]
    past = n_pages * page
    assert ns_seq == N_SEG and t % (N_SEG * N_SEG) == 0 and n_groups % GROUPS_PER_CHUNK == 0
    n_col = kv_heads * 2 * ts
    assert n_col <= V7X_LANES and n_heads == 2 * kv_heads
    assert ssm_b_re.shape[2:] == (S5_STATE, S5_GROUP)
    u_col = q_w + 2 * kv_w
    g_col = u_col + s_w

    n_c = nb + ns_seq
    c_rows = -(-n_c // V7X_SUBLANES) * V7X_SUBLANES
    c_all = jnp.concatenate([c_prompt, c_sample, jnp.zeros((c_rows - n_c, d), F32)], axis=0)
    mod = _adaln(c_all, w_ada, b_ada)

    a_re_all, a_im_all, bbr_all, bbi_all = _ssm_prep(lam_re, lam_im, log_dt, ssm_b_re, ssm_b_im)
    slopes = 2.0 ** (-8.0 * jnp.arange(1, n_heads + 1, dtype=F32) / n_heads)

    col = jnp.arange(V7X_LANES)
    col_head = jnp.minimum(2 * (col // (2 * ts)) + (col // ts) % 2, n_heads - 1)
    slope_col = slopes[col_head].reshape(1, V7X_LANES)
    t_col = (col % ts).astype(F32).reshape(1, V7X_LANES)

    xp = x_prompt
    xs = x_sample.reshape(1, ns_seq * ts, d)
    seg = t // N_SEG
    keep = w_conv.shape[1] - 1
    assert keep == 2 and past % MOBA_BLOCK == 0 and ts <= MOBA_BLOCK
    npg = _pick(n_pages, PAGES_PER_STEP, 2)
    outs ={k: [] for k in ("kp", "vp", "ks", "vs", "hpr", "hpi", "hsr", "hsi", "cp", "cs")}

    for l in range(depth):
        w_in_bf = w_in[l].astype(BF16)
        w_attn_bf = w_attn_proj[l].astype(BF16)
        w_glu_bf = w_glu[l].astype(BF16)
        w_ssm_bf = w_ssm_proj[l].astype(BF16)
        w_out_bf = w_out[l].astype(BF16)
        w_up_bf = w_up[l].astype(BF16)
        w_down_bf = w_down[l].astype(BF16)
        a_re = a_re_all[l].reshape(1, n_state)
        a_im = a_im_all[l].reshape(1, n_state)
        bre = _block_diag_in(bbr_all[l])
        bim = _block_diag_in(bbi_all[l])
        cre = _block_diag_out(ssm_c_re[l])
        cim = _block_diag_out(ssm_c_im[l])
        d_skip = ssm_d[l].reshape(1, s_w)
        mod_l = mod[l].reshape(c_rows, 6, d)

        def in_proj(h3):
            return (_mm_plain(h3, w_in_bf, 0, q_w, F32),
                    _mm_plain(h3, w_in_bf, q_w, kv_w, F32),
                    _mm_plain(h3, w_in_bf, q_w + kv_w, kv_w, F32),
                    _mm_plain(h3, w_in_bf, u_col, s_w, F32),
                    _mm_plain(h3, w_in_bf, g_col, 2 * d, F32))

        def mods_for(lo, n, rep):
            m = mod_l[lo:lo + n]
            if rep == 1:
                return [m[:, i].reshape(n, 1, d) for i in range(6)]
            return [jnp.repeat(m[:, i], rep, axis=0).reshape(1, n * rep, d) for i in range(6)]

        sh1, sc1, g1, sh2, sc2, g2 = mods_for(0, nb, 1)
        h = _rmsnorm(xp, norm_attn[l], sc1, sh1)
        q, k, v, u, gates = in_proj(h)
        attn = _attn_fresh(q, k, v, slopes)
        u_perm = u.reshape(nb, N_SEG, seg, s_w).swapaxes(1, 2).reshape(nb, t, s_w)
        zero_st = jnp.zeros((nb, N_SEG, n_state), F32)
        end_re, end_im = _ssm_scan(u_perm, zero_st, zero_st, a_re, a_im, bre, bim, cre, cim, d_skip, False)
        ini_re, ini_im, fin_re, fin_im = _ssm_carry(end_re, end_im, a_re, a_im, seg)
        y_perm, _, _ = _ssm_scan(u_perm, ini_re, ini_im, a_re, a_im, bre, bim, cre, cim, d_skip, True)
        y = y_perm.reshape(nb, seg, N_SEG, s_w).swapaxes(1, 2).reshape(nb, t, s_w)
        ssm = _mm_glu(y, w_glu_bf)
        merged = _mm_merge(attn, ssm, w_attn_bf, w_ssm_bf, gates, d)
        xp = _mm_resid(merged, w_out_bf, xp, g1, 1024, 512)
        h2 = _rmsnorm(xp, norm_ffn[l], sc2, sh2)
        act, tail_a, tail_g = _ffn_up(h2, w_up_bf, w_conv[l], b_conv[l], d_ff)
        xp = _mm_resid(act, w_down_bf, xp, g2, 1024, 256, lhs_buffers=1)
        outs["kp"].append(k.reshape(nb, t, kv_heads, HEAD_DIM))
        outs["vp"].append(v.reshape(nb, t, kv_heads, HEAD_DIM))
        outs["hpr"].append(fin_re.reshape(nb, n_groups, S5_STATE))
        outs["hpi"].append(fin_im.reshape(nb, n_groups, S5_STATE))
        keep = w_conv.shape[1] - 1
        outs["cp"].append(jnp.concatenate([tail_a[:, -1], tail_g[:, -1]], axis=-1)[:, V7X_SUBLANES - keep:, :])

        sh1, sc1, g1, sh2, sc2, g2 = mods_for(nb, ns_seq, ts)
        h = _rmsnorm(xs, norm_attn[l], sc1, sh1)
        q, k, v, u, gates = in_proj(h)
        q_s = q.reshape(ns_seq, ts, kv_heads, 2, HEAD_DIM)
        k_new = k.reshape(ns_seq, ts, kv_w)
        v_new = v.reshape(ns_seq, ts, kv_w)
        eye = jnp.eye(kv_heads, dtype=F32)
        wq = (jnp.transpose(q_s, (0, 2, 4, 3, 1))[:, :, :, None, :, :]
              * eye[None, :, None, :, None, None]).reshape(ns_seq, kv_w, n_col)
        wq = jnp.pad(wq, ((0, 0), (0, 0), (0, V7X_LANES - n_col)))
        wq_bf = wq.astype(BF16)
        scores, kmean = _paged_scores(page_table, wq_bf, cache_k, l, npg)
        o_s = _paged_attend(page_table, wq, wq_bf, scores, kmean, k_new, v_new,
                            slope_col, t_col, cache_v, l, npg)
        attn = jnp.transpose(o_s.reshape(ns_seq, kv_heads, 2, ts, HEAD_DIM), (0, 3, 1, 2, 4))
        attn = attn.reshape(1, ns_seq * ts, q_w)
        u_perm = jnp.swapaxes(u.reshape(ns_seq, ts, s_w), 0, 1).reshape(1, ts * ns_seq, s_w)
        st_re = state_ssm_re[l].reshape(1, ns_seq, n_state)
        st_im = state_ssm_im[l].reshape(1, ns_seq, n_state)
        y_perm, e_re, e_im = _ssm_scan(u_perm, st_re, st_im, a_re, a_im, bre, bim, cre, cim, d_skip, True)
        y = jnp.swapaxes(y_perm.reshape(ts, ns_seq, s_w), 0, 1).reshape(1, ns_seq * ts, s_w)
        ssm = _mm_glu(y, w_glu_bf)
        merged = _mm_merge(attn, ssm, w_attn_bf, w_ssm_bf, gates, d)
        xs = _mm_resid(merged, w_out_bf, xs, g1, 1024, 1024)
        h2 = _rmsnorm(xs, norm_ffn[l], sc2, sh2)
        keep = w_conv.shape[1] - 1
        cprev = state_conv[l]
        prev1 = jnp.concatenate([cprev[:, keep - 1:keep], jnp.zeros((ns_seq, ts - 1, 2 * d_ff), F32)], axis=1)
        prev2 = jnp.concatenate([cprev[:, keep - 2:keep], jnp.zeros((ns_seq, ts - 2, 2 * d_ff), F32)], axis=1)
        prev1 = prev1.reshape(1, ns_seq * ts, 2 * d_ff)
        prev2 = prev2.reshape(1, ns_seq * ts, 2 * d_ff)
        act, up_a, up_g = _ffn_up(h2, w_up_bf, w_conv[l], b_conv[l], d_ff, prev=(prev1, prev2), seq=ts)
        xs = _mm_resid(act, w_down_bf, xs, g2, 512, 512)
        outs["ks"].append(k_new.reshape(ns_seq, ts, kv_heads, HEAD_DIM))
        outs["vs"].append(v_new.reshape(ns_seq, ts, kv_heads, HEAD_DIM))
        outs["hsr"].append(e_re.reshape(ns_seq, n_groups, S5_STATE))
        outs["hsi"].append(e_im.reshape(ns_seq, n_groups, S5_STATE))
        up_full = jnp.concatenate([up_a, up_g], axis=-1).reshape(ns_seq, ts, 2 * d_ff)
        outs["cs"].append(up_full[:, ts - keep:, :])

    y_prompt = _rmsnorm(xp, norm_final, out_dtype=F32)
    y_sample = _rmsnorm(xs, norm_final, out_dtype=F32).reshape(ns_seq, ts, d)
    st = lambda k: jnp.stack(outs[k])
    return (y_prompt, y_sample, st("kp"), st("vp"), st("ks"), st("vs"),
            st("hpr"), st("hpi"), st("hsr"), st("hsi"), st("cp"), st("cs"))
```

```python
import functools
import math

import jax
import jax.numpy as jnp
from jax import lax
from jax.experimental import pallas as pl
from jax.experimental.pallas import tpu as pltpu

F32 = jnp.float32
BF16 = jnp.bfloat16

V7X_LANES = 128
V7X_SUBLANES = 8
V7X_VMEM_BYTES = 64 * 2**20
VMEM_BUDGET = V7X_VMEM_BYTES - 8 * 2**20

HEAD_DIM = 128
MOBA_BLOCK = 256
MOBA_TOPK = 3
S5_GROUP = 16
S5_STATE = 64
GROUPS_PER_CHUNK = 8
CHUNK_IN = GROUPS_PER_CHUNK * S5_GROUP
CHUNK_ST = GROUPS_PER_CHUNK * S5_STATE
N_SEG = V7X_SUBLANES
PAGES_PER_STEP = 16
EPS = 1e-6
NEG_INF = float("-inf")
MASK_BIAS = -1e30


def _pick(dim, pref, align):
    t = min(pref, dim)
    t -= t % align
    while t >= align:
        if dim % t == 0:
            return t
        t -= align
    return dim


def _params(sem, vmem_bytes):
    limit = int(min(max(vmem_bytes * 5 // 4 + (4 << 20), 32 << 20), VMEM_BUDGET))
    return pltpu.CompilerParams(dimension_semantics=sem, vmem_limit_bytes=limit)


def _ada_kernel(c_ref, w_ref, b_ref, o_ref):
    s = jax.nn.silu(c_ref[...]).astype(BF16)
    w = w_ref[...].astype(BF16)
    o_ref[...] = jnp.dot(s, w, preferred_element_type=F32) + b_ref[...]


def _adaln(c_all, w_ada, b_ada):
    depth, d, n = w_ada.shape
    rows = c_all.shape[0]
    tn = _pick(n, 512, V7X_LANES)
    vmem = 2 * d * tn * 4 + d * tn * 2 + 4 * rows * (d + tn) * 4
    return pl.pallas_call(
        _ada_kernel,
        grid=(depth, n // tn),
        in_specs=[
            pl.BlockSpec((rows, d), lambda l, j: (0, 0)),
            pl.BlockSpec((None, d, tn), lambda l, j: (l, 0, j)),
            pl.BlockSpec((None, 1, tn), lambda l, j: (l, 0, j)),
        ],
        out_specs=pl.BlockSpec((None, rows, tn), lambda l, j: (l, 0, j)),
        out_shape=jax.ShapeDtypeStruct((depth, rows, n), F32),
        compiler_params=_params(("arbitrary", "arbitrary"), vmem),
        name="adaln",
    )(c_all, w_ada, b_ada.reshape(depth, 1, n))


def _norm_kernel(x_ref, g_ref, *rest, modulate):
    if modulate:
        sc_ref, sh_ref, o_ref = rest
    else:
        (o_ref,) = rest
    x = x_ref[...]
    y = x * lax.rsqrt(jnp.mean(x * x, axis=-1, keepdims=True) + EPS)
    y = y * g_ref[...]
    if modulate:
        y = y * (1.0 + sc_ref[...]) + sh_ref[...]
    o_ref[...] = y.astype(o_ref.dtype)


def _row_param_spec(p, tr, tn, col_blocked):
    shared = p.shape[1] == 1
    rows = 1 if shared else tr
    if col_blocked:
        if shared:
            return pl.BlockSpec((None, rows, tn), lambda b, i, j: (b, 0, j))
        return pl.BlockSpec((None, rows, tn), lambda b, i, j: (b, i, j))
    if shared:
        return pl.BlockSpec((None, rows, tn), lambda b, i: (b, 0, 0))
    return pl.BlockSpec((None, rows, tn), lambda b, i: (b, i, 0))


def _rmsnorm(x3, gamma, scale=None, shift=None, out_dtype=BF16):
    nblk, rb, d = x3.shape
    tr = _pick(rb, 256, V7X_SUBLANES)
    modulate = scale is not None
    in_specs = [
        pl.BlockSpec((None, tr, d), lambda b, i: (b, i, 0)),
        pl.BlockSpec((1, d), lambda b, i: (0, 0)),
    ]
    args = [x3, gamma.reshape(1, d)]
    if modulate:
        in_specs += [_row_param_spec(scale, tr, d, False), _row_param_spec(shift, tr, d, False)]
        args += [scale, shift]
    vmem = 2 * tr * d * (4 + 4) + 6 * tr * d * 4
    return pl.pallas_call(
        functools.partial(_norm_kernel, modulate=modulate),
        grid=(nblk, rb // tr),
        in_specs=in_specs,
        out_specs=pl.BlockSpec((None, tr, d), lambda b, i: (b, i, 0)),
        out_shape=jax.ShapeDtypeStruct((nblk, rb, d), out_dtype),
        compiler_params=_params(("arbitrary", "arbitrary"), vmem),
        name="rmsnorm_mod",
    )(*args)


def _mm_plain_kernel(a_ref, w_ref, o_ref):
    o_ref[...] = jnp.dot(a_ref[...], w_ref[...], preferred_element_type=F32).astype(o_ref.dtype)


def _w_spec(k, tn, layer, col_block0=0):
    return pl.BlockSpec((None, k, tn), lambda b, i, j: (layer, 0, j + col_block0))


def _mm_plain(a3, w, layer, col0, ncols, out_dtype, tm_pref=1024, tn_pref=1024):
    nblk, rb, k = a3.shape
    tm = _pick(rb, tm_pref, V7X_SUBLANES)
    tn = _pick(math.gcd(ncols, col0) if col0 else ncols, tn_pref, V7X_LANES)
    cb = col0 // tn
    osz = jnp.dtype(out_dtype).itemsize
    vmem = 2 * (tm * k * 2 + k * tn * 2 + tm * tn * osz) + tm * tn * 4
    return pl.pallas_call(
        _mm_plain_kernel,
        grid=(nblk, rb // tm, ncols // tn),
        in_specs=[
            pl.BlockSpec((None, tm, k), lambda b, i, j: (b, i, 0)),
            _w_spec(k, tn, layer, cb),
        ],
        out_specs=pl.BlockSpec((None, tm, tn), lambda b, i, j: (b, i, j)),
        out_shape=jax.ShapeDtypeStruct((nblk, rb, ncols), out_dtype),
        compiler_params=_params(("arbitrary", "arbitrary", "arbitrary"), vmem),
        name="proj",
    )(a3, w)


def _mm_glu_kernel(y_ref, w_ref, yt_ref, o_ref, ybf_ref):
    @pl.when(pl.program_id(2) == 0)
    def _():
        ybf_ref[...] = y_ref[...].astype(BF16)

    acc = jnp.dot(ybf_ref[...], w_ref[...], preferred_element_type=F32)
    o_ref[...] = (yt_ref[...] * jax.nn.sigmoid(acc)).astype(o_ref.dtype)


def _mm_glu(y3, w, layer):
    nblk, rb, k = y3.shape
    n = w.shape[2]
    tm = _pick(rb, 1024, V7X_SUBLANES)
    tn = _pick(n, 512, V7X_LANES)
    vmem = 2 * (tm * k * 4 + k * tn * 2 + tm * tn * 4 + tm * tn * 2) + tm * k * 2 + tm * tn * 4
    return pl.pallas_call(
        _mm_glu_kernel,
        grid=(nblk, rb // tm, n // tn),
        in_specs=[
            pl.BlockSpec((None, tm, k), lambda b, i, j: (b, i, 0)),
            _w_spec(k, tn, layer),
            pl.BlockSpec((None, tm, tn), lambda b, i, j: (b, i, j)),
        ],
        out_specs=pl.BlockSpec((None, tm, tn), lambda b, i, j: (b, i, j)),
        out_shape=jax.ShapeDtypeStruct((nblk, rb, n), BF16),
        scratch_shapes=[pltpu.VMEM((tm, k), BF16)],
        compiler_params=_params(("arbitrary", "arbitrary", "arbitrary"), vmem),
        name="ssm_glu",
    )(y3, w, y3)


def _mm_merge_kernel(a_ref, s_ref, wa_ref, ws_ref, ga_ref, gs_ref, o_ref):
    pa = jnp.dot(a_ref[...], wa_ref[...], preferred_element_type=F32)
    ps = jnp.dot(s_ref[...], ws_ref[...], preferred_element_type=F32)
    o = jax.nn.sigmoid(ga_ref[...]) * pa + jax.nn.sigmoid(gs_ref[...]) * ps
    o_ref[...] = o.astype(o_ref.dtype)


def _mm_merge(attn3, ssm3, w_attn, w_ssm, layer, gates3, d):
    nblk, rb, ka = attn3.shape
    ks = ssm3.shape[2]
    tm = _pick(rb, 1024, V7X_SUBLANES)
    tn = _pick(d, 512, V7X_LANES)
    gs_off = d // tn
    vmem = 2 * (tm * (ka + ks) * 2 + (ka + ks) * tn * 2 + 2 * tm * tn * 4 + tm * tn * 2) + 3 * tm * tn * 4
    return pl.pallas_call(
        _mm_merge_kernel,
        grid=(nblk, rb // tm, d // tn),
        in_specs=[
            pl.BlockSpec((None, tm, ka), lambda b, i, j: (b, i, 0)),
            pl.BlockSpec((None, tm, ks), lambda b, i, j: (b, i, 0)),
            _w_spec(ka, tn, layer),
            _w_spec(ks, tn, layer),
            pl.BlockSpec((None, tm, tn), lambda b, i, j: (b, i, j)),
            pl.BlockSpec((None, tm, tn), lambda b, i, j: (b, i, j + gs_off)),
        ],
        out_specs=pl.BlockSpec((None, tm, tn), lambda b, i, j: (b, i, j)),
        out_shape=jax.ShapeDtypeStruct((nblk, rb, d), BF16),
        compiler_params=_params(("arbitrary", "arbitrary", "arbitrary"), vmem),
        name="mixer_merge",
    )(attn3, ssm3, w_attn, w_ssm, gates3, gates3)


def _mm_resid_kernel(a_ref, w_ref, x_ref, g_ref, o_ref):
    acc = jnp.dot(a_ref[...], w_ref[...], preferred_element_type=F32)
    o_ref[...] = x_ref[...] + g_ref[...] * acc


def _mm_resid(a3, w, layer, x3, gate, tm_pref, tn_pref, lhs_buffers=2):
    nblk, rb, k = a3.shape
    n = w.shape[2]
    tm = _pick(rb, tm_pref, V7X_SUBLANES)
    tn = _pick(n, tn_pref, V7X_LANES)
    vmem = lhs_buffers * tm * k * 2 + 2 * (k * tn * 2 + 2 * tm * tn * 4) + tm * tn * 4
    return pl.pallas_call(
        _mm_resid_kernel,
        grid=(nblk, rb // tm, n // tn),
        in_specs=[
            pl.BlockSpec((None, tm, k), lambda b, i, j: (b, i, 0), pipeline_mode=pl.Buffered(lhs_buffers)),
            _w_spec(k, tn, layer),
            pl.BlockSpec((None, tm, tn), lambda b, i, j: (b, i, j)),
            _row_param_spec(gate, tm, tn, True),
        ],
        out_specs=pl.BlockSpec((None, tm, tn), lambda b, i, j: (b, i, j)),
        out_shape=jax.ShapeDtypeStruct((nblk, rb, n), F32),
        compiler_params=_params(("arbitrary", "arbitrary", "arbitrary"), vmem),
        name="proj_residual",
    )(a3, w, x3, gate)


def _ffn_up_kernel(*refs, fresh, seq, tm):
    halo_rows = V7X_SUBLANES
    if fresh:
        (h_ref, halo_ref, wa_ref, wg_ref, cwa_ref, cwg_ref, cba_ref, cbg_ref,
         act_ref, sa_ref, sg_ref, lhs_ref) = refs

        @pl.when(pl.program_id(2) == 0)
        def _():
            first = pl.program_id(1) == 0
            halo = halo_ref[...]
            lhs_ref[0:halo_rows, :] = jnp.where(first, jnp.zeros_like(halo), halo)
            lhs_ref[halo_rows:, :] = h_ref[...]
    else:
        (h_ref, wa_ref, wg_ref, cwa_ref, cwg_ref, cba_ref, cbg_ref,
         p1a_ref, p2a_ref, p1g_ref, p2g_ref, act_ref, sa_ref, sg_ref) = refs

    def conv(up, r1, r2, cw_ref, cb_ref):
        cw = cw_ref[...]
        return cb_ref[...] + (cw[0:1] * r2 + cw[1:2] * r1 + cw[2:3] * up)

    if not fresh:
        lhs = h_ref[...]
        parts = []
        for w_ref, cw_ref, cb_ref, p1_ref, p2_ref, s_ref in (
                (wa_ref, cwa_ref, cba_ref, p1a_ref, p2a_ref, sa_ref),
                (wg_ref, cwg_ref, cbg_ref, p1g_ref, p2g_ref, sg_ref)):
            up = jnp.dot(lhs, w_ref[...], preferred_element_type=F32)
            s_ref[...] = up
            tloc = lax.broadcasted_iota(jnp.int32, up.shape, 0) % seq
            r1 = jnp.where(tloc < 1, p1_ref[...], pltpu.roll(up, 1, 0))
            r2 = jnp.where(tloc < 2, p2_ref[...], pltpu.roll(up, 2, 0))
            parts.append(conv(up, r1, r2, cw_ref, cb_ref))
        act_ref[...] = (jax.nn.silu(parts[1]) * parts[0]).astype(act_ref.dtype)
        return

    lhs = lhs_ref[...]

    def half(w_ref, cw_ref, cb_ref, s_ref):
        up = jnp.dot(lhs, w_ref[...], preferred_element_type=F32)
        s_ref[...] = up[tm:, :]
        r1 = pltpu.roll(up, 1, 0)[halo_rows:]
        r2 = pltpu.roll(up, 2, 0)[halo_rows:]
        return conv(up[halo_rows:], r1, r2, cw_ref, cb_ref)

    a = half(wa_ref, cwa_ref, cba_ref, sa_ref)
    g = half(wg_ref, cwg_ref, cbg_ref, sg_ref)
    act_ref[...] = (jax.nn.silu(g) * a).astype(act_ref.dtype)


def _ffn_up(h3, w_up, w_conv, b_conv, layer, d_ff, prev=None, seq=None):
    nblk, rb, k = h3.shape
    fresh = prev is None
    tn = _pick(d_ff, 512, V7X_LANES)
    goff = d_ff // tn
    cw = w_conv
    cb = b_conv.reshape(b_conv.shape[0], 1, 2 * d_ff)
    w_specs = [
        _w_spec(k, tn, layer),
        _w_spec(k, tn, layer, goff),
        _w_spec(cw.shape[1], tn, layer),
        _w_spec(cw.shape[1], tn, layer, goff),
        _w_spec(1, tn, layer),
        _w_spec(1, tn, layer, goff),
    ]
    w_args = [w_up, w_up, cw, cw, cb, cb]
    if fresh:
        tm = _pick(rb, 1024, V7X_SUBLANES)
        hb = tm // V7X_SUBLANES
        in_specs = [
            pl.BlockSpec((None, tm, k), lambda b, i, j: (b, i, 0)),
            pl.BlockSpec((None, V7X_SUBLANES, k), lambda b, i, j: (b, jnp.maximum(i * hb - 1, 0), 0)),
        ] + w_specs
        args = [h3, h3] + w_args
        srows = V7X_SUBLANES
        scratch = [pltpu.VMEM((tm + V7X_SUBLANES, k), BF16)]
        state_spec = pl.BlockSpec((None, None, srows, tn), lambda b, i, j: (b, i, 0, j))
    else:
        tm = rb
        p1, p2 = prev
        in_specs = [pl.BlockSpec((None, tm, k), lambda b, i, j: (b, i, 0))] + w_specs + [
            pl.BlockSpec((None, tm, tn), lambda b, i, j: (b, i, j)),
            pl.BlockSpec((None, tm, tn), lambda b, i, j: (b, i, j)),
            pl.BlockSpec((None, tm, tn), lambda b, i, j: (b, i, j + goff)),
            pl.BlockSpec((None, tm, tn), lambda b, i, j: (b, i, j + goff)),
        ]
        args = [h3] + w_args + [p1, p2, p1, p2]
        srows = rb
        scratch = []
        state_spec = pl.BlockSpec((None, None, srows, tn), lambda b, i, j: (b, i, 0, j))
    vmem = (3 * (tm + 8) * k * 2 + 4 * k * tn * 2 + 10 * (tm + 8) * tn * 4
            + 2 * tm * tn * 2 + 8 * srows * tn * 4)
    return pl.pallas_call(
        functools.partial(_ffn_up_kernel, fresh=fresh, seq=seq, tm=tm),
        grid=(nblk, rb // tm, d_ff // tn),
        in_specs=in_specs,
        out_specs=[
            pl.BlockSpec((None, tm, tn), lambda b, i, j: (b, i, j)),
            state_spec,
            state_spec,
        ],
        out_shape=[
            jax.ShapeDtypeStruct((nblk, rb, d_ff), BF16),
            jax.ShapeDtypeStruct((nblk, rb // tm, srows, d_ff), F32),
            jax.ShapeDtypeStruct((nblk, rb // tm, srows, d_ff), F32),
        ],
        scratch_shapes=scratch,
        compiler_params=_params(("arbitrary", "arbitrary", "arbitrary"), vmem),
        name="convffn_up",
    )(*args)


def _beats(other, gate, other_is_lower):
    return jnp.where(other > gate, 1.0, 0.0) + jnp.where(other == gate, 1.0, 0.0) * other_is_lower


def _attn_fresh_body(cc, q_ref, o_ref, kaug_ref, vbf_ref, kmean_ref, sd_ref, *, nblk):
    blk = MOBA_BLOCK
    scale = HEAD_DIM ** -0.5
    w = (cc + 1) * blk
    q2 = q_ref[...]
    qs = jnp.concatenate([q2[:, :HEAD_DIM], q2[:, HEAD_DIM:]], axis=0)
    if cc > 0:
        gate_t = lax.dot_general(kmean_ref[...], qs, (((1,), (1,)), ((), ())),
                                 precision=lax.Precision.HIGHEST, preferred_element_type=F32)
        riota = lax.broadcasted_iota(jnp.int32, gate_t.shape, 0)
        cnt = jnp.zeros_like(gate_t)
        for m in range(cc):
            cnt = cnt + _beats(gate_t[m:m + 1, :], gate_t, jnp.where(riota > m, 1.0, 0.0))
        keep = jnp.where(cnt < MOBA_TOPK, 1.0, 0.0) + jnp.where(riota >= cc, 1.0, 0.0)
        bias_t = jnp.where(keep > 0.5, 0.0, MASK_BIAS)
        bias_t = jnp.concatenate(
            [bias_t, jnp.zeros((HEAD_DIM - nblk, 2 * blk), F32)], axis=0)
        bias = bias_t.T.astype(BF16)
    else:
        bias = jnp.zeros((2 * blk, HEAD_DIM), BF16)
    q_aug = jnp.concatenate([qs.astype(BF16), bias], axis=1)
    s = lax.dot_general(q_aug, kaug_ref[0:w, :], (((1,), (1,)), ((), ())),
                        preferred_element_type=F32)
    row = lax.broadcasted_iota(jnp.int32, (blk, blk), 0)
    col = lax.broadcasted_iota(jnp.int32, (blk, blk), 1)
    causal = row >= col
    for hh in range(2):
        sh = s[hh * blk:(hh + 1) * blk, :] * scale - sd_ref[hh, :, 0:w]
        diag = jnp.where(causal, sh[:, cc * blk:], NEG_INF)
        sh = jnp.concatenate([sh[:, :cc * blk], diag], axis=1) if cc > 0 else diag
        m = jnp.max(sh, axis=-1, keepdims=True)
        p = jnp.exp(sh - m)
        l = jnp.sum(p, axis=-1, keepdims=True)
        o = jnp.dot(p.astype(BF16), vbf_ref[0:w, :], preferred_element_type=F32) / l
        o_ref[:, hh * HEAD_DIM:(hh + 1) * HEAD_DIM] = o.astype(o_ref.dtype)


def _attn_fresh_kernel(slope_ref, q_ref, k_ref, v_ref, o_ref,
                       kaug_ref, vbf_ref, kmean_ref, sd_ref, *, nblk):
    g = pl.program_id(1)
    c = pl.program_id(2)
    blk = MOBA_BLOCK
    t = nblk * blk

    @pl.when(c == 0)
    def _():
        k = k_ref[...]
        kaug_ref[:, 0:HEAD_DIM] = k.astype(BF16)
        key_blk = lax.broadcasted_iota(jnp.int32, (t, HEAD_DIM), 0) // blk
        lane = lax.broadcasted_iota(jnp.int32, (t, HEAD_DIM), 1)
        kaug_ref[:, HEAD_DIM:] = jnp.where(key_blk == lane, 1.0, 0.0).astype(BF16)
        vbf_ref[...] = v_ref[...].astype(BF16)
        for n in range(nblk):
            kmean_ref[n:n + 1, :] = jnp.mean(k[n * blk:(n + 1) * blk, :], axis=0, keepdims=True)
        d0 = (lax.broadcasted_iota(jnp.int32, (blk, t), 0)
              - lax.broadcasted_iota(jnp.int32, (blk, t), 1)).astype(F32)
        for hh in range(2):
            sd_ref[hh] = slope_ref[2 * g + hh] * d0

    for cc in range(nblk):
        pl.when(c == cc)(functools.partial(
            _attn_fresh_body, cc, q_ref, o_ref, kaug_ref, vbf_ref, kmean_ref, sd_ref, nblk=nblk))


def _attn_fresh(q3, k3, v3, slopes):
    b, t, q_w = q3.shape
    kv_heads = k3.shape[2] // HEAD_DIM
    assert q_w == 2 * kv_heads * HEAD_DIM and t % MOBA_BLOCK == 0
    nblk = t // MOBA_BLOCK
    assert nblk <= HEAD_DIM
    blk = MOBA_BLOCK
    vmem = (2 * (blk * 2 * HEAD_DIM * 4 + 2 * t * HEAD_DIM * 4 + blk * 2 * HEAD_DIM * 2)
            + 3 * t * HEAD_DIM * 2 + 2 * blk * t * 4 + 8 * 2 * blk * t * 4)
    return pl.pallas_call(
        functools.partial(_attn_fresh_kernel, nblk=nblk),
        grid=(b, kv_heads, nblk),
        in_specs=[
            pl.BlockSpec(memory_space=pltpu.SMEM),
            pl.BlockSpec((None, blk, 2 * HEAD_DIM), lambda bb, g, c: (bb, c, g)),
            pl.BlockSpec((None, t, HEAD_DIM), lambda bb, g, c: (bb, 0, g)),
            pl.BlockSpec((None, t, HEAD_DIM), lambda bb, g, c: (bb, 0, g)),
        ],
        out_specs=pl.BlockSpec((None, blk, 2 * HEAD_DIM), lambda bb, g, c: (bb, c, g)),
        out_shape=jax.ShapeDtypeStruct((b, t, q_w), BF16),
        scratch_shapes=[
            pltpu.VMEM((t, 2 * HEAD_DIM), BF16),
            pltpu.VMEM((t, HEAD_DIM), BF16),
            pltpu.VMEM((nblk, HEAD_DIM), F32),
            pltpu.VMEM((2, blk, t), F32),
        ],
        compiler_params=_params(("arbitrary", "arbitrary", "arbitrary"), vmem),
        name="moba_fresh",
    )(slopes, q3, k3, v3)


def _page_heads_on_lanes(p_ref, page, kvh):
    return jnp.concatenate([p_ref[pl.ds(g, page, stride=kvh), :] for g in range(kvh)], axis=-1)


def _page_specs(layer, npg, page_rows, hd):
    def spec(j):
        return pl.BlockSpec((None, None, page_rows, hd),
                            lambda i, n, pt: (layer, pt[i, n * npg + j], 0, 0))
    return [spec(j) for j in range(npg)]


def _paged_scores_kernel(pt_ref, wq_ref, *refs, page, kvh, npg):
    pages = refs[:npg]
    sc_ref, km_ref = refs[npg:]
    wq = wq_ref[...]
    hd = pages[0].shape[1]
    sums = []
    for j, p_ref in enumerate(pages):
        kcat = _page_heads_on_lanes(p_ref, page, kvh)
        sc_ref[j * page:(j + 1) * page, :] = jnp.dot(kcat.astype(BF16), wq, preferred_element_type=F32)
        sums.append(jnp.sum(p_ref[...].reshape(page, kvh, hd), axis=0))
    for b in range(npg // 2):
        km_ref[b] = (sums[2 * b] + sums[2 * b + 1]) * (1.0 / (2 * page))


def _paged_scores(page_table, wq_bf, cache, layer, npg):
    s, n_pages = page_table.shape
    depth, n_pool, page, kvh, hd = cache.shape
    assert 2 * page == MOBA_BLOCK and n_pages % npg == 0 and npg % 2 == 0
    kv_w = kvh * hd
    cache_rows = cache.reshape(depth, n_pool, page * kvh, hd)
    nblk = n_pages // 2
    past = n_pages * page
    vmem = 2 * (kv_w * V7X_LANES * 2 + npg * page * kv_w * 4 + npg * page * V7X_LANES * 4) + 6 * page * kv_w * 4
    grid_spec = pltpu.PrefetchScalarGridSpec(
        num_scalar_prefetch=1,
        grid=(s, n_pages // npg),
        in_specs=[pl.BlockSpec((None, kv_w, V7X_LANES), lambda i, n, pt: (i, 0, 0))]
        + _page_specs(layer, npg, page * kvh, hd),
        out_specs=[
            pl.BlockSpec((None, npg * page, V7X_LANES), lambda i, n, pt: (i, n, 0)),
            pl.BlockSpec((None, npg // 2, kvh, hd), lambda i, n, pt: (i, n, 0, 0)),
        ],
    )
    return pl.pallas_call(
        functools.partial(_paged_scores_kernel, page=page, kvh=kvh, npg=npg),
        grid_spec=grid_spec,
        out_shape=[
            jax.ShapeDtypeStruct((s, past, V7X_LANES), F32),
            jax.ShapeDtypeStruct((s, nblk, kvh, hd), F32),
        ],
        compiler_params=_params(("arbitrary", "arbitrary"), vmem),
        name="moba_paged_scores",
    )(page_table, wq_bf, *([cache_rows] * npg))


def _paged_attend_kernel(pt_ref, wq_ref, wqbf_ref, sc_ref, km_ref, knew_ref, vnew_ref,
                         slope_ref, tcol_ref, *refs, page, kvh, npg, nblk, tnew, past):
    pages = refs[:npg]
    o_ref, sel_ref, prob_ref, linv_ref, sq_ref, vpad_ref, acc_ref = refs[npg:]
    n = pl.program_id(1)
    blk = MOBA_BLOCK
    scale = HEAD_DIM ** -0.5
    lanes = V7X_LANES
    slope = slope_ref[...]
    tcol = tcol_ref[...]
    koff = lax.broadcasted_iota(jnp.int32, (blk, lanes), 0).astype(F32)

    def block_scores(i):
        r0 = pl.multiple_of(i * blk, blk)
        raw = sc_ref[pl.ds(r0, blk), :]
        dist = (tcol + lax.convert_element_type(past - i * blk, F32)) - koff
        s = raw * scale - slope * dist
        return jnp.where(sel_ref[pl.ds(i, 1), :] > 0.5, s, NEG_INF)

    @pl.when(n == 0)
    def _():
        gate = jnp.zeros((nblk, lanes), F32)
        for g in range(kvh):
            gate = gate + jnp.dot(km_ref[pl.ds(g, nblk, stride=kvh), :],
                                  wq_ref[g * HEAD_DIM:(g + 1) * HEAD_DIM, :],
                                  precision=lax.Precision.HIGHEST, preferred_element_type=F32)
        riota = lax.broadcasted_iota(jnp.int32, (nblk, lanes), 0)
        cnt = jnp.zeros((nblk, lanes), F32)
        for m in range(nblk):
            cnt = cnt + _beats(gate[m:m + 1, :], gate, jnp.where(riota > m, 1.0, 0.0))
        sel_ref[...] = jnp.where(cnt < MOBA_TOPK, 1.0, 0.0)

        s_cur = jnp.dot(knew_ref[...].astype(BF16), wqbf_ref[...], preferred_element_type=F32)
        off = lax.broadcasted_iota(jnp.int32, (tnew, lanes), 0).astype(F32)
        s_cur = s_cur * scale - slope * (tcol - off)
        s_cur = jnp.where(off <= tcol, s_cur, NEG_INF)
        m0 = jnp.max(s_cur, axis=0, keepdims=True)

        def max_body(i, m):
            return jnp.maximum(m, jnp.max(block_scores(i), axis=0, keepdims=True))

        mx = lax.fori_loop(0, nblk, max_body, m0)
        p_cur = jnp.exp(s_cur - mx)

        def sum_body(i, l):
            p = jnp.exp(block_scores(i) - mx)
            prob_ref[pl.ds(pl.multiple_of(i * blk, blk), blk), :] = p
            return l + jnp.sum(p, axis=0, keepdims=True)

        l = lax.fori_loop(0, nblk, sum_body, jnp.sum(p_cur, axis=0, keepdims=True))
        linv = 1.0 / l
        linv_ref[...] = linv
        sq_ref[...] = jnp.zeros_like(sq_ref)
        sq_ref[0:tnew, :] = p_cur * linv
        vpad_ref[...] = jnp.zeros_like(vpad_ref)
        vpad_ref[0:tnew, :] = vnew_ref[...].astype(BF16)
        acc_ref[...] = jnp.dot(sq_ref[...].T.astype(BF16), vpad_ref[...], preferred_element_type=F32)

    linv = linv_ref[...]
    acc = acc_ref[...]
    for j, p_ref in enumerate(pages):
        r0 = pl.multiple_of((n * npg + j) * page, page)
        pt_j = (prob_ref[pl.ds(r0, page), :] * linv).T.astype(BF16)
        vcat = _page_heads_on_lanes(p_ref, page, kvh).astype(BF16)
        acc = acc + jnp.dot(pt_j, vcat, preferred_element_type=F32)
    acc_ref[...] = acc

    @pl.when(n == pl.num_programs(1) - 1)
    def _():
        rows = o_ref.shape[1]
        for g in range(o_ref.shape[0]):
            o_ref[g] = acc_ref[g * rows:(g + 1) * rows, g * HEAD_DIM:(g + 1) * HEAD_DIM].astype(o_ref.dtype)


def _paged_attend(page_table, wq, wq_bf, scores, kmean, k_new, v_new, slope_col, t_col, cache, layer, npg):
    s, n_pages = page_table.shape
    depth, n_pool, page, kvh, hd = cache.shape
    kv_w = kvh * hd
    cache_rows = cache.reshape(depth, n_pool, page * kvh, hd)
    nblk = n_pages // 2
    past = n_pages * page
    tnew = k_new.shape[1]
    rows = 2 * tnew
    kmean_rows = kmean.reshape(s, nblk * kvh, hd)
    vmem = (2 * (kv_w * V7X_LANES * 6 + past * V7X_LANES * 4 + nblk * kv_w * 4 + npg * page * kv_w * 4)
            + past * V7X_LANES * 4 + 3 * V7X_LANES * kv_w * 4 + 16 * MOBA_BLOCK * V7X_LANES * 4
            + 4 * page * kv_w * 4)
    grid_spec = pltpu.PrefetchScalarGridSpec(
        num_scalar_prefetch=1,
        grid=(s, n_pages // npg),
        in_specs=[
            pl.BlockSpec((None, kv_w, V7X_LANES), lambda i, n, pt: (i, 0, 0)),
            pl.BlockSpec((None, kv_w, V7X_LANES), lambda i, n, pt: (i, 0, 0)),
            pl.BlockSpec((None, past, V7X_LANES), lambda i, n, pt: (i, 0, 0)),
            pl.BlockSpec((None, nblk * kvh, hd), lambda i, n, pt: (i, 0, 0)),
            pl.BlockSpec((None, tnew, kv_w), lambda i, n, pt: (i, 0, 0)),
            pl.BlockSpec((None, tnew, kv_w), lambda i, n, pt: (i, 0, 0)),
            pl.BlockSpec((1, V7X_LANES), lambda i, n, pt: (0, 0)),
            pl.BlockSpec((1, V7X_LANES), lambda i, n, pt: (0, 0)),
        ] + _page_specs(layer, npg, page * kvh, hd),
        out_specs=pl.BlockSpec((None, kvh, rows, hd), lambda i, n, pt: (i, 0, 0, 0)),
        scratch_shapes=[
            pltpu.VMEM((nblk, V7X_LANES), F32),
            pltpu.VMEM((past, V7X_LANES), F32),
            pltpu.VMEM((1, V7X_LANES), F32),
            pltpu.VMEM((V7X_LANES, V7X_LANES), F32),
            pltpu.VMEM((V7X_LANES, kv_w), BF16),
            pltpu.VMEM((V7X_LANES, kv_w), F32),
        ],
    )
    return pl.pallas_call(
        functools.partial(_paged_attend_kernel, page=page, kvh=kvh, npg=npg, nblk=nblk, tnew=tnew, past=past),
        grid_spec=grid_spec,
        out_shape=jax.ShapeDtypeStruct((s, kvh, rows, hd), BF16),
        compiler_params=_params(("arbitrary", "arbitrary"), vmem),
        name="moba_paged_attend",
    )(page_table, wq, wq_bf, scores, kmean_rows, k_new, v_new, slope_col, t_col, *([cache_rows] * npg))


def _ssm_prep_kernel(lr_ref, li_ref, ldt_ref, brt_ref, bit_ref, are_ref, aim_ref, bbr_ref, bbi_ref):
    lr = lr_ref[...]
    li = li_ref[...]
    dt = jnp.exp(ldt_ref[...])
    mag = jnp.exp(lr * dt)
    a_re = mag * jnp.cos(li * dt)
    a_im = mag * jnp.sin(li * dt)
    den = lr * lr + li * li
    q_re = ((a_re - 1.0) * lr + a_im * li) / den
    q_im = (a_im * lr - (a_re - 1.0) * li) / den
    are_ref[...] = a_re
    aim_ref[...] = a_im
    br = brt_ref[...]
    bi = bit_ref[...]
    bbr_ref[...] = q_re * br - q_im * bi
    bbi_ref[...] = q_re * bi + q_im * br


def _ssm_prep(lam_re, lam_im, log_dt, b_re, b_im):
    depth, g, p = lam_re.shape
    i = b_re.shape[3]
    brt = jnp.swapaxes(b_re, 2, 3)
    bit = jnp.swapaxes(b_im, 2, 3)
    vec = pl.BlockSpec((None, g, 1, p), lambda l: (l, 0, 0, 0))
    mat = pl.BlockSpec((None, g, i, p), lambda l: (l, 0, 0, 0))
    return pl.pallas_call(
        _ssm_prep_kernel,
        grid=(depth,),
        in_specs=[vec, vec, pl.BlockSpec((None, g, 1, 1), lambda l: (l, 0, 0, 0)), mat, mat],
        out_specs=[vec, vec, mat, mat],
        out_shape=[
            jax.ShapeDtypeStruct((depth, g, 1, p), F32),
            jax.ShapeDtypeStruct((depth, g, 1, p), F32),
            jax.ShapeDtypeStruct((depth, g, i, p), F32),
            jax.ShapeDtypeStruct((depth, g, i, p), F32),
        ],
        compiler_params=_params(("arbitrary",), 32 << 20),
        name="s5_discretise",
    )(lam_re.reshape(depth, g, 1, p), lam_im.reshape(depth, g, 1, p),
      log_dt.reshape(depth, g, 1, 1), brt, bit)


def _ssm_scan_kernel(u_ref, ire_ref, iim_ref, are_ref, aim_ref, bre_ref, bim_ref,
                     cre_ref, cim_ref, d_ref, *rest, with_y, n_chunks, tau_b):
    if with_y:
        y_ref, ere_ref, eim_ref, hre_ref, him_ref, bure_ref, buim_ref = rest
    else:
        ere_ref, eim_ref, hre_ref, him_ref, bure_ref, buim_ref = rest
    k = pl.program_id(1)

    @pl.when(k == 0)
    def _():
        hre_ref[...] = ire_ref[...]
        him_ref[...] = iim_ref[...]

    def chunk(c, carry):
        cu = pl.multiple_of(c * CHUNK_IN, CHUNK_IN)
        cs = pl.multiple_of(c * CHUNK_ST, CHUNK_ST)
        u_c = u_ref[:, pl.ds(cu, CHUNK_IN)]
        ub = u_c.astype(BF16)
        bure_ref[...] = jnp.dot(ub, bre_ref[c], preferred_element_type=F32)
        buim_ref[...] = jnp.dot(ub, bim_ref[c], preferred_element_type=F32)
        ar = jnp.broadcast_to(are_ref[:, pl.ds(cs, CHUNK_ST)], (N_SEG, CHUNK_ST))
        ai = jnp.broadcast_to(aim_ref[:, pl.ds(cs, CHUNK_ST)], (N_SEG, CHUNK_ST))
        hr = hre_ref[:, pl.ds(cs, CHUNK_ST)]
        hi = him_ref[:, pl.ds(cs, CHUNK_ST)]
        for t in range(tau_b):
            rs = slice(t * N_SEG, (t + 1) * N_SEG)
            nr = ar * hr - ai * hi + bure_ref[rs, :]
            ni = ar * hi + ai * hr + buim_ref[rs, :]
            hr, hi = nr, ni
            if with_y:
                bure_ref[rs, :] = hr
                buim_ref[rs, :] = hi
        hre_ref[:, pl.ds(cs, CHUNK_ST)] = hr
        him_ref[:, pl.ds(cs, CHUNK_ST)] = hi
        if with_y:
            ych = (jnp.dot(bure_ref[...].astype(BF16), cre_ref[c], preferred_element_type=F32)
                   - jnp.dot(buim_ref[...].astype(BF16), cim_ref[c], preferred_element_type=F32))
            y = ych + d_ref[:, pl.ds(cu, CHUNK_IN)] * u_c
            y_ref[:, pl.ds(cu, CHUNK_IN)] = jax.nn.gelu(y)
        return carry

    lax.fori_loop(0, n_chunks, chunk, 0)

    @pl.when(k == pl.num_programs(1) - 1)
    def _():
        ere_ref[...] = hre_ref[...]
        eim_ref[...] = him_ref[...]


def _ssm_scan(u3, init_re, init_im, a_re, a_im, bre, bim, cre, cim, d_skip, with_y):
    nb, lr, s_w = u3.shape
    ns = a_re.shape[1]
    n_chunks = bre.shape[0]
    steps = lr // N_SEG
    tau_b = _pick(steps, 32, 1)
    rb = tau_b * N_SEG
    st_spec = pl.BlockSpec((None, N_SEG, ns), lambda b, k: (b, 0, 0))
    full = lambda shape: pl.BlockSpec(shape, lambda b, k: (0,) * len(shape))
    u_spec = pl.BlockSpec((None, rb, s_w), lambda b, k: (b, k, 0))
    out_specs = [st_spec, st_spec]
    out_shape = [jax.ShapeDtypeStruct((nb, N_SEG, ns), F32)] * 2
    if with_y:
        out_specs = [u_spec] + out_specs
        out_shape = [jax.ShapeDtypeStruct((nb, lr, s_w), F32)] + out_shape
    vmem = (4 * rb * s_w * 4 + 8 * N_SEG * ns * 4 + 4 * ns * 4
            + 2 * 4 * n_chunks * CHUNK_IN * CHUNK_ST * 2 + 8 * rb * CHUNK_ST * 4)
    return pl.pallas_call(
        functools.partial(_ssm_scan_kernel, with_y=with_y, n_chunks=n_chunks, tau_b=tau_b),
        grid=(nb, steps // tau_b),
        in_specs=[u_spec, st_spec, st_spec, full((1, ns)), full((1, ns)),
                  full(bre.shape), full(bim.shape), full(cre.shape), full(cim.shape),
                  full((1, s_w))],
        out_specs=out_specs,
        out_shape=out_shape,
        scratch_shapes=[
            pltpu.VMEM((N_SEG, ns), F32),
            pltpu.VMEM((N_SEG, ns), F32),
            pltpu.VMEM((rb, CHUNK_ST), F32),
            pltpu.VMEM((rb, CHUNK_ST), F32),
        ],
        compiler_params=_params(("arbitrary", "arbitrary"), vmem),
        name="s5_scan" if with_y else "s5_segment_ends",
    )(u3, init_re, init_im, a_re, a_im, bre, bim, cre, cim, d_skip)


def _ssm_carry_kernel(ere_ref, eim_ref, are_ref, aim_ref, ire_ref, iim_ref, fre_ref, fim_ref, *, seg_len):
    br = are_ref[...]
    bi = aim_ref[...]
    pr = jnp.ones_like(br)
    pi = jnp.zeros_like(br)
    e = seg_len
    while e:
        if e & 1:
            pr, pi = pr * br - pi * bi, pr * bi + pi * br
        br, bi = br * br - bi * bi, 2.0 * br * bi
        e >>= 1
    hr = jnp.zeros_like(pr)
    hi = jnp.zeros_like(pr)
    for j in range(N_SEG):
        ire_ref[j:j + 1, :] = hr
        iim_ref[j:j + 1, :] = hi
        er = ere_ref[j:j + 1, :]
        ei = eim_ref[j:j + 1, :]
        hr, hi = pr * hr - pi * hi + er, pr * hi + pi * hr + ei
    fre_ref[...] = hr
    fim_ref[...] = hi


def _ssm_carry(end_re, end_im, a_re, a_im, seg_len):
    nb, _, ns = end_re.shape
    st = pl.BlockSpec((None, N_SEG, ns), lambda b: (b, 0, 0))
    vec = pl.BlockSpec((1, ns), lambda b: (0, 0))
    fin = pl.BlockSpec((None, 1, ns), lambda b: (b, 0, 0))
    return pl.pallas_call(
        functools.partial(_ssm_carry_kernel, seg_len=seg_len),
        grid=(nb,),
        in_specs=[st, st, vec, vec],
        out_specs=[st, st, fin, fin],
        out_shape=[jax.ShapeDtypeStruct((nb, N_SEG, ns), F32)] * 2
        + [jax.ShapeDtypeStruct((nb, 1, ns), F32)] * 2,
        compiler_params=_params(("arbitrary",), 32 << 20),
        name="s5_segment_carry",
    )(end_re, end_im, a_re, a_im)


def _block_diag_in(bbt):
    g, i, p = bbt.shape
    nc = g // GROUPS_PER_CHUNK
    eye = jnp.eye(GROUPS_PER_CHUNK, dtype=bbt.dtype)
    x = bbt.reshape(nc, GROUPS_PER_CHUNK, i, p)
    out = x[:, :, :, None, :] * eye[None, :, None, :, None]
    return out.reshape(nc, GROUPS_PER_CHUNK * i, GROUPS_PER_CHUNK * p).astype(BF16)


def _block_diag_out(c):
    g, i, p = c.shape
    nc = g // GROUPS_PER_CHUNK
    eye = jnp.eye(GROUPS_PER_CHUNK, dtype=c.dtype)
    x = jnp.swapaxes(c.reshape(nc, GROUPS_PER_CHUNK, i, p), 2, 3)
    out = x[:, :, :, None, :] * eye[None, :, None, :, None]
    return out.reshape(nc, GROUPS_PER_CHUNK * p, GROUPS_PER_CHUNK * i).astype(BF16)


def kernel(x_prompt, x_sample, c_prompt, c_sample, cache_k, cache_v, state_ssm_re, state_ssm_im,
           state_conv, page_table, w_ada, b_ada, norm_attn, w_in, w_attn_proj, lam_re, lam_im,
           log_dt, ssm_b_re, ssm_b_im, ssm_c_re, ssm_c_im, ssm_d, w_glu, w_ssm_proj, w_out,
           norm_ffn, w_up, w_conv, b_conv, w_down, norm_final):
    depth = w_ada.shape[0]
    nb, t, d = x_prompt.shape
    ns_seq, ts, _ = x_sample.shape
    q_w = w_attn_proj.shape[1]
    n_heads = q_w // HEAD_DIM
    kv_heads = cache_k.shape[3]
    kv_w = kv_heads * HEAD_DIM
    s_w = ssm_d.shape[1]
    n_groups = lam_re.shape[1]
    n_state = n_groups * S5_STATE
    d_ff = w_down.shape[1]
    n_pages = page_table.shape[1]
    page = cache_k.shape[2]
    past = n_pages * page
    assert ns_seq == N_SEG and t % (N_SEG * N_SEG) == 0 and n_groups % GROUPS_PER_CHUNK == 0
    n_col = kv_heads * 2 * ts
    assert n_col <= V7X_LANES and n_heads == 2 * kv_heads
    assert ssm_b_re.shape[2:] == (S5_STATE, S5_GROUP)
    u_col = q_w + 2 * kv_w
    g_col = u_col + s_w

    n_c = nb + ns_seq
    c_rows = -(-n_c // V7X_SUBLANES) * V7X_SUBLANES
    c_all = jnp.concatenate([c_prompt, c_sample, jnp.zeros((c_rows - n_c, d), F32)], axis=0)
    mod = _adaln(c_all, w_ada, b_ada)

    a_re_all, a_im_all, bbr_all, bbi_all = _ssm_prep(lam_re, lam_im, log_dt, ssm_b_re, ssm_b_im)
    slopes = 2.0 ** (-8.0 * jnp.arange(1, n_heads + 1, dtype=F32) / n_heads)

    col = jnp.arange(V7X_LANES)
    col_head = jnp.minimum(2 * (col // (2 * ts)) + (col // ts) % 2, n_heads - 1)
    slope_col = slopes[col_head].reshape(1, V7X_LANES)
    t_col = (col % ts).astype(F32).reshape(1, V7X_LANES)

    xp = x_prompt
    xs = x_sample.reshape(1, ns_seq * ts, d)
    seg = t // N_SEG
    keep = w_conv.shape[1] - 1
    assert keep == 2 and past % MOBA_BLOCK == 0 and ts <= MOBA_BLOCK
    npg = _pick(n_pages, PAGES_PER_STEP, 2)
    outs ={k: [] for k in ("kp", "vp", "ks", "vs", "hpr", "hpi", "hsr", "hsi", "cp", "cs")}

    w_in_bf = w_in.astype(BF16)
    w_attn_bf = w_attn_proj.astype(BF16)
    w_glu_bf = w_glu.astype(BF16)
    w_ssm_bf = w_ssm_proj.astype(BF16)
    w_out_bf = w_out.astype(BF16)
    w_up_bf = w_up.astype(BF16)
    w_down_bf = w_down.astype(BF16)

    for l in range(depth):
        a_re = a_re_all[l].reshape(1, n_state)
        a_im = a_im_all[l].reshape(1, n_state)
        bre = _block_diag_in(bbr_all[l])
        bim = _block_diag_in(bbi_all[l])
        cre = _block_diag_out(ssm_c_re[l])
        cim = _block_diag_out(ssm_c_im[l])
        d_skip = ssm_d[l].reshape(1, s_w)
        mod_l = mod[l].reshape(c_rows, 6, d)

        def in_proj(h3):
            return (_mm_plain(h3, w_in_bf, l, 0, q_w, F32),
                    _mm_plain(h3, w_in_bf, l, q_w, kv_w, F32),
                    _mm_plain(h3, w_in_bf, l, q_w + kv_w, kv_w, F32),
                    _mm_plain(h3, w_in_bf, l, u_col, s_w, F32),
                    _mm_plain(h3, w_in_bf, l, g_col, 2 * d, F32))

        def mods_for(lo, n, rep):
            m = mod_l[lo:lo + n]
            if rep == 1:
                return [m[:, i].reshape(n, 1, d) for i in range(6)]
            return [jnp.repeat(m[:, i], rep, axis=0).reshape(1, n * rep, d) for i in range(6)]

        sh1, sc1, g1, sh2, sc2, g2 = mods_for(0, nb, 1)
        h = _rmsnorm(xp, norm_attn[l], sc1, sh1)
        q, k, v, u, gates = in_proj(h)
        attn = _attn_fresh(q, k, v, slopes)
        u_perm = u.reshape(nb, N_SEG, seg, s_w).swapaxes(1, 2).reshape(nb, t, s_w)
        zero_st = jnp.zeros((nb, N_SEG, n_state), F32)
        end_re, end_im = _ssm_scan(u_perm, zero_st, zero_st, a_re, a_im, bre, bim, cre, cim, d_skip, False)
        ini_re, ini_im, fin_re, fin_im = _ssm_carry(end_re, end_im, a_re, a_im, seg)
        y_perm, _, _ = _ssm_scan(u_perm, ini_re, ini_im, a_re, a_im, bre, bim, cre, cim, d_skip, True)
        y = y_perm.reshape(nb, seg, N_SEG, s_w).swapaxes(1, 2).reshape(nb, t, s_w)
        ssm = _mm_glu(y, w_glu_bf, l)
        merged = _mm_merge(attn, ssm, w_attn_bf, w_ssm_bf, l, gates, d)
        xp = _mm_resid(merged, w_out_bf, l, xp, g1, 1024, 512)
        h2 = _rmsnorm(xp, norm_ffn[l], sc2, sh2)
        act, tail_a, tail_g = _ffn_up(h2, w_up_bf, w_conv, b_conv, l, d_ff)
        xp = _mm_resid(act, w_down_bf, l, xp, g2, 1024, 256, lhs_buffers=1)
        outs["kp"].append(k.reshape(nb, t, kv_heads, HEAD_DIM))
        outs["vp"].append(v.reshape(nb, t, kv_heads, HEAD_DIM))
        outs["hpr"].append(fin_re.reshape(nb, n_groups, S5_STATE))
        outs["hpi"].append(fin_im.reshape(nb, n_groups, S5_STATE))
        keep = w_conv.shape[1] - 1
        outs["cp"].append(jnp.concatenate([tail_a[:, -1], tail_g[:, -1]], axis=-1)[:, V7X_SUBLANES - keep:, :])

        sh1, sc1, g1, sh2, sc2, g2 = mods_for(nb, ns_seq, ts)
        h = _rmsnorm(xs, norm_attn[l], sc1, sh1)
        q, k, v, u, gates = in_proj(h)
        q_s = q.reshape(ns_seq, ts, kv_heads, 2, HEAD_DIM)
        k_new = k.reshape(ns_seq, ts, kv_w)
        v_new = v.reshape(ns_seq, ts, kv_w)
        eye = jnp.eye(kv_heads, dtype=F32)
        wq = (jnp.transpose(q_s, (0, 2, 4, 3, 1))[:, :, :, None, :, :]
              * eye[None, :, None, :, None, None]).reshape(ns_seq, kv_w, n_col)
        wq = jnp.pad(wq, ((0, 0), (0, 0), (0, V7X_LANES - n_col)))
        wq_bf = wq.astype(BF16)
        scores, kmean = _paged_scores(page_table, wq_bf, cache_k, l, npg)
        o_s = _paged_attend(page_table, wq, wq_bf, scores, kmean, k_new, v_new,
                            slope_col, t_col, cache_v, l, npg)
        attn = jnp.transpose(o_s.reshape(ns_seq, kv_heads, 2, ts, HEAD_DIM), (0, 3, 1, 2, 4))
        attn = attn.reshape(1, ns_seq * ts, q_w)
        st_re = state_ssm_re[l].reshape(1, ns_seq, n_state)
        st_im = state_ssm_im[l].reshape(1, ns_seq, n_state)
        u_perm = jnp.swapaxes(u.reshape(ns_seq, ts, s_w), 0, 1).reshape(1, ts * ns_seq, s_w)
        y_perm, e_re, e_im = _ssm_scan(u_perm, st_re, st_im, a_re, a_im, bre, bim, cre, cim, d_skip, True)
        y = jnp.swapaxes(y_perm.reshape(ts, ns_seq, s_w), 0, 1).reshape(1, ns_seq * ts, s_w)
        ssm = _mm_glu(y, w_glu_bf, l)
        merged = _mm_merge(attn, ssm, w_attn_bf, w_ssm_bf, l, gates, d)
        xs = _mm_resid(merged, w_out_bf, l, xs, g1, 1024, 1024)
        h2 = _rmsnorm(xs, norm_ffn[l], sc2, sh2)
        keep = w_conv.shape[1] - 1
        cprev = state_conv[l]
        prev1 = jnp.concatenate([cprev[:, keep - 1:keep], jnp.zeros((ns_seq, ts - 1, 2 * d_ff), F32)], axis=1)
        prev2 = jnp.concatenate([cprev[:, keep - 2:keep], jnp.zeros((ns_seq, ts - 2, 2 * d_ff), F32)], axis=1)
        prev1 = prev1.reshape(1, ns_seq * ts, 2 * d_ff)
        prev2 = prev2.reshape(1, ns_seq * ts, 2 * d_ff)
        act, up_a, up_g = _ffn_up(h2, w_up_bf, w_conv, b_conv, l, d_ff, prev=(prev1, prev2), seq=ts)
        xs = _mm_resid(act, w_down_bf, l, xs, g2, 512, 512)
        outs["ks"].append(k_new.reshape(ns_seq, ts, kv_heads, HEAD_DIM))
        outs["vs"].append(v_new.reshape(ns_seq, ts, kv_heads, HEAD_DIM))
        outs["hsr"].append(e_re.reshape(ns_seq, n_groups, S5_STATE))
        outs["hsi"].append(e_im.reshape(ns_seq, n_groups, S5_STATE))
        up_full = jnp.concatenate([up_a, up_g], axis=-1).reshape(ns_seq, ts, 2 * d_ff)
        outs["cs"].append(up_full[:, ts - keep:, :])

    y_prompt = _rmsnorm(xp, norm_final, out_dtype=F32)
    y_sample = _rmsnorm(xs, norm_final, out_dtype=F32).reshape(ns_seq, ts, d)
    st = lambda k: jnp.stack(outs[k])
    return (y_prompt, y_sample, st("kp"), st("vp"), st("ks"), st("vs"),
            st("hpr"), st("hpi"), st("hsr"), st("hsi"), st("cp"), st("cs"))
```

```python
import functools
import math

import jax
import jax.numpy as jnp
from jax import lax
from jax.experimental import pallas as pl
from jax.experimental.pallas import tpu as pltpu

F32 = jnp.float32
BF16 = jnp.bfloat16

V7X_LANES = 128
V7X_SUBLANES = 8
V7X_VMEM_BYTES = 64 * 2**20
VMEM_BUDGET = V7X_VMEM_BYTES - 8 * 2**20

HEAD_DIM = 128
MOBA_BLOCK = 256
MOBA_TOPK = 3
S5_GROUP = 16
S5_STATE = 64
GROUPS_PER_CHUNK = 8
CHUNK_IN = GROUPS_PER_CHUNK * S5_GROUP
CHUNK_ST = GROUPS_PER_CHUNK * S5_STATE
N_SEG = V7X_SUBLANES
PAGES_PER_STEP = 16
EPS = 1e-6
NEG_INF = float("-inf")
MASK_BIAS = -1e30


def _pick(dim, pref, align):
    t = min(pref, dim)
    t -= t % align
    while t >= align:
        if dim % t == 0:
            return t
        t -= align
    return dim


def _params(sem, vmem_bytes):
    limit = int(min(max(vmem_bytes * 5 // 4 + (4 << 20), 32 << 20), VMEM_BUDGET))
    return pltpu.CompilerParams(dimension_semantics=sem, vmem_limit_bytes=limit)


def _ada_kernel(c_ref, w_ref, b_ref, o_ref):
    s = jax.nn.silu(c_ref[...]).astype(BF16)
    w = w_ref[...].astype(BF16)
    o_ref[...] = jnp.dot(s, w, preferred_element_type=F32) + b_ref[...]


def _adaln(c_all, w_ada, b_ada):
    depth, d, n = w_ada.shape
    rows = c_all.shape[0]
    tn = _pick(n, 512, V7X_LANES)
    vmem = 2 * d * tn * 4 + d * tn * 2 + 4 * rows * (d + tn) * 4
    return pl.pallas_call(
        _ada_kernel,
        grid=(depth, n // tn),
        in_specs=[
            pl.BlockSpec((rows, d), lambda l, j: (0, 0)),
            pl.BlockSpec((None, d, tn), lambda l, j: (l, 0, j)),
            pl.BlockSpec((None, 1, tn), lambda l, j: (l, 0, j)),
        ],
        out_specs=pl.BlockSpec((None, rows, tn), lambda l, j: (l, 0, j)),
        out_shape=jax.ShapeDtypeStruct((depth, rows, n), F32),
        compiler_params=_params(("arbitrary", "arbitrary"), vmem),
        name="adaln",
    )(c_all, w_ada, b_ada.reshape(depth, 1, n))


def _norm_kernel(x_ref, g_ref, *rest, modulate):
    if modulate:
        sc_ref, sh_ref, o_ref = rest
    else:
        (o_ref,) = rest
    x = x_ref[...]
    y = x * lax.rsqrt(jnp.mean(x * x, axis=-1, keepdims=True) + EPS)
    y = y * g_ref[...]
    if modulate:
        y = y * (1.0 + sc_ref[...]) + sh_ref[...]
    o_ref[...] = y.astype(o_ref.dtype)


def _row_param_spec(p, tr, tn, col_blocked):
    shared = p.shape[1] == 1
    rows = 1 if shared else tr
    if col_blocked:
        if shared:
            return pl.BlockSpec((None, rows, tn), lambda b, i, j: (b, 0, j))
        return pl.BlockSpec((None, rows, tn), lambda b, i, j: (b, i, j))
    if shared:
        return pl.BlockSpec((None, rows, tn), lambda b, i: (b, 0, 0))
    return pl.BlockSpec((None, rows, tn), lambda b, i: (b, i, 0))


def _rmsnorm(x3, gamma, scale=None, shift=None, out_dtype=BF16):
    nblk, rb, d = x3.shape
    tr = _pick(rb, 256, V7X_SUBLANES)
    modulate = scale is not None
    in_specs = [
        pl.BlockSpec((None, tr, d), lambda b, i: (b, i, 0)),
        pl.BlockSpec((1, d), lambda b, i: (0, 0)),
    ]
    args = [x3, gamma.reshape(1, d)]
    if modulate:
        in_specs += [_row_param_spec(scale, tr, d, False), _row_param_spec(shift, tr, d, False)]
        args += [scale, shift]
    vmem = 2 * tr * d * (4 + 4) + 6 * tr * d * 4
    return pl.pallas_call(
        functools.partial(_norm_kernel, modulate=modulate),
        grid=(nblk, rb // tr),
        in_specs=in_specs,
        out_specs=pl.BlockSpec((None, tr, d), lambda b, i: (b, i, 0)),
        out_shape=jax.ShapeDtypeStruct((nblk, rb, d), out_dtype),
        compiler_params=_params(("arbitrary", "arbitrary"), vmem),
        name="rmsnorm_mod",
    )(*args)


def _mm_plain_kernel(a_ref, w_ref, o_ref):
    o_ref[...] = jnp.dot(a_ref[...], w_ref[...], preferred_element_type=F32).astype(o_ref.dtype)


def _w_spec(k, tn, layer, col_block0=0):
    return pl.BlockSpec((None, k, tn), lambda b, i, j: (layer, 0, j + col_block0))


def _mm_plain(a3, w, layer, col0, ncols, out_dtype, tm_pref=1024, tn_pref=1024):
    nblk, rb, k = a3.shape
    tm = _pick(rb, tm_pref, V7X_SUBLANES)
    tn = _pick(math.gcd(ncols, col0) if col0 else ncols, tn_pref, V7X_LANES)
    cb = col0 // tn
    osz = jnp.dtype(out_dtype).itemsize
    vmem = 2 * (tm * k * 2 + k * tn * 2 + tm * tn * osz) + tm * tn * 4
    return pl.pallas_call(
        _mm_plain_kernel,
        grid=(nblk, rb // tm, ncols // tn),
        in_specs=[
            pl.BlockSpec((None, tm, k), lambda b, i, j: (b, i, 0)),
            _w_spec(k, tn, layer, cb),
        ],
        out_specs=pl.BlockSpec((None, tm, tn), lambda b, i, j: (b, i, j)),
        out_shape=jax.ShapeDtypeStruct((nblk, rb, ncols), out_dtype),
        compiler_params=_params(("arbitrary", "arbitrary", "arbitrary"), vmem),
        name="proj",
    )(a3, w)


def _mm_cast_kernel(a_ref, w_ref, o_ref, wbf_ref, wsc_ref):
    @pl.when((pl.program_id(1) == 0) & (pl.program_id(2) == 0))
    def _():
        wb = w_ref[...].astype(BF16)
        wsc_ref[...] = wb
        wbf_ref[...] = wb

    o_ref[...] = jnp.dot(a_ref[...], wsc_ref[...], preferred_element_type=F32).astype(o_ref.dtype)


def _mm_cast(a3, w, layer, col0, ncols, out_dtype):
    nblk, rb, k = a3.shape
    tm = _pick(rb, 1024, V7X_SUBLANES)
    tn = _pick(math.gcd(ncols, col0) if col0 else ncols, 512, V7X_LANES)
    cb = col0 // tn
    osz = jnp.dtype(out_dtype).itemsize
    vmem = 2 * (tm * k * 2 + k * tn * 4 + tm * tn * osz + k * tn * 2) + k * tn * 2 + tm * tn * 4
    return pl.pallas_call(
        _mm_cast_kernel,
        grid=(ncols // tn, nblk, rb // tm),
        in_specs=[
            pl.BlockSpec((None, tm, k), lambda j, b, i: (b, i, 0)),
            pl.BlockSpec((None, k, tn), lambda j, b, i: (layer, 0, j + cb)),
        ],
        out_specs=[
            pl.BlockSpec((None, tm, tn), lambda j, b, i: (b, i, j)),
            pl.BlockSpec((k, tn), lambda j, b, i: (0, j)),
        ],
        out_shape=[
            jax.ShapeDtypeStruct((nblk, rb, ncols), out_dtype),
            jax.ShapeDtypeStruct((k, ncols), BF16),
        ],
        scratch_shapes=[pltpu.VMEM((k, tn), BF16)],
        compiler_params=_params(("arbitrary", "arbitrary", "arbitrary"), vmem),
        name="proj_cast",
    )(a3, w)


def _mm_glu_kernel(y_ref, w_ref, yt_ref, o_ref, ybf_ref):
    @pl.when(pl.program_id(2) == 0)
    def _():
        ybf_ref[...] = y_ref[...].astype(BF16)

    acc = jnp.dot(ybf_ref[...], w_ref[...], preferred_element_type=F32)
    o_ref[...] = (yt_ref[...] * jax.nn.sigmoid(acc)).astype(o_ref.dtype)


def _mm_glu(y3, w, layer):
    nblk, rb, k = y3.shape
    n = w.shape[2]
    tm = _pick(rb, 1024, V7X_SUBLANES)
    tn = _pick(n, 512, V7X_LANES)
    vmem = 2 * (tm * k * 4 + k * tn * 2 + tm * tn * 4 + tm * tn * 2) + tm * k * 2 + tm * tn * 4
    return pl.pallas_call(
        _mm_glu_kernel,
        grid=(nblk, rb // tm, n // tn),
        in_specs=[
            pl.BlockSpec((None, tm, k), lambda b, i, j: (b, i, 0)),
            _w_spec(k, tn, layer),
            pl.BlockSpec((None, tm, tn), lambda b, i, j: (b, i, j)),
        ],
        out_specs=pl.BlockSpec((None, tm, tn), lambda b, i, j: (b, i, j)),
        out_shape=jax.ShapeDtypeStruct((nblk, rb, n), BF16),
        scratch_shapes=[pltpu.VMEM((tm, k), BF16)],
        compiler_params=_params(("arbitrary", "arbitrary", "arbitrary"), vmem),
        name="ssm_glu",
    )(y3, w, y3)


def _mm_merge_kernel(a_ref, s_ref, wa_ref, ws_ref, ga_ref, gs_ref, o_ref):
    pa = jnp.dot(a_ref[...], wa_ref[...], preferred_element_type=F32)
    ps = jnp.dot(s_ref[...], ws_ref[...], preferred_element_type=F32)
    o = jax.nn.sigmoid(ga_ref[...]) * pa + jax.nn.sigmoid(gs_ref[...]) * ps
    o_ref[...] = o.astype(o_ref.dtype)


def _mm_merge(attn3, ssm3, w_attn, w_ssm, layer, gates3, d):
    nblk, rb, ka = attn3.shape
    ks = ssm3.shape[2]
    tm = _pick(rb, 1024, V7X_SUBLANES)
    tn = _pick(d, 512, V7X_LANES)
    gs_off = d // tn
    vmem = 2 * (tm * (ka + ks) * 2 + (ka + ks) * tn * 2 + 2 * tm * tn * 4 + tm * tn * 2) + 3 * tm * tn * 4
    return pl.pallas_call(
        _mm_merge_kernel,
        grid=(nblk, rb // tm, d // tn),
        in_specs=[
            pl.BlockSpec((None, tm, ka), lambda b, i, j: (b, i, 0)),
            pl.BlockSpec((None, tm, ks), lambda b, i, j: (b, i, 0)),
            _w_spec(ka, tn, layer),
            _w_spec(ks, tn, layer),
            pl.BlockSpec((None, tm, tn), lambda b, i, j: (b, i, j)),
            pl.BlockSpec((None, tm, tn), lambda b, i, j: (b, i, j + gs_off)),
        ],
        out_specs=pl.BlockSpec((None, tm, tn), lambda b, i, j: (b, i, j)),
        out_shape=jax.ShapeDtypeStruct((nblk, rb, d), BF16),
        compiler_params=_params(("arbitrary", "arbitrary", "arbitrary"), vmem),
        name="mixer_merge",
    )(attn3, ssm3, w_attn, w_ssm, gates3, gates3)


def _mm_resid_kernel(a_ref, w_ref, x_ref, g_ref, o_ref):
    acc = jnp.dot(a_ref[...], w_ref[...], preferred_element_type=F32)
    o_ref[...] = x_ref[...] + g_ref[...] * acc


def _mm_resid(a3, w, layer, x3, gate, tm_pref, tn_pref, lhs_buffers=2):
    nblk, rb, k = a3.shape
    n = w.shape[2]
    tm = _pick(rb, tm_pref, V7X_SUBLANES)
    tn = _pick(n, tn_pref, V7X_LANES)
    vmem = lhs_buffers * tm * k * 2 + 2 * (k * tn * 2 + 2 * tm * tn * 4) + tm * tn * 4
    return pl.pallas_call(
        _mm_resid_kernel,
        grid=(nblk, rb // tm, n // tn),
        in_specs=[
            pl.BlockSpec((None, tm, k), lambda b, i, j: (b, i, 0), pipeline_mode=pl.Buffered(lhs_buffers)),
            _w_spec(k, tn, layer),
            pl.BlockSpec((None, tm, tn), lambda b, i, j: (b, i, j)),
            _row_param_spec(gate, tm, tn, True),
        ],
        out_specs=pl.BlockSpec((None, tm, tn), lambda b, i, j: (b, i, j)),
        out_shape=jax.ShapeDtypeStruct((nblk, rb, n), F32),
        compiler_params=_params(("arbitrary", "arbitrary", "arbitrary"), vmem),
        name="proj_residual",
    )(a3, w, x3, gate)


def _conv3(up, r1, r2, cw_ref, cb_ref):
    cw = cw_ref[...]
    return cb_ref[...] + (cw[0:1] * r2 + cw[1:2] * r1 + cw[2:3] * up)


def _ffn_up_fresh_kernel(h_ref, wa_ref, wg_ref, cwa_ref, cwg_ref, cba_ref, cbg_ref,
                         act_ref, sa_ref, sg_ref, wbfa_ref, wbfg_ref, wsc_ref, tail_ref, *, tm):
    tail_rows = V7X_SUBLANES

    @pl.when((pl.program_id(1) == 0) & (pl.program_id(2) == 0))
    def _():
        for p, (w_ref, wbf_ref) in enumerate(((wa_ref, wbfa_ref), (wg_ref, wbfg_ref))):
            wb = w_ref[...].astype(BF16)
            wsc_ref[p] = wb
            wbf_ref[...] = wb

    @pl.when(pl.program_id(2) == 0)
    def _():
        tail_ref[...] = jnp.zeros_like(tail_ref)

    lhs = h_ref[...]
    parts = []
    for p, (cw_ref, cb_ref, s_ref) in enumerate(((cwa_ref, cba_ref, sa_ref), (cwg_ref, cbg_ref, sg_ref))):
        up = jnp.dot(lhs, wsc_ref[p], preferred_element_type=F32)
        ext = jnp.concatenate([tail_ref[p], up], axis=0)
        new_tail = ext[tm:, :]
        tail_ref[p] = new_tail
        s_ref[...] = new_tail
        r1 = pltpu.roll(ext, 1, 0)[tail_rows:]
        r2 = pltpu.roll(ext, 2, 0)[tail_rows:]
        parts.append(_conv3(up, r1, r2, cw_ref, cb_ref))
    act_ref[...] = (jax.nn.silu(parts[1]) * parts[0]).astype(act_ref.dtype)


def _ffn_up_carried_kernel(h_ref, wa_ref, wg_ref, cwa_ref, cwg_ref, cba_ref, cbg_ref,
                           p1a_ref, p2a_ref, p1g_ref, p2g_ref, act_ref, sa_ref, sg_ref, *, seq):
    lhs = h_ref[...]
    parts = []
    for w_ref, cw_ref, cb_ref, p1_ref, p2_ref, s_ref in (
            (wa_ref, cwa_ref, cba_ref, p1a_ref, p2a_ref, sa_ref),
            (wg_ref, cwg_ref, cbg_ref, p1g_ref, p2g_ref, sg_ref)):
        up = jnp.dot(lhs, w_ref[...], preferred_element_type=F32)
        s_ref[...] = up
        tloc = lax.broadcasted_iota(jnp.int32, up.shape, 0) % seq
        r1 = jnp.where(tloc < 1, p1_ref[...], pltpu.roll(up, 1, 0))
        r2 = jnp.where(tloc < 2, p2_ref[...], pltpu.roll(up, 2, 0))
        parts.append(_conv3(up, r1, r2, cw_ref, cb_ref))
    act_ref[...] = (jax.nn.silu(parts[1]) * parts[0]).astype(act_ref.dtype)


def _ffn_up_fresh(h3, w_up, w_conv, b_conv, layer, d_ff):
    nblk, rb, k = h3.shape
    tn = _pick(d_ff, 512, V7X_LANES)
    goff = d_ff // tn
    tm = _pick(rb, 1024, V7X_SUBLANES)
    cb = b_conv.reshape(b_conv.shape[0], 1, 2 * d_ff)
    wspec = lambda rows, off: pl.BlockSpec((None, rows, tn), lambda j, b, i: (layer, 0, j + off))
    tail_spec = pl.BlockSpec((None, None, V7X_SUBLANES, tn), lambda j, b, i: (b, i, 0, j))
    wbf_spec = pl.BlockSpec((k, tn), lambda j, b, i: (0, j))
    vmem = (2 * (tm * k * 2 + 2 * k * tn * 4 + 2 * k * tn * 2 + tm * tn * 2) + 2 * k * tn * 2
            + 10 * (tm + 8) * tn * 4)
    return pl.pallas_call(
        functools.partial(_ffn_up_fresh_kernel, tm=tm),
        grid=(goff, nblk, rb // tm),
        in_specs=[
            pl.BlockSpec((None, tm, k), lambda j, b, i: (b, i, 0)),
            wspec(k, 0), wspec(k, goff),
            wspec(w_conv.shape[1], 0), wspec(w_conv.shape[1], goff),
            wspec(1, 0), wspec(1, goff),
        ],
        out_specs=[
            pl.BlockSpec((None, tm, tn), lambda j, b, i: (b, i, j)),
            tail_spec, tail_spec, wbf_spec, wbf_spec,
        ],
        out_shape=[
            jax.ShapeDtypeStruct((nblk, rb, d_ff), BF16),
            jax.ShapeDtypeStruct((nblk, rb // tm, V7X_SUBLANES, d_ff), F32),
            jax.ShapeDtypeStruct((nblk, rb // tm, V7X_SUBLANES, d_ff), F32),
            jax.ShapeDtypeStruct((k, d_ff), BF16),
            jax.ShapeDtypeStruct((k, d_ff), BF16),
        ],
        scratch_shapes=[pltpu.VMEM((2, k, tn), BF16), pltpu.VMEM((2, V7X_SUBLANES, tn), F32)],
        compiler_params=_params(("arbitrary", "arbitrary", "arbitrary"), vmem),
        name="convffn_up",
    )(h3, w_up, w_up, w_conv, w_conv, cb, cb)


def _ffn_up_carried(h3, w_a, w_g, w_conv, b_conv, layer, d_ff, prev, seq):
    nblk, rb, k = h3.shape
    tn = _pick(d_ff, 512, V7X_LANES)
    goff = d_ff // tn
    cb = b_conv.reshape(b_conv.shape[0], 1, 2 * d_ff)
    p1, p2 = prev
    cspec = lambda rows, off: pl.BlockSpec((None, rows, tn), lambda b, j: (layer, 0, j + off))
    wspec = pl.BlockSpec((k, tn), lambda b, j: (0, j))
    row_spec = lambda off: pl.BlockSpec((None, rb, tn), lambda b, j: (b, 0, j + off))
    vmem = 2 * (rb * k * 2 + 2 * k * tn * 2 + 8 * rb * tn * 4) + 10 * rb * tn * 4
    return pl.pallas_call(
        functools.partial(_ffn_up_carried_kernel, seq=seq),
        grid=(nblk, goff),
        in_specs=[
            pl.BlockSpec((None, rb, k), lambda b, j: (b, 0, 0)),
            wspec, wspec,
            cspec(w_conv.shape[1], 0), cspec(w_conv.shape[1], goff),
            cspec(1, 0), cspec(1, goff),
            row_spec(0), row_spec(0), row_spec(goff), row_spec(goff),
        ],
        out_specs=[row_spec(0), row_spec(0), row_spec(0)],
        out_shape=[
            jax.ShapeDtypeStruct((nblk, rb, d_ff), BF16),
            jax.ShapeDtypeStruct((nblk, rb, d_ff), F32),
            jax.ShapeDtypeStruct((nblk, rb, d_ff), F32),
        ],
        compiler_params=_params(("arbitrary", "arbitrary"), vmem),
        name="convffn_up_carried",
    )(h3, w_a, w_g, w_conv, w_conv, cb, cb, p1, p2, p1, p2)


def _beats(other, gate, other_is_lower):
    return jnp.where(other > gate, 1.0, 0.0) + jnp.where(other == gate, 1.0, 0.0) * other_is_lower


def _attn_fresh_body(cc, q_ref, o_ref, kaug_ref, vbf_ref, kmean_ref, sd_ref, *, nblk):
    blk = MOBA_BLOCK
    scale = HEAD_DIM ** -0.5
    w = (cc + 1) * blk
    q2 = q_ref[...]
    qs = jnp.concatenate([q2[:, :HEAD_DIM], q2[:, HEAD_DIM:]], axis=0)
    if cc > 0:
        gate_t = lax.dot_general(kmean_ref[...], qs, (((1,), (1,)), ((), ())),
                                 precision=lax.Precision.HIGHEST, preferred_element_type=F32)
        riota = lax.broadcasted_iota(jnp.int32, gate_t.shape, 0)
        cnt = jnp.zeros_like(gate_t)
        for m in range(cc):
            cnt = cnt + _beats(gate_t[m:m + 1, :], gate_t, jnp.where(riota > m, 1.0, 0.0))
        keep = jnp.where(cnt < MOBA_TOPK, 1.0, 0.0) + jnp.where(riota >= cc, 1.0, 0.0)
        bias_t = jnp.where(keep > 0.5, 0.0, MASK_BIAS)
        bias_t = jnp.concatenate(
            [bias_t, jnp.zeros((HEAD_DIM - nblk, 2 * blk), F32)], axis=0)
        bias = bias_t.T.astype(BF16)
    else:
        bias = jnp.zeros((2 * blk, HEAD_DIM), BF16)
    q_aug = jnp.concatenate([qs.astype(BF16), bias], axis=1)
    s = lax.dot_general(q_aug, kaug_ref[0:w, :], (((1,), (1,)), ((), ())),
                        preferred_element_type=F32)
    row = lax.broadcasted_iota(jnp.int32, (blk, blk), 0)
    col = lax.broadcasted_iota(jnp.int32, (blk, blk), 1)
    causal = row >= col
    for hh in range(2):
        sh = s[hh * blk:(hh + 1) * blk, :] * scale - sd_ref[hh, :, 0:w]
        diag = jnp.where(causal, sh[:, cc * blk:], NEG_INF)
        sh = jnp.concatenate([sh[:, :cc * blk], diag], axis=1) if cc > 0 else diag
        m = jnp.max(sh, axis=-1, keepdims=True)
        p = jnp.exp(sh - m)
        l = jnp.sum(p, axis=-1, keepdims=True)
        o = jnp.dot(p.astype(BF16), vbf_ref[0:w, :], preferred_element_type=F32) / l
        o_ref[:, hh * HEAD_DIM:(hh + 1) * HEAD_DIM] = o.astype(o_ref.dtype)


def _attn_fresh_kernel(slope_ref, q_ref, k_ref, v_ref, o_ref,
                       kaug_ref, vbf_ref, kmean_ref, sd_ref, *, nblk):
    g = pl.program_id(1)
    c = pl.program_id(2)
    blk = MOBA_BLOCK
    t = nblk * blk

    @pl.when(c == 0)
    def _():
        k = k_ref[...]
        kaug_ref[:, 0:HEAD_DIM] = k.astype(BF16)
        key_blk = lax.broadcasted_iota(jnp.int32, (t, HEAD_DIM), 0) // blk
        lane = lax.broadcasted_iota(jnp.int32, (t, HEAD_DIM), 1)
        kaug_ref[:, HEAD_DIM:] = jnp.where(key_blk == lane, 1.0, 0.0).astype(BF16)
        vbf_ref[...] = v_ref[...].astype(BF16)
        for n in range(nblk):
            kmean_ref[n:n + 1, :] = jnp.mean(k[n * blk:(n + 1) * blk, :], axis=0, keepdims=True)
        d0 = (lax.broadcasted_iota(jnp.int32, (blk, t), 0)
              - lax.broadcasted_iota(jnp.int32, (blk, t), 1)).astype(F32)
        for hh in range(2):
            sd_ref[hh] = slope_ref[2 * g + hh] * d0

    for cc in range(nblk):
        pl.when(c == cc)(functools.partial(
            _attn_fresh_body, cc, q_ref, o_ref, kaug_ref, vbf_ref, kmean_ref, sd_ref, nblk=nblk))


def _attn_fresh(q3, k3, v3, slopes):
    b, t, q_w = q3.shape
    kv_heads = k3.shape[2] // HEAD_DIM
    assert q_w == 2 * kv_heads * HEAD_DIM and t % MOBA_BLOCK == 0
    nblk = t // MOBA_BLOCK
    assert nblk <= HEAD_DIM
    blk = MOBA_BLOCK
    vmem = (2 * (blk * 2 * HEAD_DIM * 4 + 2 * t * HEAD_DIM * 4 + blk * 2 * HEAD_DIM * 2)
            + 3 * t * HEAD_DIM * 2 + 2 * blk * t * 4 + 8 * 2 * blk * t * 4)
    return pl.pallas_call(
        functools.partial(_attn_fresh_kernel, nblk=nblk),
        grid=(b, kv_heads, nblk),
        in_specs=[
            pl.BlockSpec(memory_space=pltpu.SMEM),
            pl.BlockSpec((None, blk, 2 * HEAD_DIM), lambda bb, g, c: (bb, c, g)),
            pl.BlockSpec((None, t, HEAD_DIM), lambda bb, g, c: (bb, 0, g)),
            pl.BlockSpec((None, t, HEAD_DIM), lambda bb, g, c: (bb, 0, g)),
        ],
        out_specs=pl.BlockSpec((None, blk, 2 * HEAD_DIM), lambda bb, g, c: (bb, c, g)),
        out_shape=jax.ShapeDtypeStruct((b, t, q_w), BF16),
        scratch_shapes=[
            pltpu.VMEM((t, 2 * HEAD_DIM), BF16),
            pltpu.VMEM((t, HEAD_DIM), BF16),
            pltpu.VMEM((nblk, HEAD_DIM), F32),
            pltpu.VMEM((2, blk, t), F32),
        ],
        compiler_params=_params(("arbitrary", "arbitrary", "arbitrary"), vmem),
        name="moba_fresh",
    )(slopes, q3, k3, v3)


def _page_heads_on_lanes(p_ref, page, kvh):
    return jnp.concatenate([p_ref[pl.ds(g, page, stride=kvh), :] for g in range(kvh)], axis=-1)


def _page_specs(layer, npg, page_rows, hd):
    def spec(j):
        return pl.BlockSpec((None, None, page_rows, hd),
                            lambda i, n, pt: (layer, pt[i, n * npg + j], 0, 0))
    return [spec(j) for j in range(npg)]


def _paged_scores_kernel(pt_ref, wq_ref, *refs, page, kvh, npg):
    pages = refs[:npg]
    sc_ref, km_ref = refs[npg:]
    wq = wq_ref[...]
    hd = pages[0].shape[1]
    sums = []
    for j, p_ref in enumerate(pages):
        kcat = _page_heads_on_lanes(p_ref, page, kvh)
        sc_ref[j * page:(j + 1) * page, :] = jnp.dot(kcat.astype(BF16), wq, preferred_element_type=F32)
        sums.append(jnp.sum(p_ref[...].reshape(page, kvh, hd), axis=0))
    for b in range(npg // 2):
        km_ref[b] = (sums[2 * b] + sums[2 * b + 1]) * (1.0 / (2 * page))


def _paged_scores(page_table, wq_bf, cache, layer, npg):
    s, n_pages = page_table.shape
    depth, n_pool, page, kvh, hd = cache.shape
    assert 2 * page == MOBA_BLOCK and n_pages % npg == 0 and npg % 2 == 0
    kv_w = kvh * hd
    cache_rows = cache.reshape(depth, n_pool, page * kvh, hd)
    nblk = n_pages // 2
    past = n_pages * page
    vmem = 2 * (kv_w * V7X_LANES * 2 + npg * page * kv_w * 4 + npg * page * V7X_LANES * 4) + 6 * page * kv_w * 4
    grid_spec = pltpu.PrefetchScalarGridSpec(
        num_scalar_prefetch=1,
        grid=(s, n_pages // npg),
        in_specs=[pl.BlockSpec((None, kv_w, V7X_LANES), lambda i, n, pt: (i, 0, 0))]
        + _page_specs(layer, npg, page * kvh, hd),
        out_specs=[
            pl.BlockSpec((None, npg * page, V7X_LANES), lambda i, n, pt: (i, n, 0)),
            pl.BlockSpec((None, npg // 2, kvh, hd), lambda i, n, pt: (i, n, 0, 0)),
        ],
    )
    return pl.pallas_call(
        functools.partial(_paged_scores_kernel, page=page, kvh=kvh, npg=npg),
        grid_spec=grid_spec,
        out_shape=[
            jax.ShapeDtypeStruct((s, past, V7X_LANES), F32),
            jax.ShapeDtypeStruct((s, nblk, kvh, hd), F32),
        ],
        compiler_params=_params(("arbitrary", "arbitrary"), vmem),
        name="moba_paged_scores",
    )(page_table, wq_bf, *([cache_rows] * npg))


def _paged_attend_kernel(pt_ref, wq_ref, wqbf_ref, sc_ref, km_ref, knew_ref, vnew_ref,
                         slope_ref, tcol_ref, *refs, page, kvh, npg, nblk, tnew, past):
    pages = refs[:npg]
    o_ref, sel_ref, prob_ref, linv_ref, sq_ref, vpad_ref, acc_ref = refs[npg:]
    n = pl.program_id(1)
    blk = MOBA_BLOCK
    scale = HEAD_DIM ** -0.5
    lanes = V7X_LANES
    slope = slope_ref[...]
    tcol = tcol_ref[...]
    koff = lax.broadcasted_iota(jnp.int32, (blk, lanes), 0).astype(F32)

    def block_scores(i):
        r0 = pl.multiple_of(i * blk, blk)
        raw = sc_ref[pl.ds(r0, blk), :]
        dist = (tcol + lax.convert_element_type(past - i * blk, F32)) - koff
        s = raw * scale - slope * dist
        return jnp.where(sel_ref[pl.ds(i, 1), :] > 0.5, s, NEG_INF)

    @pl.when(n == 0)
    def _():
        gate = jnp.zeros((nblk, lanes), F32)
        for g in range(kvh):
            gate = gate + jnp.dot(km_ref[pl.ds(g, nblk, stride=kvh), :],
                                  wq_ref[g * HEAD_DIM:(g + 1) * HEAD_DIM, :],
                                  precision=lax.Precision.HIGHEST, preferred_element_type=F32)
        riota = lax.broadcasted_iota(jnp.int32, (nblk, lanes), 0)
        cnt = jnp.zeros((nblk, lanes), F32)
        for m in range(nblk):
            cnt = cnt + _beats(gate[m:m + 1, :], gate, jnp.where(riota > m, 1.0, 0.0))
        sel_ref[...] = jnp.where(cnt < MOBA_TOPK, 1.0, 0.0)

        s_cur = jnp.dot(knew_ref[...].astype(BF16), wqbf_ref[...], preferred_element_type=F32)
        off = lax.broadcasted_iota(jnp.int32, (tnew, lanes), 0).astype(F32)
        s_cur = s_cur * scale - slope * (tcol - off)
        s_cur = jnp.where(off <= tcol, s_cur, NEG_INF)
        m0 = jnp.max(s_cur, axis=0, keepdims=True)

        def max_body(i, m):
            return jnp.maximum(m, jnp.max(block_scores(i), axis=0, keepdims=True))

        mx = lax.fori_loop(0, nblk, max_body, m0)
        p_cur = jnp.exp(s_cur - mx)

        def sum_body(i, l):
            p = jnp.exp(block_scores(i) - mx)
            prob_ref[pl.ds(pl.multiple_of(i * blk, blk), blk), :] = p
            return l + jnp.sum(p, axis=0, keepdims=True)

        l = lax.fori_loop(0, nblk, sum_body, jnp.sum(p_cur, axis=0, keepdims=True))
        linv = 1.0 / l
        linv_ref[...] = linv
        sq_ref[...] = jnp.zeros_like(sq_ref)
        sq_ref[0:tnew, :] = p_cur * linv
        vpad_ref[...] = jnp.zeros_like(vpad_ref)
        vpad_ref[0:tnew, :] = vnew_ref[...].astype(BF16)
        acc_ref[...] = jnp.dot(sq_ref[...].T.astype(BF16), vpad_ref[...], preferred_element_type=F32)

    linv = linv_ref[...]
    acc = acc_ref[...]
    for j, p_ref in enumerate(pages):
        r0 = pl.multiple_of((n * npg + j) * page, page)
        pt_j = (prob_ref[pl.ds(r0, page), :] * linv).T.astype(BF16)
        vcat = _page_heads_on_lanes(p_ref, page, kvh).astype(BF16)
        acc = acc + jnp.dot(pt_j, vcat, preferred_element_type=F32)
    acc_ref[...] = acc

    @pl.when(n == pl.num_programs(1) - 1)
    def _():
        rows = o_ref.shape[1]
        for g in range(o_ref.shape[0]):
            o_ref[g] = acc_ref[g * rows:(g + 1) * rows, g * HEAD_DIM:(g + 1) * HEAD_DIM].astype(o_ref.dtype)


def _paged_attend(page_table, wq, wq_bf, scores, kmean, k_new, v_new, slope_col, t_col, cache, layer, npg):
    s, n_pages = page_table.shape
    depth, n_pool, page, kvh, hd = cache.shape
    kv_w = kvh * hd
    cache_rows = cache.reshape(depth, n_pool, page * kvh, hd)
    nblk = n_pages // 2
    past = n_pages * page
    tnew = k_new.shape[1]
    rows = 2 * tnew
    kmean_rows = kmean.reshape(s, nblk * kvh, hd)
    vmem = (2 * (kv_w * V7X_LANES * 6 + past * V7X_LANES * 4 + nblk * kv_w * 4 + npg * page * kv_w * 4)
            + past * V7X_LANES * 4 + 3 * V7X_LANES * kv_w * 4 + 16 * MOBA_BLOCK * V7X_LANES * 4
            + 4 * page * kv_w * 4)
    grid_spec = pltpu.PrefetchScalarGridSpec(
        num_scalar_prefetch=1,
        grid=(s, n_pages // npg),
        in_specs=[
            pl.BlockSpec((None, kv_w, V7X_LANES), lambda i, n, pt: (i, 0, 0)),
            pl.BlockSpec((None, kv_w, V7X_LANES), lambda i, n, pt: (i, 0, 0)),
            pl.BlockSpec((None, past, V7X_LANES), lambda i, n, pt: (i, 0, 0)),
            pl.BlockSpec((None, nblk * kvh, hd), lambda i, n, pt: (i, 0, 0)),
            pl.BlockSpec((None, tnew, kv_w), lambda i, n, pt: (i, 0, 0)),
            pl.BlockSpec((None, tnew, kv_w), lambda i, n, pt: (i, 0, 0)),
            pl.BlockSpec((1, V7X_LANES), lambda i, n, pt: (0, 0)),
            pl.BlockSpec((1, V7X_LANES), lambda i, n, pt: (0, 0)),
        ] + _page_specs(layer, npg, page * kvh, hd),
        out_specs=pl.BlockSpec((None, kvh, rows, hd), lambda i, n, pt: (i, 0, 0, 0)),
        scratch_shapes=[
            pltpu.VMEM((nblk, V7X_LANES), F32),
            pltpu.VMEM((past, V7X_LANES), F32),
            pltpu.VMEM((1, V7X_LANES), F32),
            pltpu.VMEM((V7X_LANES, V7X_LANES), F32),
            pltpu.VMEM((V7X_LANES, kv_w), BF16),
            pltpu.VMEM((V7X_LANES, kv_w), F32),
        ],
    )
    return pl.pallas_call(
        functools.partial(_paged_attend_kernel, page=page, kvh=kvh, npg=npg, nblk=nblk, tnew=tnew, past=past),
        grid_spec=grid_spec,
        out_shape=jax.ShapeDtypeStruct((s, kvh, rows, hd), BF16),
        compiler_params=_params(("arbitrary", "arbitrary"), vmem),
        name="moba_paged_attend",
    )(page_table, wq, wq_bf, scores, kmean_rows, k_new, v_new, slope_col, t_col, *([cache_rows] * npg))


def _ssm_prep_kernel(lr_ref, li_ref, ldt_ref, brt_ref, bit_ref, are_ref, aim_ref, bbr_ref, bbi_ref):
    lr = lr_ref[...]
    li = li_ref[...]
    dt = jnp.exp(ldt_ref[...])
    mag = jnp.exp(lr * dt)
    a_re = mag * jnp.cos(li * dt)
    a_im = mag * jnp.sin(li * dt)
    den = lr * lr + li * li
    q_re = ((a_re - 1.0) * lr + a_im * li) / den
    q_im = (a_im * lr - (a_re - 1.0) * li) / den
    are_ref[...] = a_re
    aim_ref[...] = a_im
    br = brt_ref[...]
    bi = bit_ref[...]
    bbr_ref[...] = q_re * br - q_im * bi
    bbi_ref[...] = q_re * bi + q_im * br


def _ssm_prep(lam_re, lam_im, log_dt, b_re, b_im):
    depth, g, p = lam_re.shape
    i = b_re.shape[3]
    brt = jnp.swapaxes(b_re, 2, 3)
    bit = jnp.swapaxes(b_im, 2, 3)
    vec = pl.BlockSpec((None, g, 1, p), lambda l: (l, 0, 0, 0))
    mat = pl.BlockSpec((None, g, i, p), lambda l: (l, 0, 0, 0))
    return pl.pallas_call(
        _ssm_prep_kernel,
        grid=(depth,),
        in_specs=[vec, vec, pl.BlockSpec((None, g, 1, 1), lambda l: (l, 0, 0, 0)), mat, mat],
        out_specs=[vec, vec, mat, mat],
        out_shape=[
            jax.ShapeDtypeStruct((depth, g, 1, p), F32),
            jax.ShapeDtypeStruct((depth, g, 1, p), F32),
            jax.ShapeDtypeStruct((depth, g, i, p), F32),
            jax.ShapeDtypeStruct((depth, g, i, p), F32),
        ],
        compiler_params=_params(("arbitrary",), 32 << 20),
        name="s5_discretise",
    )(lam_re.reshape(depth, g, 1, p), lam_im.reshape(depth, g, 1, p),
      log_dt.reshape(depth, g, 1, 1), brt, bit)


def _ssm_scan_kernel(u_ref, ire_ref, iim_ref, are_ref, aim_ref, bre_ref, bim_ref,
                     cre_ref, cim_ref, d_ref, *rest, with_y, n_chunks, tau_b):
    if with_y:
        y_ref, ere_ref, eim_ref, hre_ref, him_ref, bure_ref, buim_ref = rest
    else:
        ere_ref, eim_ref, hre_ref, him_ref, bure_ref, buim_ref = rest
    k = pl.program_id(1)

    @pl.when(k == 0)
    def _():
        hre_ref[...] = ire_ref[...]
        him_ref[...] = iim_ref[...]

    def chunk(c, carry):
        cu = pl.multiple_of(c * CHUNK_IN, CHUNK_IN)
        cs = pl.multiple_of(c * CHUNK_ST, CHUNK_ST)
        u_c = u_ref[:, pl.ds(cu, CHUNK_IN)]
        ub = u_c.astype(BF16)
        bure_ref[...] = jnp.dot(ub, bre_ref[c], preferred_element_type=F32)
        buim_ref[...] = jnp.dot(ub, bim_ref[c], preferred_element_type=F32)
        ar = jnp.broadcast_to(are_ref[:, pl.ds(cs, CHUNK_ST)], (N_SEG, CHUNK_ST))
        ai = jnp.broadcast_to(aim_ref[:, pl.ds(cs, CHUNK_ST)], (N_SEG, CHUNK_ST))
        hr = hre_ref[:, pl.ds(cs, CHUNK_ST)]
        hi = him_ref[:, pl.ds(cs, CHUNK_ST)]
        for t in range(tau_b):
            rs = slice(t * N_SEG, (t + 1) * N_SEG)
            nr = ar * hr - ai * hi + bure_ref[rs, :]
            ni = ar * hi + ai * hr + buim_ref[rs, :]
            hr, hi = nr, ni
            if with_y:
                bure_ref[rs, :] = hr
                buim_ref[rs, :] = hi
        hre_ref[:, pl.ds(cs, CHUNK_ST)] = hr
        him_ref[:, pl.ds(cs, CHUNK_ST)] = hi
        if with_y:
            ych = (jnp.dot(bure_ref[...].astype(BF16), cre_ref[c], preferred_element_type=F32)
                   - jnp.dot(buim_ref[...].astype(BF16), cim_ref[c], preferred_element_type=F32))
            y = ych + d_ref[:, pl.ds(cu, CHUNK_IN)] * u_c
            y_ref[:, pl.ds(cu, CHUNK_IN)] = jax.nn.gelu(y)
        return carry

    lax.fori_loop(0, n_chunks, chunk, 0)

    @pl.when(k == pl.num_programs(1) - 1)
    def _():
        ere_ref[...] = hre_ref[...]
        eim_ref[...] = him_ref[...]


def _ssm_scan(u3, init_re, init_im, a_re, a_im, bre, bim, cre, cim, d_skip, with_y):
    nb, lr, s_w = u3.shape
    ns = a_re.shape[1]
    n_chunks = bre.shape[0]
    steps = lr // N_SEG
    tau_b = _pick(steps, 32, 1)
    rb = tau_b * N_SEG
    st_spec = pl.BlockSpec((None, N_SEG, ns), lambda b, k: (b, 0, 0))
    full = lambda shape: pl.BlockSpec(shape, lambda b, k: (0,) * len(shape))
    u_spec = pl.BlockSpec((None, rb, s_w), lambda b, k: (b, k, 0))
    out_specs = [st_spec, st_spec]
    out_shape = [jax.ShapeDtypeStruct((nb, N_SEG, ns), F32)] * 2
    if with_y:
        out_specs = [u_spec] + out_specs
        out_shape = [jax.ShapeDtypeStruct((nb, lr, s_w), F32)] + out_shape
    vmem = (4 * rb * s_w * 4 + 8 * N_SEG * ns * 4 + 4 * ns * 4
            + 2 * 4 * n_chunks * CHUNK_IN * CHUNK_ST * 2 + 8 * rb * CHUNK_ST * 4)
    return pl.pallas_call(
        functools.partial(_ssm_scan_kernel, with_y=with_y, n_chunks=n_chunks, tau_b=tau_b),
        grid=(nb, steps // tau_b),
        in_specs=[u_spec, st_spec, st_spec, full((1, ns)), full((1, ns)),
                  full(bre.shape), full(bim.shape), full(cre.shape), full(cim.shape),
                  full((1, s_w))],
        out_specs=out_specs,
        out_shape=out_shape,
        scratch_shapes=[
            pltpu.VMEM((N_SEG, ns), F32),
            pltpu.VMEM((N_SEG, ns), F32),
            pltpu.VMEM((rb, CHUNK_ST), F32),
            pltpu.VMEM((rb, CHUNK_ST), F32),
        ],
        compiler_params=_params(("arbitrary", "arbitrary"), vmem),
        name="s5_scan" if with_y else "s5_segment_ends",
    )(u3, init_re, init_im, a_re, a_im, bre, bim, cre, cim, d_skip)


def _ssm_carry_kernel(ere_ref, eim_ref, are_ref, aim_ref, ire_ref, iim_ref, fre_ref, fim_ref, *, seg_len):
    br = are_ref[...]
    bi = aim_ref[...]
    pr = jnp.ones_like(br)
    pi = jnp.zeros_like(br)
    e = seg_len
    while e:
        if e & 1:
            pr, pi = pr * br - pi * bi, pr * bi + pi * br
        br, bi = br * br - bi * bi, 2.0 * br * bi
        e >>= 1
    hr = jnp.zeros_like(pr)
    hi = jnp.zeros_like(pr)
    for j in range(N_SEG):
        ire_ref[j:j + 1, :] = hr
        iim_ref[j:j + 1, :] = hi
        er = ere_ref[j:j + 1, :]
        ei = eim_ref[j:j + 1, :]
        hr, hi = pr * hr - pi * hi + er, pr * hi + pi * hr + ei
    fre_ref[...] = hr
    fim_ref[...] = hi


def _ssm_carry(end_re, end_im, a_re, a_im, seg_len):
    nb, _, ns = end_re.shape
    st = pl.BlockSpec((None, N_SEG, ns), lambda b: (b, 0, 0))
    vec = pl.BlockSpec((1, ns), lambda b: (0, 0))
    fin = pl.BlockSpec((None, 1, ns), lambda b: (b, 0, 0))
    return pl.pallas_call(
        functools.partial(_ssm_carry_kernel, seg_len=seg_len),
        grid=(nb,),
        in_specs=[st, st, vec, vec],
        out_specs=[st, st, fin, fin],
        out_shape=[jax.ShapeDtypeStruct((nb, N_SEG, ns), F32)] * 2
        + [jax.ShapeDtypeStruct((nb, 1, ns), F32)] * 2,
        compiler_params=_params(("arbitrary",), 32 << 20),
        name="s5_segment_carry",
    )(end_re, end_im, a_re, a_im)


def _block_diag_in(bbt):
    g, i, p = bbt.shape
    nc = g // GROUPS_PER_CHUNK
    eye = jnp.eye(GROUPS_PER_CHUNK, dtype=bbt.dtype)
    x = bbt.reshape(nc, GROUPS_PER_CHUNK, i, p)
    out = x[:, :, :, None, :] * eye[None, :, None, :, None]
    return out.reshape(nc, GROUPS_PER_CHUNK * i, GROUPS_PER_CHUNK * p).astype(BF16)


def _block_diag_out(c):
    g, i, p = c.shape
    nc = g // GROUPS_PER_CHUNK
    eye = jnp.eye(GROUPS_PER_CHUNK, dtype=c.dtype)
    x = jnp.swapaxes(c.reshape(nc, GROUPS_PER_CHUNK, i, p), 2, 3)
    out = x[:, :, :, None, :] * eye[None, :, None, :, None]
    return out.reshape(nc, GROUPS_PER_CHUNK * p, GROUPS_PER_CHUNK * i).astype(BF16)


def kernel(x_prompt, x_sample, c_prompt, c_sample, cache_k, cache_v, state_ssm_re, state_ssm_im,
           state_conv, page_table, w_ada, b_ada, norm_attn, w_in, w_attn_proj, lam_re, lam_im,
           log_dt, ssm_b_re, ssm_b_im, ssm_c_re, ssm_c_im, ssm_d, w_glu, w_ssm_proj, w_out,
           norm_ffn, w_up, w_conv, b_conv, w_down, norm_final):
    depth = w_ada.shape[0]
    nb, t, d = x_prompt.shape
    ns_seq, ts, _ = x_sample.shape
    q_w = w_attn_proj.shape[1]
    n_heads = q_w // HEAD_DIM
    kv_heads = cache_k.shape[3]
    kv_w = kv_heads * HEAD_DIM
    s_w = ssm_d.shape[1]
    n_groups = lam_re.shape[1]
    n_state = n_groups * S5_STATE
    d_ff = w_down.shape[1]
    n_pages = page_table.shape[1]
    page = cache_k.shape[2]
    past = n_pages * page
    assert ns_seq == N_SEG and t % (N_SEG * N_SEG) == 0 and n_groups % GROUPS_PER_CHUNK == 0
    n_col = kv_heads * 2 * ts
    assert n_col <= V7X_LANES and n_heads == 2 * kv_heads
    assert ssm_b_re.shape[2:] == (S5_STATE, S5_GROUP)
    u_col = q_w + 2 * kv_w
    g_col = u_col + s_w

    n_c = nb + ns_seq
    c_rows = -(-n_c // V7X_SUBLANES) * V7X_SUBLANES
    c_all = jnp.concatenate([c_prompt, c_sample, jnp.zeros((c_rows - n_c, d), F32)], axis=0)
    mod = _adaln(c_all, w_ada, b_ada)

    a_re_all, a_im_all, bbr_all, bbi_all = _ssm_prep(lam_re, lam_im, log_dt, ssm_b_re, ssm_b_im)
    slopes = 2.0 ** (-8.0 * jnp.arange(1, n_heads + 1, dtype=F32) / n_heads)

    col = jnp.arange(V7X_LANES)
    col_head = jnp.minimum(2 * (col // (2 * ts)) + (col // ts) % 2, n_heads - 1)
    slope_col = slopes[col_head].reshape(1, V7X_LANES)
    t_col = (col % ts).astype(F32).reshape(1, V7X_LANES)

    xp = x_prompt
    xs = x_sample.reshape(1, ns_seq * ts, d)
    seg = t // N_SEG
    keep = w_conv.shape[1] - 1
    assert keep == 2 and past % MOBA_BLOCK == 0 and ts <= MOBA_BLOCK
    npg = _pick(n_pages, PAGES_PER_STEP, 2)
    outs ={k: [] for k in ("kp", "vp", "ks", "vs", "hpr", "hpi", "hsr", "hsi", "cp", "cs")}

    w_attn_bf = w_attn_proj.astype(BF16)
    w_glu_bf = w_glu.astype(BF16)
    w_ssm_bf = w_ssm_proj.astype(BF16)
    w_out_bf = w_out.astype(BF16)
    w_down_bf = w_down.astype(BF16)
    in_cols = ((0, q_w), (q_w, kv_w), (q_w + kv_w, kv_w), (u_col, s_w), (g_col, 2 * d))

    for l in range(depth):
        a_re = a_re_all[l].reshape(1, n_state)
        a_im = a_im_all[l].reshape(1, n_state)
        bre = _block_diag_in(bbr_all[l])
        bim = _block_diag_in(bbi_all[l])
        cre = _block_diag_out(ssm_c_re[l])
        cim = _block_diag_out(ssm_c_im[l])
        d_skip = ssm_d[l].reshape(1, s_w)
        mod_l = mod[l].reshape(c_rows, 6, d)

        def mods_for(lo, n, rep):
            m = mod_l[lo:lo + n]
            if rep == 1:
                return [m[:, i].reshape(n, 1, d) for i in range(6)]
            return [jnp.repeat(m[:, i], rep, axis=0).reshape(1, n * rep, d) for i in range(6)]

        sh1, sc1, g1, sh2, sc2, g2 = mods_for(0, nb, 1)
        h = _rmsnorm(xp, norm_attn[l], sc1, sh1)
        proj = [_mm_cast(h, w_in, l, c0, nc, F32) for c0, nc in in_cols]
        q, k, v, u, gates = [p[0] for p in proj]
        w_in_bf = [p[1] for p in proj]
        attn = _attn_fresh(q, k, v, slopes)
        u_perm = u.reshape(nb, N_SEG, seg, s_w).swapaxes(1, 2).reshape(nb, t, s_w)
        zero_st = jnp.zeros((nb, N_SEG, n_state), F32)
        end_re, end_im = _ssm_scan(u_perm, zero_st, zero_st, a_re, a_im, bre, bim, cre, cim, d_skip, False)
        ini_re, ini_im, fin_re, fin_im = _ssm_carry(end_re, end_im, a_re, a_im, seg)
        y_perm, _, _ = _ssm_scan(u_perm, ini_re, ini_im, a_re, a_im, bre, bim, cre, cim, d_skip, True)
        y = y_perm.reshape(nb, seg, N_SEG, s_w).swapaxes(1, 2).reshape(nb, t, s_w)
        ssm = _mm_glu(y, w_glu_bf, l)
        merged = _mm_merge(attn, ssm, w_attn_bf, w_ssm_bf, l, gates, d)
        xp = _mm_resid(merged, w_out_bf, l, xp, g1, 1024, 512)
        h2 = _rmsnorm(xp, norm_ffn[l], sc2, sh2)
        act, tail_a, tail_g, w_up_a_bf, w_up_g_bf = _ffn_up_fresh(h2, w_up, w_conv, b_conv, l, d_ff)
        xp = _mm_resid(act, w_down_bf, l, xp, g2, 1024, 256, lhs_buffers=1)
        outs["kp"].append(k.reshape(nb, t, kv_heads, HEAD_DIM))
        outs["vp"].append(v.reshape(nb, t, kv_heads, HEAD_DIM))
        outs["hpr"].append(fin_re.reshape(nb, n_groups, S5_STATE))
        outs["hpi"].append(fin_im.reshape(nb, n_groups, S5_STATE))
        keep = w_conv.shape[1] - 1
        outs["cp"].append(jnp.concatenate([tail_a[:, -1], tail_g[:, -1]], axis=-1)[:, V7X_SUBLANES - keep:, :])

        sh1, sc1, g1, sh2, sc2, g2 = mods_for(nb, ns_seq, ts)
        h = _rmsnorm(xs, norm_attn[l], sc1, sh1)
        q, k, v, u, gates = [_mm_plain(h, wb[None], 0, 0, nc, F32)
                             for wb, (_, nc) in zip(w_in_bf, in_cols)]
        q_s = q.reshape(ns_seq, ts, kv_heads, 2, HEAD_DIM)
        k_new = k.reshape(ns_seq, ts, kv_w)
        v_new = v.reshape(ns_seq, ts, kv_w)
        eye = jnp.eye(kv_heads, dtype=F32)
        wq = (jnp.transpose(q_s, (0, 2, 4, 3, 1))[:, :, :, None, :, :]
              * eye[None, :, None, :, None, None]).reshape(ns_seq, kv_w, n_col)
        wq = jnp.pad(wq, ((0, 0), (0, 0), (0, V7X_LANES - n_col)))
        wq_bf = wq.astype(BF16)
        scores, kmean = _paged_scores(page_table, wq_bf, cache_k, l, npg)
        o_s = _paged_attend(page_table, wq, wq_bf, scores, kmean, k_new, v_new,
                            slope_col, t_col, cache_v, l, npg)
        attn = jnp.transpose(o_s.reshape(ns_seq, kv_heads, 2, ts, HEAD_DIM), (0, 3, 1, 2, 4))
        attn = attn.reshape(1, ns_seq * ts, q_w)
        st_re = state_ssm_re[l].reshape(1, ns_seq, n_state)
        st_im = state_ssm_im[l].reshape(1, ns_seq, n_state)
        u_perm = jnp.swapaxes(u.reshape(ns_seq, ts, s_w), 0, 1).reshape(1, ts * ns_seq, s_w)
        y_perm, e_re, e_im = _ssm_scan(u_perm, st_re, st_im, a_re, a_im, bre, bim, cre, cim, d_skip, True)
        y = jnp.swapaxes(y_perm.reshape(ts, ns_seq, s_w), 0, 1).reshape(1, ns_seq * ts, s_w)
        ssm = _mm_glu(y, w_glu_bf, l)
        merged = _mm_merge(attn, ssm, w_attn_bf, w_ssm_bf, l, gates, d)
        xs = _mm_resid(merged, w_out_bf, l, xs, g1, 1024, 1024)
        h2 = _rmsnorm(xs, norm_ffn[l], sc2, sh2)
        keep = w_conv.shape[1] - 1
        cprev = state_conv[l]
        prev1 = jnp.concatenate([cprev[:, keep - 1:keep], jnp.zeros((ns_seq, ts - 1, 2 * d_ff), F32)], axis=1)
        prev2 = jnp.concatenate([cprev[:, keep - 2:keep], jnp.zeros((ns_seq, ts - 2, 2 * d_ff), F32)], axis=1)
        prev1 = prev1.reshape(1, ns_seq * ts, 2 * d_ff)
        prev2 = prev2.reshape(1, ns_seq * ts, 2 * d_ff)
        act, up_a, up_g = _ffn_up_carried(h2, w_up_a_bf, w_up_g_bf, w_conv, b_conv, l, d_ff,
                                          (prev1, prev2), ts)
        xs = _mm_resid(act, w_down_bf, l, xs, g2, 512, 512)
        outs["ks"].append(k_new.reshape(ns_seq, ts, kv_heads, HEAD_DIM))
        outs["vs"].append(v_new.reshape(ns_seq, ts, kv_heads, HEAD_DIM))
        outs["hsr"].append(e_re.reshape(ns_seq, n_groups, S5_STATE))
        outs["hsi"].append(e_im.reshape(ns_seq, n_groups, S5_STATE))
        up_full = jnp.concatenate([up_a, up_g], axis=-1).reshape(ns_seq, ts, 2 * d_ff)
        outs["cs"].append(up_full[:, ts - keep:, :])

    y_prompt = _rmsnorm(xp, norm_final, out_dtype=F32)
    y_sample = _rmsnorm(xs, norm_final, out_dtype=F32).reshape(ns_seq, ts, d)
    st = lambda k: jnp.stack(outs[k])
    return (y_prompt, y_sample, st("kp"), st("vp"), st("ks"), st("vs"),
            st("hpr"), st("hpi"), st("hsr"), st("hsi"), st("cp"), st("cs"))
```

```python
import functools
import math

import jax
import jax.numpy as jnp
from jax import lax
from jax.experimental import pallas as pl
from jax.experimental.pallas import tpu as pltpu

F32 = jnp.float32
BF16 = jnp.bfloat16

V7X_LANES = 128
V7X_SUBLANES = 8
V7X_VMEM_BYTES = 64 * 2**20
VMEM_BUDGET = V7X_VMEM_BYTES - 8 * 2**20

HEAD_DIM = 128
MOBA_BLOCK = 256
MOBA_TOPK = 3
S5_GROUP = 16
S5_STATE = 64
GROUPS_PER_CHUNK = 8
CHUNK_IN = GROUPS_PER_CHUNK * S5_GROUP
CHUNK_ST = GROUPS_PER_CHUNK * S5_STATE
N_SEG = V7X_SUBLANES
PAGES_PER_STEP = 16
EPS = 1e-6
NEG_INF = float("-inf")
LOG2_E = math.log2(math.e)
MASK_BIAS = -1e30


def _pick(dim, pref, align):
    t = min(pref, dim)
    t -= t % align
    while t >= align:
        if dim % t == 0:
            return t
        t -= align
    return dim


def _params(sem, vmem_bytes):
    limit = int(min(max(vmem_bytes * 5 // 4 + (4 << 20), 32 << 20), VMEM_BUDGET))
    return pltpu.CompilerParams(dimension_semantics=sem, vmem_limit_bytes=limit)


def _ada_kernel(c_ref, w_ref, b_ref, o_ref):
    s = jax.nn.silu(c_ref[...]).astype(BF16)
    w = w_ref[...].astype(BF16)
    o_ref[...] = jnp.dot(s, w, preferred_element_type=F32) + b_ref[...]


def _adaln(c_all, w_ada, b_ada):
    depth, d, n = w_ada.shape
    rows = c_all.shape[0]
    tn = _pick(n, 512, V7X_LANES)
    vmem = 2 * d * tn * 4 + d * tn * 2 + 4 * rows * (d + tn) * 4
    return pl.pallas_call(
        _ada_kernel,
        grid=(depth, n // tn),
        in_specs=[
            pl.BlockSpec((rows, d), lambda l, j: (0, 0)),
            pl.BlockSpec((None, d, tn), lambda l, j: (l, 0, j)),
            pl.BlockSpec((None, 1, tn), lambda l, j: (l, 0, j)),
        ],
        out_specs=pl.BlockSpec((None, rows, tn), lambda l, j: (l, 0, j)),
        out_shape=jax.ShapeDtypeStruct((depth, rows, n), F32),
        compiler_params=_params(("arbitrary", "arbitrary"), vmem),
        name="adaln",
    )(c_all, w_ada, b_ada.reshape(depth, 1, n))


def _norm_kernel(x_ref, g_ref, *rest, modulate):
    if modulate:
        sc_ref, sh_ref, o_ref = rest
    else:
        (o_ref,) = rest
    x = x_ref[...]
    y = x * lax.rsqrt(jnp.mean(x * x, axis=-1, keepdims=True) + EPS)
    y = y * g_ref[...]
    if modulate:
        y = y * (1.0 + sc_ref[...]) + sh_ref[...]
    o_ref[...] = y.astype(o_ref.dtype)


def _row_param_spec(p, tr, tn, col_blocked):
    shared = p.shape[1] == 1
    rows = 1 if shared else tr
    if col_blocked:
        if shared:
            return pl.BlockSpec((None, rows, tn), lambda b, i, j: (b, 0, j))
        return pl.BlockSpec((None, rows, tn), lambda b, i, j: (b, i, j))
    if shared:
        return pl.BlockSpec((None, rows, tn), lambda b, i: (b, 0, 0))
    return pl.BlockSpec((None, rows, tn), lambda b, i: (b, i, 0))


def _rmsnorm(x3, gamma, scale=None, shift=None, out_dtype=BF16):
    nblk, rb, d = x3.shape
    tr = _pick(rb, 512, V7X_SUBLANES)
    modulate = scale is not None
    in_specs = [
        pl.BlockSpec((None, tr, d), lambda b, i: (b, i, 0)),
        pl.BlockSpec((1, d), lambda b, i: (0, 0)),
    ]
    args = [x3, gamma.reshape(1, d)]
    if modulate:
        in_specs += [_row_param_spec(scale, tr, d, False), _row_param_spec(shift, tr, d, False)]
        args += [scale, shift]
    vmem = 2 * tr * d * (4 + 4) + 6 * tr * d * 4
    return pl.pallas_call(
        functools.partial(_norm_kernel, modulate=modulate),
        grid=(nblk, rb // tr),
        in_specs=in_specs,
        out_specs=pl.BlockSpec((None, tr, d), lambda b, i: (b, i, 0)),
        out_shape=jax.ShapeDtypeStruct((nblk, rb, d), out_dtype),
        compiler_params=_params(("arbitrary", "arbitrary"), vmem),
        name="rmsnorm_mod",
    )(*args)


def _mm_plain_kernel(a_ref, w_ref, o_ref):
    o_ref[...] = jnp.dot(a_ref[...], w_ref[...], preferred_element_type=F32).astype(o_ref.dtype)


def _w_spec(k, tn, layer, col_block0=0):
    return pl.BlockSpec((None, k, tn), lambda b, i, j: (layer, 0, j + col_block0))


def _mm_plain(a3, w, layer, col0, ncols, out_dtype, tm_pref=1024, tn_pref=1024):
    nblk, rb, k = a3.shape
    tm = _pick(rb, tm_pref, V7X_SUBLANES)
    tn = _pick(math.gcd(ncols, col0) if col0 else ncols, tn_pref, V7X_LANES)
    cb = col0 // tn
    osz = jnp.dtype(out_dtype).itemsize
    vmem = 2 * (tm * k * 2 + k * tn * 2 + tm * tn * osz) + tm * tn * 4
    return pl.pallas_call(
        _mm_plain_kernel,
        grid=(nblk, rb // tm, ncols // tn),
        in_specs=[
            pl.BlockSpec((None, tm, k), lambda b, i, j: (b, i, 0)),
            _w_spec(k, tn, layer, cb),
        ],
        out_specs=pl.BlockSpec((None, tm, tn), lambda b, i, j: (b, i, j)),
        out_shape=jax.ShapeDtypeStruct((nblk, rb, ncols), out_dtype),
        compiler_params=_params(("arbitrary", "arbitrary", "arbitrary"), vmem),
        name="proj",
    )(a3, w)


def _mm_glu_kernel(y_ref, w_ref, yt_ref, o_ref, ybf_ref):
    @pl.when(pl.program_id(2) == 0)
    def _():
        ybf_ref[...] = y_ref[...].astype(BF16)

    acc = jnp.dot(ybf_ref[...], w_ref[...], preferred_element_type=F32)
    o_ref[...] = (yt_ref[...] * jax.nn.sigmoid(acc)).astype(o_ref.dtype)


def _mm_glu(y3, w, layer):
    nblk, rb, k = y3.shape
    n = w.shape[2]
    tm = _pick(rb, 1024, V7X_SUBLANES)
    tn = _pick(n, 512, V7X_LANES)
    vmem = 2 * (tm * k * 4 + k * tn * 2 + tm * tn * 4 + tm * tn * 2) + tm * k * 2 + tm * tn * 4
    return pl.pallas_call(
        _mm_glu_kernel,
        grid=(nblk, rb // tm, n // tn),
        in_specs=[
            pl.BlockSpec((None, tm, k), lambda b, i, j: (b, i, 0)),
            _w_spec(k, tn, layer),
            pl.BlockSpec((None, tm, tn), lambda b, i, j: (b, i, j)),
        ],
        out_specs=pl.BlockSpec((None, tm, tn), lambda b, i, j: (b, i, j)),
        out_shape=jax.ShapeDtypeStruct((nblk, rb, n), BF16),
        scratch_shapes=[pltpu.VMEM((tm, k), BF16)],
        compiler_params=_params(("arbitrary", "arbitrary", "arbitrary"), vmem),
        name="ssm_glu",
    )(y3, w, y3)


def _mm_merge_kernel(a_ref, s_ref, wa_ref, ws_ref, ga_ref, gs_ref, o_ref):
    pa = jnp.dot(a_ref[...], wa_ref[...], preferred_element_type=F32)
    ps = jnp.dot(s_ref[...], ws_ref[...], preferred_element_type=F32)
    o = jax.nn.sigmoid(ga_ref[...]) * pa + jax.nn.sigmoid(gs_ref[...]) * ps
    o_ref[...] = o.astype(o_ref.dtype)


def _mm_merge(attn3, ssm3, w_attn, w_ssm, layer, gates3, d):
    nblk, rb, ka = attn3.shape
    ks = ssm3.shape[2]
    tm = _pick(rb, 1024, V7X_SUBLANES)
    tn = _pick(d, 512, V7X_LANES)
    gs_off = d // tn
    vmem = 2 * (tm * (ka + ks) * 2 + (ka + ks) * tn * 2 + 2 * tm * tn * 4 + tm * tn * 2) + 3 * tm * tn * 4
    return pl.pallas_call(
        _mm_merge_kernel,
        grid=(nblk, rb // tm, d // tn),
        in_specs=[
            pl.BlockSpec((None, tm, ka), lambda b, i, j: (b, i, 0)),
            pl.BlockSpec((None, tm, ks), lambda b, i, j: (b, i, 0)),
            _w_spec(ka, tn, layer),
            _w_spec(ks, tn, layer),
            pl.BlockSpec((None, tm, tn), lambda b, i, j: (b, i, j)),
            pl.BlockSpec((None, tm, tn), lambda b, i, j: (b, i, j + gs_off)),
        ],
        out_specs=pl.BlockSpec((None, tm, tn), lambda b, i, j: (b, i, j)),
        out_shape=jax.ShapeDtypeStruct((nblk, rb, d), BF16),
        compiler_params=_params(("arbitrary", "arbitrary", "arbitrary"), vmem),
        name="mixer_merge",
    )(attn3, ssm3, w_attn, w_ssm, gates3, gates3)


def _mm_resid_kernel(a_ref, w_ref, x_ref, g_ref, o_ref):
    acc = jnp.dot(a_ref[...], w_ref[...], preferred_element_type=F32)
    o_ref[...] = x_ref[...] + g_ref[...] * acc


def _mm_resid(a3, w, layer, x3, gate, tm_pref, tn_pref, lhs_buffers=2):
    nblk, rb, k = a3.shape
    n = w.shape[2]
    tm = _pick(rb, tm_pref, V7X_SUBLANES)
    tn = _pick(n, tn_pref, V7X_LANES)
    vmem = lhs_buffers * tm * k * 2 + 2 * (k * tn * 2 + 2 * tm * tn * 4) + tm * tn * 4
    return pl.pallas_call(
        _mm_resid_kernel,
        grid=(nblk, rb // tm, n // tn),
        in_specs=[
            pl.BlockSpec((None, tm, k), lambda b, i, j: (b, i, 0), pipeline_mode=pl.Buffered(lhs_buffers)),
            _w_spec(k, tn, layer),
            pl.BlockSpec((None, tm, tn), lambda b, i, j: (b, i, j)),
            _row_param_spec(gate, tm, tn, True),
        ],
        out_specs=pl.BlockSpec((None, tm, tn), lambda b, i, j: (b, i, j)),
        out_shape=jax.ShapeDtypeStruct((nblk, rb, n), F32),
        compiler_params=_params(("arbitrary", "arbitrary", "arbitrary"), vmem),
        name="proj_residual",
    )(a3, w, x3, gate)


def _conv3(up, r1, r2, cw_ref, cb_ref):
    cw = cw_ref[...]
    return cb_ref[...] + (cw[0:1] * r2 + cw[1:2] * r1 + cw[2:3] * up)


def _ffn_up_fresh_kernel(h_ref, wa_ref, wg_ref, cwa_ref, cwg_ref, cba_ref, cbg_ref,
                         act_ref, sa_ref, sg_ref, wbfa_ref, wbfg_ref, wsc_ref, tail_ref, *, tm):
    tail_rows = V7X_SUBLANES

    @pl.when((pl.program_id(1) == 0) & (pl.program_id(2) == 0))
    def _():
        for p, (w_ref, wbf_ref) in enumerate(((wa_ref, wbfa_ref), (wg_ref, wbfg_ref))):
            wb = w_ref[...].astype(BF16)
            wsc_ref[p] = wb
            wbf_ref[...] = wb

    @pl.when(pl.program_id(2) == 0)
    def _():
        tail_ref[...] = jnp.zeros_like(tail_ref)

    lhs = h_ref[...]
    parts = []
    for p, (cw_ref, cb_ref, s_ref) in enumerate(((cwa_ref, cba_ref, sa_ref), (cwg_ref, cbg_ref, sg_ref))):
        up = jnp.dot(lhs, wsc_ref[p], preferred_element_type=F32)
        ext = jnp.concatenate([tail_ref[p], up], axis=0)
        new_tail = ext[tm:, :]
        tail_ref[p] = new_tail
        s_ref[...] = new_tail
        r1 = pltpu.roll(ext, 1, 0)[tail_rows:]
        r2 = pltpu.roll(ext, 2, 0)[tail_rows:]
        parts.append(_conv3(up, r1, r2, cw_ref, cb_ref))
    act_ref[...] = (jax.nn.silu(parts[1]) * parts[0]).astype(act_ref.dtype)


def _ffn_up_carried_kernel(h_ref, wa_ref, wg_ref, cwa_ref, cwg_ref, cba_ref, cbg_ref,
                           p1a_ref, p2a_ref, p1g_ref, p2g_ref, act_ref, sa_ref, sg_ref, *, seq):
    lhs = h_ref[...]
    parts = []
    for w_ref, cw_ref, cb_ref, p1_ref, p2_ref, s_ref in (
            (wa_ref, cwa_ref, cba_ref, p1a_ref, p2a_ref, sa_ref),
            (wg_ref, cwg_ref, cbg_ref, p1g_ref, p2g_ref, sg_ref)):
        up = jnp.dot(lhs, w_ref[...], preferred_element_type=F32)
        s_ref[...] = up
        tloc = lax.broadcasted_iota(jnp.int32, up.shape, 0) % seq
        r1 = jnp.where(tloc < 1, p1_ref[...], pltpu.roll(up, 1, 0))
        r2 = jnp.where(tloc < 2, p2_ref[...], pltpu.roll(up, 2, 0))
        parts.append(_conv3(up, r1, r2, cw_ref, cb_ref))
    act_ref[...] = (jax.nn.silu(parts[1]) * parts[0]).astype(act_ref.dtype)


def _ffn_up_fresh(h3, w_up, w_conv, b_conv, layer, d_ff):
    nblk, rb, k = h3.shape
    tn = _pick(d_ff, 512, V7X_LANES)
    goff = d_ff // tn
    tm = _pick(rb, 1024, V7X_SUBLANES)
    cb = b_conv.reshape(b_conv.shape[0], 1, 2 * d_ff)
    wspec = lambda rows, off: pl.BlockSpec((None, rows, tn), lambda j, b, i: (layer, 0, j + off))
    tail_spec = pl.BlockSpec((None, None, V7X_SUBLANES, tn), lambda j, b, i: (b, i, 0, j))
    wbf_spec = pl.BlockSpec((k, tn), lambda j, b, i: (0, j))
    vmem = (2 * (tm * k * 2 + 2 * k * tn * 4 + 2 * k * tn * 2 + tm * tn * 2) + 2 * k * tn * 2
            + 10 * (tm + 8) * tn * 4)
    return pl.pallas_call(
        functools.partial(_ffn_up_fresh_kernel, tm=tm),
        grid=(goff, nblk, rb // tm),
        in_specs=[
            pl.BlockSpec((None, tm, k), lambda j, b, i: (b, i, 0)),
            wspec(k, 0), wspec(k, goff),
            wspec(w_conv.shape[1], 0), wspec(w_conv.shape[1], goff),
            wspec(1, 0), wspec(1, goff),
        ],
        out_specs=[
            pl.BlockSpec((None, tm, tn), lambda j, b, i: (b, i, j)),
            tail_spec, tail_spec, wbf_spec, wbf_spec,
        ],
        out_shape=[
            jax.ShapeDtypeStruct((nblk, rb, d_ff), BF16),
            jax.ShapeDtypeStruct((nblk, rb // tm, V7X_SUBLANES, d_ff), F32),
            jax.ShapeDtypeStruct((nblk, rb // tm, V7X_SUBLANES, d_ff), F32),
            jax.ShapeDtypeStruct((k, d_ff), BF16),
            jax.ShapeDtypeStruct((k, d_ff), BF16),
        ],
        scratch_shapes=[pltpu.VMEM((2, k, tn), BF16), pltpu.VMEM((2, V7X_SUBLANES, tn), F32)],
        compiler_params=_params(("arbitrary", "arbitrary", "arbitrary"), vmem),
        name="convffn_up",
    )(h3, w_up, w_up, w_conv, w_conv, cb, cb)


def _ffn_up_carried(h3, w_a, w_g, w_conv, b_conv, layer, d_ff, prev, seq):
    nblk, rb, k = h3.shape
    tn = _pick(d_ff, 512, V7X_LANES)
    goff = d_ff // tn
    cb = b_conv.reshape(b_conv.shape[0], 1, 2 * d_ff)
    p1, p2 = prev
    cspec = lambda rows, off: pl.BlockSpec((None, rows, tn), lambda b, j: (layer, 0, j + off))
    wspec = pl.BlockSpec((k, tn), lambda b, j: (0, j))
    row_spec = lambda off: pl.BlockSpec((None, rb, tn), lambda b, j: (b, 0, j + off))
    vmem = 2 * (rb * k * 2 + 2 * k * tn * 2 + 8 * rb * tn * 4) + 10 * rb * tn * 4
    return pl.pallas_call(
        functools.partial(_ffn_up_carried_kernel, seq=seq),
        grid=(nblk, goff),
        in_specs=[
            pl.BlockSpec((None, rb, k), lambda b, j: (b, 0, 0)),
            wspec, wspec,
            cspec(w_conv.shape[1], 0), cspec(w_conv.shape[1], goff),
            cspec(1, 0), cspec(1, goff),
            row_spec(0), row_spec(0), row_spec(goff), row_spec(goff),
        ],
        out_specs=[row_spec(0), row_spec(0), row_spec(0)],
        out_shape=[
            jax.ShapeDtypeStruct((nblk, rb, d_ff), BF16),
            jax.ShapeDtypeStruct((nblk, rb, d_ff), F32),
            jax.ShapeDtypeStruct((nblk, rb, d_ff), F32),
        ],
        compiler_params=_params(("arbitrary", "arbitrary"), vmem),
        name="convffn_up_carried",
    )(h3, w_a, w_g, w_conv, w_conv, cb, cb, p1, p2, p1, p2)


def _beats(other, gate, other_is_lower):
    return jnp.where(other > gate, 1.0, 0.0) + jnp.where(other == gate, 1.0, 0.0) * other_is_lower


def _attn_fresh_body(cc, q_ref, o_ref, kaug_ref, vbf_ref, kmean_ref, sd_ref, *, nblk):
    blk = MOBA_BLOCK
    scale = HEAD_DIM ** -0.5
    w = (cc + 1) * blk
    q2 = q_ref[...]
    qs = jnp.concatenate([q2[:, :HEAD_DIM], q2[:, HEAD_DIM:]], axis=0)
    if cc > 0:
        gate_t = lax.dot_general(kmean_ref[...], qs, (((1,), (1,)), ((), ())),
                                 precision=lax.Precision.HIGHEST, preferred_element_type=F32)
        riota = lax.broadcasted_iota(jnp.int32, gate_t.shape, 0)
        cnt = jnp.zeros_like(gate_t)
        for m in range(cc):
            cnt = cnt + _beats(gate_t[m:m + 1, :], gate_t, jnp.where(riota > m, 1.0, 0.0))
        keep = jnp.where(cnt < MOBA_TOPK, 1.0, 0.0) + jnp.where(riota >= cc, 1.0, 0.0)
        bias_t = jnp.where(keep > 0.5, 0.0, MASK_BIAS)
        bias_t = jnp.concatenate(
            [bias_t, jnp.zeros((HEAD_DIM - nblk, 2 * blk), F32)], axis=0)
        bias = bias_t.T.astype(BF16)
    else:
        bias = jnp.zeros((2 * blk, HEAD_DIM), BF16)
    q_aug = jnp.concatenate([qs.astype(BF16), bias], axis=1)
    s = lax.dot_general(q_aug, kaug_ref[0:w, :], (((1,), (1,)), ((), ())),
                        preferred_element_type=F32)
    row = lax.broadcasted_iota(jnp.int32, (blk, blk), 0)
    col = lax.broadcasted_iota(jnp.int32, (blk, blk), 1)
    causal = row >= col
    for hh in range(2):
        sh = s[hh * blk:(hh + 1) * blk, :] * (scale * LOG2_E) - sd_ref[hh, :, 0:w]
        diag = jnp.where(causal, sh[:, cc * blk:], NEG_INF)
        sh = jnp.concatenate([sh[:, :cc * blk], diag], axis=1) if cc > 0 else diag
        m = jnp.max(sh, axis=-1, keepdims=True)
        p = jnp.exp2(sh - m)
        l = jnp.sum(p, axis=-1, keepdims=True)
        o = jnp.dot(p.astype(BF16), vbf_ref[0:w, :], preferred_element_type=F32) / l
        o_ref[:, hh * HEAD_DIM:(hh + 1) * HEAD_DIM] = o.astype(o_ref.dtype)


def _attn_fresh_kernel(slope_ref, q_ref, k_ref, v_ref, o_ref,
                       kaug_ref, vbf_ref, kmean_ref, sd_ref, *, nblk):
    g = pl.program_id(1)
    c = pl.program_id(2)
    blk = MOBA_BLOCK
    t = nblk * blk

    @pl.when(c == 0)
    def _():
        k = k_ref[...]
        kaug_ref[:, 0:HEAD_DIM] = k.astype(BF16)
        key_blk = lax.broadcasted_iota(jnp.int32, (t, HEAD_DIM), 0) // blk
        lane = lax.broadcasted_iota(jnp.int32, (t, HEAD_DIM), 1)
        kaug_ref[:, HEAD_DIM:] = jnp.where(key_blk == lane, 1.0, 0.0).astype(BF16)
        vbf_ref[...] = v_ref[...].astype(BF16)
        for n in range(nblk):
            kmean_ref[n:n + 1, :] = jnp.mean(k[n * blk:(n + 1) * blk, :], axis=0, keepdims=True)
        d0 = (lax.broadcasted_iota(jnp.int32, (blk, t), 0)
              - lax.broadcasted_iota(jnp.int32, (blk, t), 1)).astype(F32)
        for hh in range(2):
            sd_ref[hh] = (slope_ref[2 * g + hh] * LOG2_E) * d0

    for cc in range(nblk):
        pl.when(c == cc)(functools.partial(
            _attn_fresh_body, cc, q_ref, o_ref, kaug_ref, vbf_ref, kmean_ref, sd_ref, nblk=nblk))


def _attn_fresh(q3, k3, v3, slopes):
    b, t, q_w = q3.shape
    kv_heads = k3.shape[2] // HEAD_DIM
    assert q_w == 2 * kv_heads * HEAD_DIM and t % MOBA_BLOCK == 0
    nblk = t // MOBA_BLOCK
    assert nblk <= HEAD_DIM
    blk = MOBA_BLOCK
    vmem = (2 * (blk * 2 * HEAD_DIM * 4 + 2 * t * HEAD_DIM * 4 + blk * 2 * HEAD_DIM * 2)
            + 3 * t * HEAD_DIM * 2 + 2 * blk * t * 4 + 8 * 2 * blk * t * 4)
    return pl.pallas_call(
        functools.partial(_attn_fresh_kernel, nblk=nblk),
        grid=(b, kv_heads, nblk),
        in_specs=[
            pl.BlockSpec(memory_space=pltpu.SMEM),
            pl.BlockSpec((None, blk, 2 * HEAD_DIM), lambda bb, g, c: (bb, c, g)),
            pl.BlockSpec((None, t, HEAD_DIM), lambda bb, g, c: (bb, 0, g)),
            pl.BlockSpec((None, t, HEAD_DIM), lambda bb, g, c: (bb, 0, g)),
        ],
        out_specs=pl.BlockSpec((None, blk, 2 * HEAD_DIM), lambda bb, g, c: (bb, c, g)),
        out_shape=jax.ShapeDtypeStruct((b, t, q_w), BF16),
        scratch_shapes=[
            pltpu.VMEM((t, 2 * HEAD_DIM), BF16),
            pltpu.VMEM((t, HEAD_DIM), BF16),
            pltpu.VMEM((nblk, HEAD_DIM), F32),
            pltpu.VMEM((2, blk, t), F32),
        ],
        compiler_params=_params(("arbitrary", "arbitrary", "arbitrary"), vmem),
        name="moba_fresh",
    )(slopes, q3, k3, v3)


def _page_heads_on_lanes(p_ref, page, kvh):
    return jnp.concatenate([p_ref[pl.ds(g, page, stride=kvh), :] for g in range(kvh)], axis=-1)


def _page_specs(layer, npg, page_rows, hd):
    def spec(j):
        return pl.BlockSpec((None, None, page_rows, hd),
                            lambda i, n, pt: (layer, pt[i, n * npg + j], 0, 0))
    return [spec(j) for j in range(npg)]


def _paged_scores_kernel(pt_ref, wq_ref, *refs, page, kvh, npg):
    pages = refs[:npg]
    sc_ref, km_ref = refs[npg:]
    wq = wq_ref[...]
    hd = pages[0].shape[1]
    sums = []
    for j, p_ref in enumerate(pages):
        kcat = _page_heads_on_lanes(p_ref, page, kvh)
        sc_ref[j * page:(j + 1) * page, :] = jnp.dot(kcat.astype(BF16), wq, preferred_element_type=F32)
        sums.append(jnp.sum(p_ref[...].reshape(page, kvh, hd), axis=0))
    for b in range(npg // 2):
        km_ref[b] = (sums[2 * b] + sums[2 * b + 1]) * (1.0 / (2 * page))


def _paged_scores(page_table, wq_bf, cache, layer, npg):
    s, n_pages = page_table.shape
    depth, n_pool, page, kvh, hd = cache.shape
    assert 2 * page == MOBA_BLOCK and n_pages % npg == 0 and npg % 2 == 0
    kv_w = kvh * hd
    cache_rows = cache.reshape(depth, n_pool, page * kvh, hd)
    nblk = n_pages // 2
    past = n_pages * page
    vmem = 2 * (kv_w * V7X_LANES * 2 + npg * page * kv_w * 4 + npg * page * V7X_LANES * 4) + 6 * page * kv_w * 4
    grid_spec = pltpu.PrefetchScalarGridSpec(
        num_scalar_prefetch=1,
        grid=(s, n_pages // npg),
        in_specs=[pl.BlockSpec((None, kv_w, V7X_LANES), lambda i, n, pt: (i, 0, 0))]
        + _page_specs(layer, npg, page * kvh, hd),
        out_specs=[
            pl.BlockSpec((None, npg * page, V7X_LANES), lambda i, n, pt: (i, n, 0)),
            pl.BlockSpec((None, npg // 2, kvh, hd), lambda i, n, pt: (i, n, 0, 0)),
        ],
    )
    return pl.pallas_call(
        functools.partial(_paged_scores_kernel, page=page, kvh=kvh, npg=npg),
        grid_spec=grid_spec,
        out_shape=[
            jax.ShapeDtypeStruct((s, past, V7X_LANES), F32),
            jax.ShapeDtypeStruct((s, nblk, kvh, hd), F32),
        ],
        compiler_params=_params(("arbitrary", "arbitrary"), vmem),
        name="moba_paged_scores",
    )(page_table, wq_bf, *([cache_rows] * npg))


def _paged_attend_kernel(pt_ref, wq_ref, wqbf_ref, sc_ref, km_ref, knew_ref, vnew_ref,
                         slope_ref, tcol_ref, *refs, page, kvh, npg, nblk, tnew, past):
    pages = refs[:npg]
    o_ref, sel_ref, prob_ref, linv_ref, sq_ref, vpad_ref, acc_ref = refs[npg:]
    n = pl.program_id(1)
    blk = MOBA_BLOCK
    scale = HEAD_DIM ** -0.5
    lanes = V7X_LANES
    slope = slope_ref[...]
    tcol = tcol_ref[...]
    koff = lax.broadcasted_iota(jnp.int32, (blk, lanes), 0).astype(F32)

    def block_scores(i):
        r0 = pl.multiple_of(i * blk, blk)
        raw = sc_ref[pl.ds(r0, blk), :]
        dist = (tcol + lax.convert_element_type(past - i * blk, F32)) - koff
        s = raw * scale - slope * dist
        return jnp.where(sel_ref[pl.ds(i, 1), :] > 0.5, s, NEG_INF)

    @pl.when(n == 0)
    def _():
        gate = jnp.zeros((nblk, lanes), F32)
        for g in range(kvh):
            gate = gate + jnp.dot(km_ref[pl.ds(g, nblk, stride=kvh), :],
                                  wq_ref[g * HEAD_DIM:(g + 1) * HEAD_DIM, :],
                                  precision=lax.Precision.HIGHEST, preferred_element_type=F32)
        riota = lax.broadcasted_iota(jnp.int32, (nblk, lanes), 0)
        cnt = jnp.zeros((nblk, lanes), F32)
        for m in range(nblk):
            cnt = cnt + _beats(gate[m:m + 1, :], gate, jnp.where(riota > m, 1.0, 0.0))
        sel_ref[...] = jnp.where(cnt < MOBA_TOPK, 1.0, 0.0)

        s_cur = jnp.dot(knew_ref[...].astype(BF16), wqbf_ref[...], preferred_element_type=F32)
        off = lax.broadcasted_iota(jnp.int32, (tnew, lanes), 0).astype(F32)
        s_cur = s_cur * scale - slope * (tcol - off)
        s_cur = jnp.where(off <= tcol, s_cur, NEG_INF)
        m0 = jnp.max(s_cur, axis=0, keepdims=True)

        def max_body(i, m):
            return jnp.maximum(m, jnp.max(block_scores(i), axis=0, keepdims=True))

        mx = lax.fori_loop(0, nblk, max_body, m0)
        p_cur = jnp.exp(s_cur - mx)

        def sum_body(i, l):
            p = jnp.exp(block_scores(i) - mx)
            prob_ref[pl.ds(pl.multiple_of(i * blk, blk), blk), :] = p
            return l + jnp.sum(p, axis=0, keepdims=True)

        l = lax.fori_loop(0, nblk, sum_body, jnp.sum(p_cur, axis=0, keepdims=True))
        linv = 1.0 / l
        linv_ref[...] = linv
        sq_ref[...] = jnp.zeros_like(sq_ref)
        sq_ref[0:tnew, :] = p_cur * linv
        vpad_ref[...] = jnp.zeros_like(vpad_ref)
        vpad_ref[0:tnew, :] = vnew_ref[...].astype(BF16)
        acc_ref[...] = jnp.dot(sq_ref[...].T.astype(BF16), vpad_ref[...], preferred_element_type=F32)

    linv = linv_ref[...]
    acc = acc_ref[...]
    for j, p_ref in enumerate(pages):
        r0 = pl.multiple_of((n * npg + j) * page, page)
        pt_j = (prob_ref[pl.ds(r0, page), :] * linv).T.astype(BF16)
        vcat = _page_heads_on_lanes(p_ref, page, kvh).astype(BF16)
        acc = acc + jnp.dot(pt_j, vcat, preferred_element_type=F32)
    acc_ref[...] = acc

    @pl.when(n == pl.num_programs(1) - 1)
    def _():
        rows = o_ref.shape[1]
        for g in range(o_ref.shape[0]):
            o_ref[g] = acc_ref[g * rows:(g + 1) * rows, g * HEAD_DIM:(g + 1) * HEAD_DIM].astype(o_ref.dtype)


def _paged_attend(page_table, wq, wq_bf, scores, kmean, k_new, v_new, slope_col, t_col, cache, layer, npg):
    s, n_pages = page_table.shape
    depth, n_pool, page, kvh, hd = cache.shape
    kv_w = kvh * hd
    cache_rows = cache.reshape(depth, n_pool, page * kvh, hd)
    nblk = n_pages // 2
    past = n_pages * page
    tnew = k_new.shape[1]
    rows = 2 * tnew
    kmean_rows = kmean.reshape(s, nblk * kvh, hd)
    vmem = (2 * (kv_w * V7X_LANES * 6 + past * V7X_LANES * 4 + nblk * kv_w * 4 + npg * page * kv_w * 4)
            + past * V7X_LANES * 4 + 3 * V7X_LANES * kv_w * 4 + 16 * MOBA_BLOCK * V7X_LANES * 4
            + 4 * page * kv_w * 4)
    grid_spec = pltpu.PrefetchScalarGridSpec(
        num_scalar_prefetch=1,
        grid=(s, n_pages // npg),
        in_specs=[
            pl.BlockSpec((None, kv_w, V7X_LANES), lambda i, n, pt: (i, 0, 0)),
            pl.BlockSpec((None, kv_w, V7X_LANES), lambda i, n, pt: (i, 0, 0)),
            pl.BlockSpec((None, past, V7X_LANES), lambda i, n, pt: (i, 0, 0)),
            pl.BlockSpec((None, nblk * kvh, hd), lambda i, n, pt: (i, 0, 0)),
            pl.BlockSpec((None, tnew, kv_w), lambda i, n, pt: (i, 0, 0)),
            pl.BlockSpec((None, tnew, kv_w), lambda i, n, pt: (i, 0, 0)),
            pl.BlockSpec((1, V7X_LANES), lambda i, n, pt: (0, 0)),
            pl.BlockSpec((1, V7X_LANES), lambda i, n, pt: (0, 0)),
        ] + _page_specs(layer, npg, page * kvh, hd),
        out_specs=pl.BlockSpec((None, kvh, rows, hd), lambda i, n, pt: (i, 0, 0, 0)),
        scratch_shapes=[
            pltpu.VMEM((nblk, V7X_LANES), F32),
            pltpu.VMEM((past, V7X_LANES), F32),
            pltpu.VMEM((1, V7X_LANES), F32),
            pltpu.VMEM((V7X_LANES, V7X_LANES), F32),
            pltpu.VMEM((V7X_LANES, kv_w), BF16),
            pltpu.VMEM((V7X_LANES, kv_w), F32),
        ],
    )
    return pl.pallas_call(
        functools.partial(_paged_attend_kernel, page=page, kvh=kvh, npg=npg, nblk=nblk, tnew=tnew, past=past),
        grid_spec=grid_spec,
        out_shape=jax.ShapeDtypeStruct((s, kvh, rows, hd), BF16),
        compiler_params=_params(("arbitrary", "arbitrary"), vmem),
        name="moba_paged_attend",
    )(page_table, wq, wq_bf, scores, kmean_rows, k_new, v_new, slope_col, t_col, *([cache_rows] * npg))


def _ssm_prep_kernel(lr_ref, li_ref, ldt_ref, brt_ref, bit_ref, are_ref, aim_ref, bbr_ref, bbi_ref):
    lr = lr_ref[...]
    li = li_ref[...]
    dt = jnp.exp(ldt_ref[...])
    mag = jnp.exp(lr * dt)
    a_re = mag * jnp.cos(li * dt)
    a_im = mag * jnp.sin(li * dt)
    den = lr * lr + li * li
    q_re = ((a_re - 1.0) * lr + a_im * li) / den
    q_im = (a_im * lr - (a_re - 1.0) * li) / den
    are_ref[...] = a_re
    aim_ref[...] = a_im
    br = brt_ref[...]
    bi = bit_ref[...]
    bbr_ref[...] = q_re * br - q_im * bi
    bbi_ref[...] = q_re * bi + q_im * br


def _ssm_prep(lam_re, lam_im, log_dt, b_re, b_im):
    depth, g, p = lam_re.shape
    i = b_re.shape[3]
    brt = jnp.swapaxes(b_re, 2, 3)
    bit = jnp.swapaxes(b_im, 2, 3)
    vec = pl.BlockSpec((None, g, 1, p), lambda l: (l, 0, 0, 0))
    mat = pl.BlockSpec((None, g, i, p), lambda l: (l, 0, 0, 0))
    return pl.pallas_call(
        _ssm_prep_kernel,
        grid=(depth,),
        in_specs=[vec, vec, pl.BlockSpec((None, g, 1, 1), lambda l: (l, 0, 0, 0)), mat, mat],
        out_specs=[vec, vec, mat, mat],
        out_shape=[
            jax.ShapeDtypeStruct((depth, g, 1, p), F32),
            jax.ShapeDtypeStruct((depth, g, 1, p), F32),
            jax.ShapeDtypeStruct((depth, g, i, p), F32),
            jax.ShapeDtypeStruct((depth, g, i, p), F32),
        ],
        compiler_params=_params(("arbitrary",), 32 << 20),
        name="s5_discretise",
    )(lam_re.reshape(depth, g, 1, p), lam_im.reshape(depth, g, 1, p),
      log_dt.reshape(depth, g, 1, 1), brt, bit)


def _ssm_scan_kernel(u_ref, ire_ref, iim_ref, are_ref, aim_ref, bre_ref, bim_ref,
                     cre_ref, cim_ref, d_ref, *rest, with_y, n_chunks, tau_b):
    if with_y:
        y_ref, ere_ref, eim_ref, hre_ref, him_ref, bure_ref, buim_ref = rest
    else:
        ere_ref, eim_ref, hre_ref, him_ref, bure_ref, buim_ref = rest
    k = pl.program_id(1)

    @pl.when(k == 0)
    def _():
        hre_ref[...] = ire_ref[...]
        him_ref[...] = iim_ref[...]

    def chunk(c, carry):
        cu = pl.multiple_of(c * CHUNK_IN, CHUNK_IN)
        cs = pl.multiple_of(c * CHUNK_ST, CHUNK_ST)
        u_c = u_ref[:, pl.ds(cu, CHUNK_IN)]
        ub = u_c.astype(BF16)
        bure_ref[...] = jnp.dot(ub, bre_ref[c], preferred_element_type=F32)
        buim_ref[...] = jnp.dot(ub, bim_ref[c], preferred_element_type=F32)
        ar = jnp.broadcast_to(are_ref[:, pl.ds(cs, CHUNK_ST)], (N_SEG, CHUNK_ST))
        ai = jnp.broadcast_to(aim_ref[:, pl.ds(cs, CHUNK_ST)], (N_SEG, CHUNK_ST))
        hr = hre_ref[:, pl.ds(cs, CHUNK_ST)]
        hi = him_ref[:, pl.ds(cs, CHUNK_ST)]
        for t in range(tau_b):
            rs = slice(t * N_SEG, (t + 1) * N_SEG)
            nr = ar * hr - ai * hi + bure_ref[rs, :]
            ni = ar * hi + ai * hr + buim_ref[rs, :]
            hr, hi = nr, ni
            if with_y:
                bure_ref[rs, :] = hr
                buim_ref[rs, :] = hi
        hre_ref[:, pl.ds(cs, CHUNK_ST)] = hr
        him_ref[:, pl.ds(cs, CHUNK_ST)] = hi
        if with_y:
            ych = (jnp.dot(bure_ref[...].astype(BF16), cre_ref[c], preferred_element_type=F32)
                   - jnp.dot(buim_ref[...].astype(BF16), cim_ref[c], preferred_element_type=F32))
            y = ych + d_ref[:, pl.ds(cu, CHUNK_IN)] * u_c
            y_ref[:, pl.ds(cu, CHUNK_IN)] = jax.nn.gelu(y)
        return carry

    lax.fori_loop(0, n_chunks, chunk, 0)

    @pl.when(k == pl.num_programs(1) - 1)
    def _():
        ere_ref[...] = hre_ref[...]
        eim_ref[...] = him_ref[...]


def _ssm_scan(u3, init_re, init_im, a_re, a_im, bre, bim, cre, cim, d_skip, with_y):
    nb, lr, s_w = u3.shape
    ns = a_re.shape[1]
    n_chunks = bre.shape[0]
    steps = lr // N_SEG
    tau_b = _pick(steps, 32, 1)
    rb = tau_b * N_SEG
    st_spec = pl.BlockSpec((None, N_SEG, ns), lambda b, k: (b, 0, 0))
    full = lambda shape: pl.BlockSpec(shape, lambda b, k: (0,) * len(shape))
    u_spec = pl.BlockSpec((None, rb, s_w), lambda b, k: (b, k, 0))
    out_specs = [st_spec, st_spec]
    out_shape = [jax.ShapeDtypeStruct((nb, N_SEG, ns), F32)] * 2
    if with_y:
        out_specs = [u_spec] + out_specs
        out_shape = [jax.ShapeDtypeStruct((nb, lr, s_w), F32)] + out_shape
    vmem = (4 * rb * s_w * 4 + 8 * N_SEG * ns * 4 + 4 * ns * 4
            + 2 * 4 * n_chunks * CHUNK_IN * CHUNK_ST * 2 + 8 * rb * CHUNK_ST * 4)
    return pl.pallas_call(
        functools.partial(_ssm_scan_kernel, with_y=with_y, n_chunks=n_chunks, tau_b=tau_b),
        grid=(nb, steps // tau_b),
        in_specs=[u_spec, st_spec, st_spec, full((1, ns)), full((1, ns)),
                  full(bre.shape), full(bim.shape), full(cre.shape), full(cim.shape),
                  full((1, s_w))],
        out_specs=out_specs,
        out_shape=out_shape,
        scratch_shapes=[
            pltpu.VMEM((N_SEG, ns), F32),
            pltpu.VMEM((N_SEG, ns), F32),
            pltpu.VMEM((rb, CHUNK_ST), F32),
            pltpu.VMEM((rb, CHUNK_ST), F32),
        ],
        compiler_params=_params(("arbitrary", "arbitrary"), vmem),
        name="s5_scan" if with_y else "s5_segment_ends",
    )(u3, init_re, init_im, a_re, a_im, bre, bim, cre, cim, d_skip)


def _ssm_carry_kernel(ere_ref, eim_ref, are_ref, aim_ref, ire_ref, iim_ref, fre_ref, fim_ref, *, seg_len):
    br = are_ref[...]
    bi = aim_ref[...]
    pr = jnp.ones_like(br)
    pi = jnp.zeros_like(br)
    e = seg_len
    while e:
        if e & 1:
            pr, pi = pr * br - pi * bi, pr * bi + pi * br
        br, bi = br * br - bi * bi, 2.0 * br * bi
        e >>= 1
    hr = jnp.zeros_like(pr)
    hi = jnp.zeros_like(pr)
    for j in range(N_SEG):
        ire_ref[j:j + 1, :] = hr
        iim_ref[j:j + 1, :] = hi
        er = ere_ref[j:j + 1, :]
        ei = eim_ref[j:j + 1, :]
        hr, hi = pr * hr - pi * hi + er, pr * hi + pi * hr + ei
    fre_ref[...] = hr
    fim_ref[...] = hi


def _ssm_carry(end_re, end_im, a_re, a_im, seg_len):
    nb, _, ns = end_re.shape
    st = pl.BlockSpec((None, N_SEG, ns), lambda b: (b, 0, 0))
    vec = pl.BlockSpec((1, ns), lambda b: (0, 0))
    fin = pl.BlockSpec((None, 1, ns), lambda b: (b, 0, 0))
    return pl.pallas_call(
        functools.partial(_ssm_carry_kernel, seg_len=seg_len),
        grid=(nb,),
        in_specs=[st, st, vec, vec],
        out_specs=[st, st, fin, fin],
        out_shape=[jax.ShapeDtypeStruct((nb, N_SEG, ns), F32)] * 2
        + [jax.ShapeDtypeStruct((nb, 1, ns), F32)] * 2,
        compiler_params=_params(("arbitrary",), 32 << 20),
        name="s5_segment_carry",
    )(end_re, end_im, a_re, a_im)


def _block_diag_in(bbt):
    g, i, p = bbt.shape
    nc = g // GROUPS_PER_CHUNK
    eye = jnp.eye(GROUPS_PER_CHUNK, dtype=bbt.dtype)
    x = bbt.reshape(nc, GROUPS_PER_CHUNK, i, p)
    out = x[:, :, :, None, :] * eye[None, :, None, :, None]
    return out.reshape(nc, GROUPS_PER_CHUNK * i, GROUPS_PER_CHUNK * p).astype(BF16)


def _block_diag_out(c):
    g, i, p = c.shape
    nc = g // GROUPS_PER_CHUNK
    eye = jnp.eye(GROUPS_PER_CHUNK, dtype=c.dtype)
    x = jnp.swapaxes(c.reshape(nc, GROUPS_PER_CHUNK, i, p), 2, 3)
    out = x[:, :, :, None, :] * eye[None, :, None, :, None]
    return out.reshape(nc, GROUPS_PER_CHUNK * p, GROUPS_PER_CHUNK * i).astype(BF16)


def kernel(x_prompt, x_sample, c_prompt, c_sample, cache_k, cache_v, state_ssm_re, state_ssm_im,
           state_conv, page_table, w_ada, b_ada, norm_attn, w_in, w_attn_proj, lam_re, lam_im,
           log_dt, ssm_b_re, ssm_b_im, ssm_c_re, ssm_c_im, ssm_d, w_glu, w_ssm_proj, w_out,
           norm_ffn, w_up, w_conv, b_conv, w_down, norm_final):
    depth = w_ada.shape[0]
    nb, t, d = x_prompt.shape
    ns_seq, ts, _ = x_sample.shape
    q_w = w_attn_proj.shape[1]
    n_heads = q_w // HEAD_DIM
    kv_heads = cache_k.shape[3]
    kv_w = kv_heads * HEAD_DIM
    s_w = ssm_d.shape[1]
    n_groups = lam_re.shape[1]
    n_state = n_groups * S5_STATE
    d_ff = w_down.shape[1]
    n_pages = page_table.shape[1]
    page = cache_k.shape[2]
    past = n_pages * page
    assert ns_seq == N_SEG and t % (N_SEG * N_SEG) == 0 and n_groups % GROUPS_PER_CHUNK == 0
    n_col = kv_heads * 2 * ts
    assert n_col <= V7X_LANES and n_heads == 2 * kv_heads
    assert ssm_b_re.shape[2:] == (S5_STATE, S5_GROUP)
    u_col = q_w + 2 * kv_w
    g_col = u_col + s_w

    n_c = nb + ns_seq
    c_rows = -(-n_c // V7X_SUBLANES) * V7X_SUBLANES
    c_all = jnp.concatenate([c_prompt, c_sample, jnp.zeros((c_rows - n_c, d), F32)], axis=0)
    mod = _adaln(c_all, w_ada, b_ada)

    a_re_all, a_im_all, bbr_all, bbi_all = _ssm_prep(lam_re, lam_im, log_dt, ssm_b_re, ssm_b_im)
    slopes = 2.0 ** (-8.0 * jnp.arange(1, n_heads + 1, dtype=F32) / n_heads)

    col = jnp.arange(V7X_LANES)
    col_head = jnp.minimum(2 * (col // (2 * ts)) + (col // ts) % 2, n_heads - 1)
    slope_col = slopes[col_head].reshape(1, V7X_LANES)
    t_col = (col % ts).astype(F32).reshape(1, V7X_LANES)

    xp = x_prompt
    xs = x_sample.reshape(1, ns_seq * ts, d)
    seg = t // N_SEG
    keep = w_conv.shape[1] - 1
    assert keep == 2 and past % MOBA_BLOCK == 0 and ts <= MOBA_BLOCK
    npg = _pick(n_pages, PAGES_PER_STEP, 2)
    outs ={k: [] for k in ("kp", "vp", "ks", "vs", "hpr", "hpi", "hsr", "hsi", "cp", "cs")}

    w_in_bf = w_in.astype(BF16)
    w_attn_bf = w_attn_proj.astype(BF16)
    w_glu_bf = w_glu.astype(BF16)
    w_ssm_bf = w_ssm_proj.astype(BF16)
    w_out_bf = w_out.astype(BF16)
    w_down_bf = w_down.astype(BF16)
    in_cols = ((0, q_w), (q_w, kv_w), (q_w + kv_w, kv_w), (u_col, s_w), (g_col, 2 * d))

    for l in range(depth):
        a_re = a_re_all[l].reshape(1, n_state)
        a_im = a_im_all[l].reshape(1, n_state)
        bre = _block_diag_in(bbr_all[l])
        bim = _block_diag_in(bbi_all[l])
        cre = _block_diag_out(ssm_c_re[l])
        cim = _block_diag_out(ssm_c_im[l])
        d_skip = ssm_d[l].reshape(1, s_w)
        mod_l = mod[l].reshape(c_rows, 6, d)

        def mods_for(lo, n, rep):
            m = mod_l[lo:lo + n]
            if rep == 1:
                return [m[:, i].reshape(n, 1, d) for i in range(6)]
            return [jnp.repeat(m[:, i], rep, axis=0).reshape(1, n * rep, d) for i in range(6)]

        sh1, sc1, g1, sh2, sc2, g2 = mods_for(0, nb, 1)
        h = _rmsnorm(xp, norm_attn[l], sc1, sh1)
        q, k, v, u, gates = [_mm_plain(h, w_in_bf, l, c0, nc, F32) for c0, nc in in_cols]
        attn = _attn_fresh(q, k, v, slopes)
        u_perm = u.reshape(nb, N_SEG, seg, s_w).swapaxes(1, 2).reshape(nb, t, s_w)
        zero_st = jnp.zeros((nb, N_SEG, n_state), F32)
        end_re, end_im = _ssm_scan(u_perm, zero_st, zero_st, a_re, a_im, bre, bim, cre, cim, d_skip, False)
        ini_re, ini_im, fin_re, fin_im = _ssm_carry(end_re, end_im, a_re, a_im, seg)
        y_perm, _, _ = _ssm_scan(u_perm, ini_re, ini_im, a_re, a_im, bre, bim, cre, cim, d_skip, True)
        y = y_perm.reshape(nb, seg, N_SEG, s_w).swapaxes(1, 2).reshape(nb, t, s_w)
        ssm = _mm_glu(y, w_glu_bf, l)
        merged = _mm_merge(attn, ssm, w_attn_bf, w_ssm_bf, l, gates, d)
        xp = _mm_resid(merged, w_out_bf, l, xp, g1, 1024, 512)
        h2 = _rmsnorm(xp, norm_ffn[l], sc2, sh2)
        act, tail_a, tail_g, w_up_a_bf, w_up_g_bf = _ffn_up_fresh(h2, w_up, w_conv, b_conv, l, d_ff)
        xp = _mm_resid(act, w_down_bf, l, xp, g2, 1024, 256, lhs_buffers=1)
        outs["kp"].append(k.reshape(nb, t, kv_heads, HEAD_DIM))
        outs["vp"].append(v.reshape(nb, t, kv_heads, HEAD_DIM))
        outs["hpr"].append(fin_re.reshape(nb, n_groups, S5_STATE))
        outs["hpi"].append(fin_im.reshape(nb, n_groups, S5_STATE))
        keep = w_conv.shape[1] - 1
        outs["cp"].append(jnp.concatenate([tail_a[:, -1], tail_g[:, -1]], axis=-1)[:, V7X_SUBLANES - keep:, :])

        sh1, sc1, g1, sh2, sc2, g2 = mods_for(nb, ns_seq, ts)
        h = _rmsnorm(xs, norm_attn[l], sc1, sh1)
        q, k, v, u, gates = [_mm_plain(h, w_in_bf, l, c0, nc, F32) for c0, nc in in_cols]
        q_s = q.reshape(ns_seq, ts, kv_heads, 2, HEAD_DIM)
        k_new = k.reshape(ns_seq, ts, kv_w)
        v_new = v.reshape(ns_seq, ts, kv_w)
        eye = jnp.eye(kv_heads, dtype=F32)
        wq = (jnp.transpose(q_s, (0, 2, 4, 3, 1))[:, :, :, None, :, :]
              * eye[None, :, None, :, None, None]).reshape(ns_seq, kv_w, n_col)
        wq = jnp.pad(wq, ((0, 0), (0, 0), (0, V7X_LANES - n_col)))
        wq_bf = wq.astype(BF16)
        scores, kmean = _paged_scores(page_table, wq_bf, cache_k, l, npg)
        o_s = _paged_attend(page_table, wq, wq_bf, scores, kmean, k_new, v_new,
                            slope_col, t_col, cache_v, l, npg)
        attn = jnp.transpose(o_s.reshape(ns_seq, kv_heads, 2, ts, HEAD_DIM), (0, 3, 1, 2, 4))
        attn = attn.reshape(1, ns_seq * ts, q_w)
        st_re = state_ssm_re[l].reshape(1, ns_seq, n_state)
        st_im = state_ssm_im[l].reshape(1, ns_seq, n_state)
        u_perm = jnp.swapaxes(u.reshape(ns_seq, ts, s_w), 0, 1).reshape(1, ts * ns_seq, s_w)
        y_perm, e_re, e_im = _ssm_scan(u_perm, st_re, st_im, a_re, a_im, bre, bim, cre, cim, d_skip, True)
        y = jnp.swapaxes(y_perm.reshape(ts, ns_seq, s_w), 0, 1).reshape(1, ns_seq * ts, s_w)
        ssm = _mm_glu(y, w_glu_bf, l)
        merged = _mm_merge(attn, ssm, w_attn_bf, w_ssm_bf, l, gates, d)
        xs = _mm_resid(merged, w_out_bf, l, xs, g1, 1024, 1024)
        h2 = _rmsnorm(xs, norm_ffn[l], sc2, sh2)
        keep = w_conv.shape[1] - 1
        cprev = state_conv[l]
        prev1 = jnp.concatenate([cprev[:, keep - 1:keep], jnp.zeros((ns_seq, ts - 1, 2 * d_ff), F32)], axis=1)
        prev2 = jnp.concatenate([cprev[:, keep - 2:keep], jnp.zeros((ns_seq, ts - 2, 2 * d_ff), F32)], axis=1)
        prev1 = prev1.reshape(1, ns_seq * ts, 2 * d_ff)
        prev2 = prev2.reshape(1, ns_seq * ts, 2 * d_ff)
        act, up_a, up_g = _ffn_up_carried(h2, w_up_a_bf, w_up_g_bf, w_conv, b_conv, l, d_ff,
                                          (prev1, prev2), ts)
        xs = _mm_resid(act, w_down_bf, l, xs, g2, 512, 512)
        outs["ks"].append(k_new.reshape(ns_seq, ts, kv_heads, HEAD_DIM))
        outs["vs"].append(v_new.reshape(ns_seq, ts, kv_heads, HEAD_DIM))
        outs["hsr"].append(e_re.reshape(ns_seq, n_groups, S5_STATE))
        outs["hsi"].append(e_im.reshape(ns_seq, n_groups, S5_STATE))
        up_full = jnp.concatenate([up_a, up_g], axis=-1).reshape(ns_seq, ts, 2 * d_ff)
        outs["cs"].append(up_full[:, ts - keep:, :])

    y_prompt = _rmsnorm(xp, norm_final, out_dtype=F32)
    y_sample = _rmsnorm(xs, norm_final, out_dtype=F32).reshape(ns_seq, ts, d)
    st = lambda k: jnp.stack(outs[k])
    return (y_prompt, y_sample, st("kp"), st("vp"), st("ks"), st("vs"),
            st("hpr"), st("hpi"), st("hsr"), st("hsi"), st("cp"), st("cs"))
```

```python
import functools
import math

import jax
import jax.numpy as jnp
from jax import lax
from jax.experimental import pallas as pl
from jax.experimental.pallas import tpu as pltpu

F32 = jnp.float32
BF16 = jnp.bfloat16

V7X_LANES = 128
V7X_SUBLANES = 8
V7X_VMEM_BYTES = 64 * 2**20
VMEM_BUDGET = V7X_VMEM_BYTES - 8 * 2**20

HEAD_DIM = 128
MOBA_BLOCK = 256
MOBA_TOPK = 3
S5_GROUP = 16
S5_STATE = 64
GROUPS_PER_CHUNK = 8
CHUNK_IN = GROUPS_PER_CHUNK * S5_GROUP
CHUNK_ST = GROUPS_PER_CHUNK * S5_STATE
N_SEG = V7X_SUBLANES
PAGES_PER_STEP = 16
EPS = 1e-6
NEG_INF = float("-inf")
LOG2_E = math.log2(math.e)
MASK_BIAS = -1e30


def _pick(dim, pref, align):
    t = min(pref, dim)
    t -= t % align
    while t >= align:
        if dim % t == 0:
            return t
        t -= align
    return dim


def _params(sem, vmem_bytes):
    limit = int(min(max(vmem_bytes * 5 // 4 + (4 << 20), 32 << 20), VMEM_BUDGET))
    return pltpu.CompilerParams(dimension_semantics=sem, vmem_limit_bytes=limit)


def _ada_kernel(c_ref, w_ref, b_ref, o_ref):
    s = jax.nn.silu(c_ref[...]).astype(BF16)
    w = w_ref[...].astype(BF16)
    o_ref[...] = jnp.dot(s, w, preferred_element_type=F32) + b_ref[...]


def _adaln(c_all, w_ada, b_ada):
    depth, d, n = w_ada.shape
    rows = c_all.shape[0]
    tn = _pick(n, 512, V7X_LANES)
    vmem = 2 * d * tn * 4 + d * tn * 2 + 4 * rows * (d + tn) * 4
    return pl.pallas_call(
        _ada_kernel,
        grid=(depth, n // tn),
        in_specs=[
            pl.BlockSpec((rows, d), lambda l, j: (0, 0)),
            pl.BlockSpec((None, d, tn), lambda l, j: (l, 0, j)),
            pl.BlockSpec((None, 1, tn), lambda l, j: (l, 0, j)),
        ],
        out_specs=pl.BlockSpec((None, rows, tn), lambda l, j: (l, 0, j)),
        out_shape=jax.ShapeDtypeStruct((depth, rows, n), F32),
        compiler_params=_params(("arbitrary", "arbitrary"), vmem),
        name="adaln",
    )(c_all, w_ada, b_ada.reshape(depth, 1, n))


def _norm_kernel(x_ref, g_ref, *rest, modulate):
    if modulate:
        sc_ref, sh_ref, o_ref = rest
    else:
        (o_ref,) = rest
    x = x_ref[...]
    y = x * lax.rsqrt(jnp.mean(x * x, axis=-1, keepdims=True) + EPS)
    y = y * g_ref[...]
    if modulate:
        y = y * (1.0 + sc_ref[...]) + sh_ref[...]
    o_ref[...] = y.astype(o_ref.dtype)


def _row_param_spec(p, tr, tn, col_blocked):
    shared = p.shape[1] == 1
    rows = 1 if shared else tr
    if col_blocked:
        if shared:
            return pl.BlockSpec((None, rows, tn), lambda b, i, j: (b, 0, j))
        return pl.BlockSpec((None, rows, tn), lambda b, i, j: (b, i, j))
    if shared:
        return pl.BlockSpec((None, rows, tn), lambda b, i: (b, 0, 0))
    return pl.BlockSpec((None, rows, tn), lambda b, i: (b, i, 0))


def _rmsnorm(x3, gamma, scale=None, shift=None, out_dtype=BF16):
    nblk, rb, d = x3.shape
    tr = _pick(rb, 512, V7X_SUBLANES)
    modulate = scale is not None
    in_specs = [
        pl.BlockSpec((None, tr, d), lambda b, i: (b, i, 0)),
        pl.BlockSpec((1, d), lambda b, i: (0, 0)),
    ]
    args = [x3, gamma.reshape(1, d)]
    if modulate:
        in_specs += [_row_param_spec(scale, tr, d, False), _row_param_spec(shift, tr, d, False)]
        args += [scale, shift]
    vmem = 2 * tr * d * (4 + 4) + 6 * tr * d * 4
    return pl.pallas_call(
        functools.partial(_norm_kernel, modulate=modulate),
        grid=(nblk, rb // tr),
        in_specs=in_specs,
        out_specs=pl.BlockSpec((None, tr, d), lambda b, i: (b, i, 0)),
        out_shape=jax.ShapeDtypeStruct((nblk, rb, d), out_dtype),
        compiler_params=_params(("arbitrary", "arbitrary"), vmem),
        name="rmsnorm_mod",
    )(*args)


def _mm_plain_kernel(a_ref, w_ref, o_ref):
    o_ref[...] = jnp.dot(a_ref[...], w_ref[...], preferred_element_type=F32).astype(o_ref.dtype)


def _w_spec(k, tn, layer, col_block0=0):
    return pl.BlockSpec((None, k, tn), lambda b, i, j: (layer, 0, j + col_block0))


def _mm_plain_round_kernel(a_ref, w_ref, o_ref, wbf_ref):
    wb = w_ref[...].astype(BF16)
    wbf_ref[...] = wb
    o_ref[...] = jnp.dot(a_ref[...], wb, preferred_element_type=F32).astype(o_ref.dtype)


def _mm_plain(a3, w, layer, col0, ncols, out_dtype, tm_pref=1024, tn_pref=1024, round_weights=False):
    nblk, rb, k = a3.shape
    tm = _pick(rb, tm_pref, V7X_SUBLANES)
    tn = _pick(math.gcd(ncols, col0) if col0 else ncols, 512 if round_weights else tn_pref, V7X_LANES)
    cb = col0 // tn
    osz = jnp.dtype(out_dtype).itemsize
    wsz = 4 if round_weights else 2
    vmem = 2 * (tm * k * 2 + k * tn * (wsz + 2) + tm * tn * osz) + tm * tn * 4
    out_specs = [pl.BlockSpec((None, tm, tn), lambda b, i, j: (b, i, j))]
    out_shape = [jax.ShapeDtypeStruct((nblk, rb, ncols), out_dtype)]
    if round_weights:
        assert nblk == 1 and tm == rb
        out_specs.append(pl.BlockSpec((k, tn), lambda b, i, j: (0, j)))
        out_shape.append(jax.ShapeDtypeStruct((k, ncols), BF16))
    res = pl.pallas_call(
        _mm_plain_round_kernel if round_weights else _mm_plain_kernel,
        grid=(nblk, rb // tm, ncols // tn),
        in_specs=[
            pl.BlockSpec((None, tm, k), lambda b, i, j: (b, i, 0)),
            _w_spec(k, tn, layer, cb),
        ],
        out_specs=out_specs,
        out_shape=out_shape,
        compiler_params=_params(("arbitrary", "arbitrary", "arbitrary"), vmem),
        name="proj_round" if round_weights else "proj",
    )(a3, w)
    return res if round_weights else res[0]


def _mm_glu_kernel(y_ref, w_ref, yt_ref, o_ref, ybf_ref):
    @pl.when(pl.program_id(2) == 0)
    def _():
        ybf_ref[...] = y_ref[...].astype(BF16)

    acc = jnp.dot(ybf_ref[...], w_ref[...], preferred_element_type=F32)
    o_ref[...] = (yt_ref[...] * jax.nn.sigmoid(acc)).astype(o_ref.dtype)


def _mm_glu(y3, w, layer):
    nblk, rb, k = y3.shape
    n = w.shape[2]
    tm = _pick(rb, 1024, V7X_SUBLANES)
    tn = _pick(n, 512, V7X_LANES)
    vmem = 2 * (tm * k * 4 + k * tn * 2 + tm * tn * 4 + tm * tn * 2) + tm * k * 2 + tm * tn * 4
    return pl.pallas_call(
        _mm_glu_kernel,
        grid=(nblk, rb // tm, n // tn),
        in_specs=[
            pl.BlockSpec((None, tm, k), lambda b, i, j: (b, i, 0)),
            _w_spec(k, tn, layer),
            pl.BlockSpec((None, tm, tn), lambda b, i, j: (b, i, j)),
        ],
        out_specs=pl.BlockSpec((None, tm, tn), lambda b, i, j: (b, i, j)),
        out_shape=jax.ShapeDtypeStruct((nblk, rb, n), BF16),
        scratch_shapes=[pltpu.VMEM((tm, k), BF16)],
        compiler_params=_params(("arbitrary", "arbitrary", "arbitrary"), vmem),
        name="ssm_glu",
    )(y3, w, y3)


def _mm_merge_kernel(a_ref, s_ref, wa_ref, ws_ref, ga_ref, gs_ref, o_ref):
    pa = jnp.dot(a_ref[...], wa_ref[...], preferred_element_type=F32)
    ps = jnp.dot(s_ref[...], ws_ref[...], preferred_element_type=F32)
    o = jax.nn.sigmoid(ga_ref[...]) * pa + jax.nn.sigmoid(gs_ref[...]) * ps
    o_ref[...] = o.astype(o_ref.dtype)


def _mm_merge(attn3, ssm3, w_attn, w_ssm, layer, gates3, d):
    nblk, rb, ka = attn3.shape
    ks = ssm3.shape[2]
    tm = _pick(rb, 1024, V7X_SUBLANES)
    tn = _pick(d, 512, V7X_LANES)
    gs_off = d // tn
    vmem = 2 * (tm * (ka + ks) * 2 + (ka + ks) * tn * 2 + 2 * tm * tn * 4 + tm * tn * 2) + 3 * tm * tn * 4
    return pl.pallas_call(
        _mm_merge_kernel,
        grid=(nblk, rb // tm, d // tn),
        in_specs=[
            pl.BlockSpec((None, tm, ka), lambda b, i, j: (b, i, 0)),
            pl.BlockSpec((None, tm, ks), lambda b, i, j: (b, i, 0)),
            _w_spec(ka, tn, layer),
            _w_spec(ks, tn, layer),
            pl.BlockSpec((None, tm, tn), lambda b, i, j: (b, i, j)),
            pl.BlockSpec((None, tm, tn), lambda b, i, j: (b, i, j + gs_off)),
        ],
        out_specs=pl.BlockSpec((None, tm, tn), lambda b, i, j: (b, i, j)),
        out_shape=jax.ShapeDtypeStruct((nblk, rb, d), BF16),
        compiler_params=_params(("arbitrary", "arbitrary", "arbitrary"), vmem),
        name="mixer_merge",
    )(attn3, ssm3, w_attn, w_ssm, gates3, gates3)


def _mm_resid_kernel(a_ref, w_ref, x_ref, g_ref, o_ref):
    acc = jnp.dot(a_ref[...], w_ref[...], preferred_element_type=F32)
    o_ref[...] = x_ref[...] + g_ref[...] * acc


def _mm_resid_round_kernel(a_ref, w_ref, x_ref, g_ref, o_ref, wbf_ref):
    wb = w_ref[...].astype(BF16)
    wbf_ref[...] = wb
    o_ref[...] = x_ref[...] + g_ref[...] * jnp.dot(a_ref[...], wb, preferred_element_type=F32)


def _mm_resid(a3, w, layer, x3, gate, tm_pref, tn_pref, lhs_buffers=2, round_weights=False):
    nblk, rb, k = a3.shape
    n = w.shape[2]
    tm = _pick(rb, tm_pref, V7X_SUBLANES)
    tn = _pick(n, tn_pref, V7X_LANES)
    wsz = 4 if round_weights else 2
    vmem = lhs_buffers * tm * k * 2 + 2 * (k * tn * (wsz + 2) + 2 * tm * tn * 4) + tm * tn * 4
    out_specs = [pl.BlockSpec((None, tm, tn), lambda b, i, j: (b, i, j))]
    out_shape = [jax.ShapeDtypeStruct((nblk, rb, n), F32)]
    if round_weights:
        assert nblk == 1 and tm == rb
        out_specs.append(pl.BlockSpec((k, tn), lambda b, i, j: (0, j)))
        out_shape.append(jax.ShapeDtypeStruct((k, n), BF16))
    res = pl.pallas_call(
        _mm_resid_round_kernel if round_weights else _mm_resid_kernel,
        grid=(nblk, rb // tm, n // tn),
        in_specs=[
            pl.BlockSpec((None, tm, k), lambda b, i, j: (b, i, 0), pipeline_mode=pl.Buffered(lhs_buffers)),
            _w_spec(k, tn, layer),
            pl.BlockSpec((None, tm, tn), lambda b, i, j: (b, i, j)),
            _row_param_spec(gate, tm, tn, True),
        ],
        out_specs=out_specs,
        out_shape=out_shape,
        compiler_params=_params(("arbitrary", "arbitrary", "arbitrary"), vmem),
        name="proj_residual_round" if round_weights else "proj_residual",
    )(a3, w, x3, gate)
    return res if round_weights else res[0]


def _conv3(up, r1, r2, cw_ref, cb_ref):
    cw = cw_ref[...]
    return cb_ref[...] + (cw[0:1] * r2 + cw[1:2] * r1 + cw[2:3] * up)


def _ffn_up_fresh_kernel(h_ref, wa_ref, wg_ref, cwa_ref, cwg_ref, cba_ref, cbg_ref,
                         act_ref, sa_ref, sg_ref, wbfa_ref, wbfg_ref, wsc_ref, tail_ref, *, tm):
    tail_rows = V7X_SUBLANES

    @pl.when((pl.program_id(1) == 0) & (pl.program_id(2) == 0))
    def _():
        for p, (w_ref, wbf_ref) in enumerate(((wa_ref, wbfa_ref), (wg_ref, wbfg_ref))):
            wb = w_ref[...].astype(BF16)
            wsc_ref[p] = wb
            wbf_ref[...] = wb

    @pl.when(pl.program_id(2) == 0)
    def _():
        tail_ref[...] = jnp.zeros_like(tail_ref)

    lhs = h_ref[...]
    parts = []
    for p, (cw_ref, cb_ref, s_ref) in enumerate(((cwa_ref, cba_ref, sa_ref), (cwg_ref, cbg_ref, sg_ref))):
        up = jnp.dot(lhs, wsc_ref[p], preferred_element_type=F32)
        ext = jnp.concatenate([tail_ref[p], up], axis=0)
        new_tail = ext[tm:, :]
        tail_ref[p] = new_tail
        s_ref[...] = new_tail
        r1 = pltpu.roll(ext, 1, 0)[tail_rows:]
        r2 = pltpu.roll(ext, 2, 0)[tail_rows:]
        parts.append(_conv3(up, r1, r2, cw_ref, cb_ref))
    act_ref[...] = (jax.nn.silu(parts[1]) * parts[0]).astype(act_ref.dtype)


def _ffn_up_carried_kernel(h_ref, wa_ref, wg_ref, cwa_ref, cwg_ref, cba_ref, cbg_ref,
                           p1a_ref, p2a_ref, p1g_ref, p2g_ref, act_ref, sa_ref, sg_ref, *, seq):
    lhs = h_ref[...]
    parts = []
    for w_ref, cw_ref, cb_ref, p1_ref, p2_ref, s_ref in (
            (wa_ref, cwa_ref, cba_ref, p1a_ref, p2a_ref, sa_ref),
            (wg_ref, cwg_ref, cbg_ref, p1g_ref, p2g_ref, sg_ref)):
        up = jnp.dot(lhs, w_ref[...], preferred_element_type=F32)
        s_ref[...] = up
        tloc = lax.broadcasted_iota(jnp.int32, up.shape, 0) % seq
        r1 = jnp.where(tloc < 1, p1_ref[...], pltpu.roll(up, 1, 0))
        r2 = jnp.where(tloc < 2, p2_ref[...], pltpu.roll(up, 2, 0))
        parts.append(_conv3(up, r1, r2, cw_ref, cb_ref))
    act_ref[...] = (jax.nn.silu(parts[1]) * parts[0]).astype(act_ref.dtype)


def _ffn_up_fresh(h3, w_up, w_conv, b_conv, layer, d_ff):
    nblk, rb, k = h3.shape
    tn = _pick(d_ff, 512, V7X_LANES)
    goff = d_ff // tn
    tm = _pick(rb, 1024, V7X_SUBLANES)
    cb = b_conv.reshape(b_conv.shape[0], 1, 2 * d_ff)
    wspec = lambda rows, off: pl.BlockSpec((None, rows, tn), lambda j, b, i: (layer, 0, j + off))
    tail_spec = pl.BlockSpec((None, None, V7X_SUBLANES, tn), lambda j, b, i: (b, i, 0, j))
    wbf_spec = pl.BlockSpec((k, tn), lambda j, b, i: (0, j))
    vmem = (2 * (tm * k * 2 + 2 * k * tn * 4 + 2 * k * tn * 2 + tm * tn * 2) + 2 * k * tn * 2
            + 10 * (tm + 8) * tn * 4)
    return pl.pallas_call(
        functools.partial(_ffn_up_fresh_kernel, tm=tm),
        grid=(goff, nblk, rb // tm),
        in_specs=[
            pl.BlockSpec((None, tm, k), lambda j, b, i: (b, i, 0)),
            wspec(k, 0), wspec(k, goff),
            wspec(w_conv.shape[1], 0), wspec(w_conv.shape[1], goff),
            wspec(1, 0), wspec(1, goff),
        ],
        out_specs=[
            pl.BlockSpec((None, tm, tn), lambda j, b, i: (b, i, j)),
            tail_spec, tail_spec, wbf_spec, wbf_spec,
        ],
        out_shape=[
            jax.ShapeDtypeStruct((nblk, rb, d_ff), BF16),
            jax.ShapeDtypeStruct((nblk, rb // tm, V7X_SUBLANES, d_ff), F32),
            jax.ShapeDtypeStruct((nblk, rb // tm, V7X_SUBLANES, d_ff), F32),
            jax.ShapeDtypeStruct((k, d_ff), BF16),
            jax.ShapeDtypeStruct((k, d_ff), BF16),
        ],
        scratch_shapes=[pltpu.VMEM((2, k, tn), BF16), pltpu.VMEM((2, V7X_SUBLANES, tn), F32)],
        compiler_params=_params(("arbitrary", "arbitrary", "arbitrary"), vmem),
        name="convffn_up",
    )(h3, w_up, w_up, w_conv, w_conv, cb, cb)


def _ffn_up_carried(h3, w_a, w_g, w_conv, b_conv, layer, d_ff, prev, seq):
    nblk, rb, k = h3.shape
    tn = _pick(d_ff, 512, V7X_LANES)
    goff = d_ff // tn
    cb = b_conv.reshape(b_conv.shape[0], 1, 2 * d_ff)
    p1, p2 = prev
    cspec = lambda rows, off: pl.BlockSpec((None, rows, tn), lambda b, j: (layer, 0, j + off))
    wspec = pl.BlockSpec((k, tn), lambda b, j: (0, j))
    row_spec = lambda off: pl.BlockSpec((None, rb, tn), lambda b, j: (b, 0, j + off))
    vmem = 2 * (rb * k * 2 + 2 * k * tn * 2 + 8 * rb * tn * 4) + 10 * rb * tn * 4
    return pl.pallas_call(
        functools.partial(_ffn_up_carried_kernel, seq=seq),
        grid=(nblk, goff),
        in_specs=[
            pl.BlockSpec((None, rb, k), lambda b, j: (b, 0, 0)),
            wspec, wspec,
            cspec(w_conv.shape[1], 0), cspec(w_conv.shape[1], goff),
            cspec(1, 0), cspec(1, goff),
            row_spec(0), row_spec(0), row_spec(goff), row_spec(goff),
        ],
        out_specs=[row_spec(0), row_spec(0), row_spec(0)],
        out_shape=[
            jax.ShapeDtypeStruct((nblk, rb, d_ff), BF16),
            jax.ShapeDtypeStruct((nblk, rb, d_ff), F32),
            jax.ShapeDtypeStruct((nblk, rb, d_ff), F32),
        ],
        compiler_params=_params(("arbitrary", "arbitrary"), vmem),
        name="convffn_up_carried",
    )(h3, w_a, w_g, w_conv, w_conv, cb, cb, p1, p2, p1, p2)


def _beats(other, gate, other_is_lower):
    return jnp.where(other > gate, 1.0, 0.0) + jnp.where(other == gate, 1.0, 0.0) * other_is_lower


def _attn_fresh_body(cc, q_ref, o_ref, kaug_ref, vbf_ref, kmean_ref, sd_ref, *, nblk):
    blk = MOBA_BLOCK
    scale = HEAD_DIM ** -0.5
    w = (cc + 1) * blk
    q2 = q_ref[...]
    qs = jnp.concatenate([q2[:, :HEAD_DIM], q2[:, HEAD_DIM:]], axis=0)
    if cc > 0:
        gate_t = lax.dot_general(kmean_ref[...], qs, (((1,), (1,)), ((), ())),
                                 precision=lax.Precision.HIGHEST, preferred_element_type=F32)
        riota = lax.broadcasted_iota(jnp.int32, gate_t.shape, 0)
        cnt = jnp.zeros_like(gate_t)
        for m in range(cc):
            cnt = cnt + _beats(gate_t[m:m + 1, :], gate_t, jnp.where(riota > m, 1.0, 0.0))
        keep = jnp.where(cnt < MOBA_TOPK, 1.0, 0.0) + jnp.where(riota >= cc, 1.0, 0.0)
        bias_t = jnp.where(keep > 0.5, 0.0, MASK_BIAS)
        bias_t = jnp.concatenate(
            [bias_t, jnp.zeros((HEAD_DIM - nblk, 2 * blk), F32)], axis=0)
        bias = bias_t.T.astype(BF16)
    else:
        bias = jnp.zeros((2 * blk, HEAD_DIM), BF16)
    q_aug = jnp.concatenate([qs.astype(BF16), bias], axis=1)
    s = lax.dot_general(q_aug, kaug_ref[0:w, :], (((1,), (1,)), ((), ())),
                        preferred_element_type=F32)
    row = lax.broadcasted_iota(jnp.int32, (blk, blk), 0)
    col = lax.broadcasted_iota(jnp.int32, (blk, blk), 1)
    causal = row >= col
    for hh in range(2):
        sh = s[hh * blk:(hh + 1) * blk, :] * (scale * LOG2_E) - sd_ref[hh, :, 0:w]
        diag = jnp.where(causal, sh[:, cc * blk:], NEG_INF)
        sh = jnp.concatenate([sh[:, :cc * blk], diag], axis=1) if cc > 0 else diag
        m = jnp.max(sh, axis=-1, keepdims=True)
        p = jnp.exp2(sh - m)
        l = jnp.sum(p, axis=-1, keepdims=True)
        o = jnp.dot(p.astype(BF16), vbf_ref[0:w, :], preferred_element_type=F32) / l
        o_ref[:, hh * HEAD_DIM:(hh + 1) * HEAD_DIM] = o.astype(o_ref.dtype)


def _attn_fresh_kernel(slope_ref, q_ref, k_ref, v_ref, o_ref,
                       kaug_ref, vbf_ref, kmean_ref, sd_ref, *, nblk):
    g = pl.program_id(1)
    c = pl.program_id(2)
    blk = MOBA_BLOCK
    t = nblk * blk

    @pl.when(c == 0)
    def _():
        k = k_ref[...]
        kaug_ref[:, 0:HEAD_DIM] = k.astype(BF16)
        key_blk = lax.broadcasted_iota(jnp.int32, (t, HEAD_DIM), 0) // blk
        lane = lax.broadcasted_iota(jnp.int32, (t, HEAD_DIM), 1)
        kaug_ref[:, HEAD_DIM:] = jnp.where(key_blk == lane, 1.0, 0.0).astype(BF16)
        vbf_ref[...] = v_ref[...].astype(BF16)
        for n in range(nblk):
            kmean_ref[n:n + 1, :] = jnp.mean(k[n * blk:(n + 1) * blk, :], axis=0, keepdims=True)
        d0 = (lax.broadcasted_iota(jnp.int32, (blk, t), 0)
              - lax.broadcasted_iota(jnp.int32, (blk, t), 1)).astype(F32)
        for hh in range(2):
            sd_ref[hh] = (slope_ref[2 * g + hh] * LOG2_E) * d0

    for cc in range(nblk):
        pl.when(c == cc)(functools.partial(
            _attn_fresh_body, cc, q_ref, o_ref, kaug_ref, vbf_ref, kmean_ref, sd_ref, nblk=nblk))


def _attn_fresh(q3, k3, v3, slopes):
    b, t, q_w = q3.shape
    kv_heads = k3.shape[2] // HEAD_DIM
    assert q_w == 2 * kv_heads * HEAD_DIM and t % MOBA_BLOCK == 0
    nblk = t // MOBA_BLOCK
    assert nblk <= HEAD_DIM
    blk = MOBA_BLOCK
    vmem = (2 * (blk * 2 * HEAD_DIM * 4 + 2 * t * HEAD_DIM * 4 + blk * 2 * HEAD_DIM * 2)
            + 3 * t * HEAD_DIM * 2 + 2 * blk * t * 4 + 8 * 2 * blk * t * 4)
    return pl.pallas_call(
        functools.partial(_attn_fresh_kernel, nblk=nblk),
        grid=(b, kv_heads, nblk),
        in_specs=[
            pl.BlockSpec(memory_space=pltpu.SMEM),
            pl.BlockSpec((None, blk, 2 * HEAD_DIM), lambda bb, g, c: (bb, c, g)),
            pl.BlockSpec((None, t, HEAD_DIM), lambda bb, g, c: (bb, 0, g)),
            pl.BlockSpec((None, t, HEAD_DIM), lambda bb, g, c: (bb, 0, g)),
        ],
        out_specs=pl.BlockSpec((None, blk, 2 * HEAD_DIM), lambda bb, g, c: (bb, c, g)),
        out_shape=jax.ShapeDtypeStruct((b, t, q_w), BF16),
        scratch_shapes=[
            pltpu.VMEM((t, 2 * HEAD_DIM), BF16),
            pltpu.VMEM((t, HEAD_DIM), BF16),
            pltpu.VMEM((nblk, HEAD_DIM), F32),
            pltpu.VMEM((2, blk, t), F32),
        ],
        compiler_params=_params(("arbitrary", "arbitrary", "arbitrary"), vmem),
        name="moba_fresh",
    )(slopes, q3, k3, v3)


def _page_heads_on_lanes(p_ref, page, kvh):
    return jnp.concatenate([p_ref[pl.ds(g, page, stride=kvh), :] for g in range(kvh)], axis=-1)


def _page_specs(layer, npg, page_rows, hd):
    def spec(j):
        return pl.BlockSpec((None, None, page_rows, hd),
                            lambda i, n, pt: (layer, pt[i, n * npg + j], 0, 0))
    return [spec(j) for j in range(npg)]


def _paged_scores_kernel(pt_ref, wq_ref, *refs, page, kvh, npg):
    pages = refs[:npg]
    sc_ref, km_ref = refs[npg:]
    wq = wq_ref[...]
    hd = pages[0].shape[1]
    sums = []
    for j, p_ref in enumerate(pages):
        kcat = _page_heads_on_lanes(p_ref, page, kvh)
        sc_ref[j * page:(j + 1) * page, :] = jnp.dot(kcat.astype(BF16), wq, preferred_element_type=F32)
        sums.append(jnp.sum(p_ref[...].reshape(page, kvh, hd), axis=0))
    for b in range(npg // 2):
        km_ref[b] = (sums[2 * b] + sums[2 * b + 1]) * (1.0 / (2 * page))


def _paged_scores(page_table, wq_bf, cache, layer, npg):
    s, n_pages = page_table.shape
    depth, n_pool, page, kvh, hd = cache.shape
    assert 2 * page == MOBA_BLOCK and n_pages % npg == 0 and npg % 2 == 0
    kv_w = kvh * hd
    cache_rows = cache.reshape(depth, n_pool, page * kvh, hd)
    nblk = n_pages // 2
    past = n_pages * page
    vmem = 2 * (kv_w * V7X_LANES * 2 + npg * page * kv_w * 4 + npg * page * V7X_LANES * 4) + 6 * page * kv_w * 4
    grid_spec = pltpu.PrefetchScalarGridSpec(
        num_scalar_prefetch=1,
        grid=(s, n_pages // npg),
        in_specs=[pl.BlockSpec((None, kv_w, V7X_LANES), lambda i, n, pt: (i, 0, 0))]
        + _page_specs(layer, npg, page * kvh, hd),
        out_specs=[
            pl.BlockSpec((None, npg * page, V7X_LANES), lambda i, n, pt: (i, n, 0)),
            pl.BlockSpec((None, npg // 2, kvh, hd), lambda i, n, pt: (i, n, 0, 0)),
        ],
    )
    return pl.pallas_call(
        functools.partial(_paged_scores_kernel, page=page, kvh=kvh, npg=npg),
        grid_spec=grid_spec,
        out_shape=[
            jax.ShapeDtypeStruct((s, past, V7X_LANES), F32),
            jax.ShapeDtypeStruct((s, nblk, kvh, hd), F32),
        ],
        compiler_params=_params(("arbitrary", "arbitrary"), vmem),
        name="moba_paged_scores",
    )(page_table, wq_bf, *([cache_rows] * npg))


def _paged_attend_kernel(pt_ref, wq_ref, wqbf_ref, sc_ref, km_ref, knew_ref, vnew_ref,
                         slope_ref, tcol_ref, *refs, page, kvh, npg, nblk, tnew, past):
    pages = refs[:npg]
    o_ref, sel_ref, prob_ref, linv_ref, sq_ref, vpad_ref, acc_ref = refs[npg:]
    n = pl.program_id(1)
    blk = MOBA_BLOCK
    scale = HEAD_DIM ** -0.5
    lanes = V7X_LANES
    slope = slope_ref[...]
    tcol = tcol_ref[...]
    koff = lax.broadcasted_iota(jnp.int32, (blk, lanes), 0).astype(F32)

    def block_scores(i):
        r0 = pl.multiple_of(i * blk, blk)
        raw = sc_ref[pl.ds(r0, blk), :]
        dist = (tcol + lax.convert_element_type(past - i * blk, F32)) - koff
        s = raw * scale - slope * dist
        return jnp.where(sel_ref[pl.ds(i, 1), :] > 0.5, s, NEG_INF)

    @pl.when(n == 0)
    def _():
        gate = jnp.zeros((nblk, lanes), F32)
        for g in range(kvh):
            gate = gate + jnp.dot(km_ref[pl.ds(g, nblk, stride=kvh), :],
                                  wq_ref[g * HEAD_DIM:(g + 1) * HEAD_DIM, :],
                                  precision=lax.Precision.HIGHEST, preferred_element_type=F32)
        riota = lax.broadcasted_iota(jnp.int32, (nblk, lanes), 0)
        cnt = jnp.zeros((nblk, lanes), F32)
        for m in range(nblk):
            cnt = cnt + _beats(gate[m:m + 1, :], gate, jnp.where(riota > m, 1.0, 0.0))
        sel_ref[...] = jnp.where(cnt < MOBA_TOPK, 1.0, 0.0)

        s_cur = jnp.dot(knew_ref[...].astype(BF16), wqbf_ref[...], preferred_element_type=F32)
        off = lax.broadcasted_iota(jnp.int32, (tnew, lanes), 0).astype(F32)
        s_cur = s_cur * scale - slope * (tcol - off)
        s_cur = jnp.where(off <= tcol, s_cur, NEG_INF)
        m0 = jnp.max(s_cur, axis=0, keepdims=True)

        def max_body(i, m):
            return jnp.maximum(m, jnp.max(block_scores(i), axis=0, keepdims=True))

        mx = lax.fori_loop(0, nblk, max_body, m0)
        p_cur = jnp.exp(s_cur - mx)

        def sum_body(i, l):
            p = jnp.exp(block_scores(i) - mx)
            prob_ref[pl.ds(pl.multiple_of(i * blk, blk), blk), :] = p
            return l + jnp.sum(p, axis=0, keepdims=True)

        l = lax.fori_loop(0, nblk, sum_body, jnp.sum(p_cur, axis=0, keepdims=True))
        linv = 1.0 / l
        linv_ref[...] = linv
        sq_ref[...] = jnp.zeros_like(sq_ref)
        sq_ref[0:tnew, :] = p_cur * linv
        vpad_ref[...] = jnp.zeros_like(vpad_ref)
        vpad_ref[0:tnew, :] = vnew_ref[...].astype(BF16)
        acc_ref[...] = jnp.dot(sq_ref[...].T.astype(BF16), vpad_ref[...], preferred_element_type=F32)

    linv = linv_ref[...]
    acc = acc_ref[...]
    for j, p_ref in enumerate(pages):
        r0 = pl.multiple_of((n * npg + j) * page, page)
        pt_j = (prob_ref[pl.ds(r0, page), :] * linv).T.astype(BF16)
        vcat = _page_heads_on_lanes(p_ref, page, kvh).astype(BF16)
        acc = acc + jnp.dot(pt_j, vcat, preferred_element_type=F32)
    acc_ref[...] = acc

    @pl.when(n == pl.num_programs(1) - 1)
    def _():
        rows = o_ref.shape[1]
        for g in range(o_ref.shape[0]):
            o_ref[g] = acc_ref[g * rows:(g + 1) * rows, g * HEAD_DIM:(g + 1) * HEAD_DIM].astype(o_ref.dtype)


def _paged_attend(page_table, wq, wq_bf, scores, kmean, k_new, v_new, slope_col, t_col, cache, layer, npg):
    s, n_pages = page_table.shape
    depth, n_pool, page, kvh, hd = cache.shape
    kv_w = kvh * hd
    cache_rows = cache.reshape(depth, n_pool, page * kvh, hd)
    nblk = n_pages // 2
    past = n_pages * page
    tnew = k_new.shape[1]
    rows = 2 * tnew
    kmean_rows = kmean.reshape(s, nblk * kvh, hd)
    vmem = (2 * (kv_w * V7X_LANES * 6 + past * V7X_LANES * 4 + nblk * kv_w * 4 + npg * page * kv_w * 4)
            + past * V7X_LANES * 4 + 3 * V7X_LANES * kv_w * 4 + 16 * MOBA_BLOCK * V7X_LANES * 4
            + 4 * page * kv_w * 4)
    grid_spec = pltpu.PrefetchScalarGridSpec(
        num_scalar_prefetch=1,
        grid=(s, n_pages // npg),
        in_specs=[
            pl.BlockSpec((None, kv_w, V7X_LANES), lambda i, n, pt: (i, 0, 0)),
            pl.BlockSpec((None, kv_w, V7X_LANES), lambda i, n, pt: (i, 0, 0)),
            pl.BlockSpec((None, past, V7X_LANES), lambda i, n, pt: (i, 0, 0)),
            pl.BlockSpec((None, nblk * kvh, hd), lambda i, n, pt: (i, 0, 0)),
            pl.BlockSpec((None, tnew, kv_w), lambda i, n, pt: (i, 0, 0)),
            pl.BlockSpec((None, tnew, kv_w), lambda i, n, pt: (i, 0, 0)),
            pl.BlockSpec((1, V7X_LANES), lambda i, n, pt: (0, 0)),
            pl.BlockSpec((1, V7X_LANES), lambda i, n, pt: (0, 0)),
        ] + _page_specs(layer, npg, page * kvh, hd),
        out_specs=pl.BlockSpec((None, kvh, rows, hd), lambda i, n, pt: (i, 0, 0, 0)),
        scratch_shapes=[
            pltpu.VMEM((nblk, V7X_LANES), F32),
            pltpu.VMEM((past, V7X_LANES), F32),
            pltpu.VMEM((1, V7X_LANES), F32),
            pltpu.VMEM((V7X_LANES, V7X_LANES), F32),
            pltpu.VMEM((V7X_LANES, kv_w), BF16),
            pltpu.VMEM((V7X_LANES, kv_w), F32),
        ],
    )
    return pl.pallas_call(
        functools.partial(_paged_attend_kernel, page=page, kvh=kvh, npg=npg, nblk=nblk, tnew=tnew, past=past),
        grid_spec=grid_spec,
        out_shape=jax.ShapeDtypeStruct((s, kvh, rows, hd), BF16),
        compiler_params=_params(("arbitrary", "arbitrary"), vmem),
        name="moba_paged_attend",
    )(page_table, wq, wq_bf, scores, kmean_rows, k_new, v_new, slope_col, t_col, *([cache_rows] * npg))


def _ssm_prep_kernel(lr_ref, li_ref, ldt_ref, brt_ref, bit_ref, are_ref, aim_ref, bbr_ref, bbi_ref):
    lr = lr_ref[...]
    li = li_ref[...]
    dt = jnp.exp(ldt_ref[...])
    mag = jnp.exp(lr * dt)
    a_re = mag * jnp.cos(li * dt)
    a_im = mag * jnp.sin(li * dt)
    den = lr * lr + li * li
    q_re = ((a_re - 1.0) * lr + a_im * li) / den
    q_im = (a_im * lr - (a_re - 1.0) * li) / den
    are_ref[...] = a_re
    aim_ref[...] = a_im
    br = brt_ref[...]
    bi = bit_ref[...]
    bbr_ref[...] = q_re * br - q_im * bi
    bbi_ref[...] = q_re * bi + q_im * br


def _ssm_prep(lam_re, lam_im, log_dt, b_re, b_im):
    depth, g, p = lam_re.shape
    i = b_re.shape[3]
    brt = jnp.swapaxes(b_re, 2, 3)
    bit = jnp.swapaxes(b_im, 2, 3)
    vec = pl.BlockSpec((None, g, 1, p), lambda l: (l, 0, 0, 0))
    mat = pl.BlockSpec((None, g, i, p), lambda l: (l, 0, 0, 0))
    return pl.pallas_call(
        _ssm_prep_kernel,
        grid=(depth,),
        in_specs=[vec, vec, pl.BlockSpec((None, g, 1, 1), lambda l: (l, 0, 0, 0)), mat, mat],
        out_specs=[vec, vec, mat, mat],
        out_shape=[
            jax.ShapeDtypeStruct((depth, g, 1, p), F32),
            jax.ShapeDtypeStruct((depth, g, 1, p), F32),
            jax.ShapeDtypeStruct((depth, g, i, p), F32),
            jax.ShapeDtypeStruct((depth, g, i, p), F32),
        ],
        compiler_params=_params(("arbitrary",), 32 << 20),
        name="s5_discretise",
    )(lam_re.reshape(depth, g, 1, p), lam_im.reshape(depth, g, 1, p),
      log_dt.reshape(depth, g, 1, 1), brt, bit)


def _ssm_scan_kernel(u_ref, ire_ref, iim_ref, are_ref, aim_ref, bre_ref, bim_ref,
                     cre_ref, cim_ref, d_ref, *rest, with_y, n_chunks, tau_b):
    if with_y:
        y_ref, ere_ref, eim_ref, hre_ref, him_ref, bure_ref, buim_ref = rest
    else:
        ere_ref, eim_ref, hre_ref, him_ref, bure_ref, buim_ref = rest
    k = pl.program_id(1)

    @pl.when(k == 0)
    def _():
        hre_ref[...] = ire_ref[...]
        him_ref[...] = iim_ref[...]

    def chunk(c, carry):
        cu = pl.multiple_of(c * CHUNK_IN, CHUNK_IN)
        cs = pl.multiple_of(c * CHUNK_ST, CHUNK_ST)
        u_c = u_ref[:, pl.ds(cu, CHUNK_IN)]
        ub = u_c.astype(BF16)
        bure_ref[...] = jnp.dot(ub, bre_ref[c], preferred_element_type=F32)
        buim_ref[...] = jnp.dot(ub, bim_ref[c], preferred_element_type=F32)
        ar = jnp.broadcast_to(are_ref[:, pl.ds(cs, CHUNK_ST)], (N_SEG, CHUNK_ST))
        ai = jnp.broadcast_to(aim_ref[:, pl.ds(cs, CHUNK_ST)], (N_SEG, CHUNK_ST))
        hr = hre_ref[:, pl.ds(cs, CHUNK_ST)]
        hi = him_ref[:, pl.ds(cs, CHUNK_ST)]
        for t in range(tau_b):
            rs = slice(t * N_SEG, (t + 1) * N_SEG)
            nr = ar * hr - ai * hi + bure_ref[rs, :]
            ni = ar * hi + ai * hr + buim_ref[rs, :]
            hr, hi = nr, ni
            if with_y:
                bure_ref[rs, :] = hr
                buim_ref[rs, :] = hi
        hre_ref[:, pl.ds(cs, CHUNK_ST)] = hr
        him_ref[:, pl.ds(cs, CHUNK_ST)] = hi
        if with_y:
            ych = (jnp.dot(bure_ref[...].astype(BF16), cre_ref[c], preferred_element_type=F32)
                   - jnp.dot(buim_ref[...].astype(BF16), cim_ref[c], preferred_element_type=F32))
            y = ych + d_ref[:, pl.ds(cu, CHUNK_IN)] * u_c
            y_ref[:, pl.ds(cu, CHUNK_IN)] = jax.nn.gelu(y)
        return carry

    lax.fori_loop(0, n_chunks, chunk, 0)

    @pl.when(k == pl.num_programs(1) - 1)
    def _():
        ere_ref[...] = hre_ref[...]
        eim_ref[...] = him_ref[...]


def _ssm_scan(u3, init_re, init_im, a_re, a_im, bre, bim, cre, cim, d_skip, with_y):
    nb, lr, s_w = u3.shape
    ns = a_re.shape[1]
    n_chunks = bre.shape[0]
    steps = lr // N_SEG
    tau_b = _pick(steps, 32, 1)
    rb = tau_b * N_SEG
    st_spec = pl.BlockSpec((None, N_SEG, ns), lambda b, k: (b, 0, 0))
    full = lambda shape: pl.BlockSpec(shape, lambda b, k: (0,) * len(shape))
    u_spec = pl.BlockSpec((None, rb, s_w), lambda b, k: (b, k, 0))
    out_specs = [st_spec, st_spec]
    out_shape = [jax.ShapeDtypeStruct((nb, N_SEG, ns), F32)] * 2
    if with_y:
        out_specs = [u_spec] + out_specs
        out_shape = [jax.ShapeDtypeStruct((nb, lr, s_w), F32)] + out_shape
    vmem = (4 * rb * s_w * 4 + 8 * N_SEG * ns * 4 + 4 * ns * 4
            + 2 * 4 * n_chunks * CHUNK_IN * CHUNK_ST * 2 + 8 * rb * CHUNK_ST * 4)
    return pl.pallas_call(
        functools.partial(_ssm_scan_kernel, with_y=with_y, n_chunks=n_chunks, tau_b=tau_b),
        grid=(nb, steps // tau_b),
        in_specs=[u_spec, st_spec, st_spec, full((1, ns)), full((1, ns)),
                  full(bre.shape), full(bim.shape), full(cre.shape), full(cim.shape),
                  full((1, s_w))],
        out_specs=out_specs,
        out_shape=out_shape,
        scratch_shapes=[
            pltpu.VMEM((N_SEG, ns), F32),
            pltpu.VMEM((N_SEG, ns), F32),
            pltpu.VMEM((rb, CHUNK_ST), F32),
            pltpu.VMEM((rb, CHUNK_ST), F32),
        ],
        compiler_params=_params(("arbitrary", "arbitrary"), vmem),
        name="s5_scan" if with_y else "s5_segment_ends",
    )(u3, init_re, init_im, a_re, a_im, bre, bim, cre, cim, d_skip)


def _ssm_carry_kernel(ere_ref, eim_ref, are_ref, aim_ref, ire_ref, iim_ref, fre_ref, fim_ref, *, seg_len):
    br = are_ref[...]
    bi = aim_ref[...]
    pr = jnp.ones_like(br)
    pi = jnp.zeros_like(br)
    e = seg_len
    while e:
        if e & 1:
            pr, pi = pr * br - pi * bi, pr * bi + pi * br
        br, bi = br * br - bi * bi, 2.0 * br * bi
        e >>= 1
    hr = jnp.zeros_like(pr)
    hi = jnp.zeros_like(pr)
    for j in range(N_SEG):
        ire_ref[j:j + 1, :] = hr
        iim_ref[j:j + 1, :] = hi
        er = ere_ref[j:j + 1, :]
        ei = eim_ref[j:j + 1, :]
        hr, hi = pr * hr - pi * hi + er, pr * hi + pi * hr + ei
    fre_ref[...] = hr
    fim_ref[...] = hi


def _ssm_carry(end_re, end_im, a_re, a_im, seg_len):
    nb, _, ns = end_re.shape
    st = pl.BlockSpec((None, N_SEG, ns), lambda b: (b, 0, 0))
    vec = pl.BlockSpec((1, ns), lambda b: (0, 0))
    fin = pl.BlockSpec((None, 1, ns), lambda b: (b, 0, 0))
    return pl.pallas_call(
        functools.partial(_ssm_carry_kernel, seg_len=seg_len),
        grid=(nb,),
        in_specs=[st, st, vec, vec],
        out_specs=[st, st, fin, fin],
        out_shape=[jax.ShapeDtypeStruct((nb, N_SEG, ns), F32)] * 2
        + [jax.ShapeDtypeStruct((nb, 1, ns), F32)] * 2,
        compiler_params=_params(("arbitrary",), 32 << 20),
        name="s5_segment_carry",
    )(end_re, end_im, a_re, a_im)


def _block_diag_in(bbt):
    g, i, p = bbt.shape
    nc = g // GROUPS_PER_CHUNK
    eye = jnp.eye(GROUPS_PER_CHUNK, dtype=bbt.dtype)
    x = bbt.reshape(nc, GROUPS_PER_CHUNK, i, p)
    out = x[:, :, :, None, :] * eye[None, :, None, :, None]
    return out.reshape(nc, GROUPS_PER_CHUNK * i, GROUPS_PER_CHUNK * p).astype(BF16)


def _block_diag_out(c):
    g, i, p = c.shape
    nc = g // GROUPS_PER_CHUNK
    eye = jnp.eye(GROUPS_PER_CHUNK, dtype=c.dtype)
    x = jnp.swapaxes(c.reshape(nc, GROUPS_PER_CHUNK, i, p), 2, 3)
    out = x[:, :, :, None, :] * eye[None, :, None, :, None]
    return out.reshape(nc, GROUPS_PER_CHUNK * p, GROUPS_PER_CHUNK * i).astype(BF16)


def kernel(x_prompt, x_sample, c_prompt, c_sample, cache_k, cache_v, state_ssm_re, state_ssm_im,
           state_conv, page_table, w_ada, b_ada, norm_attn, w_in, w_attn_proj, lam_re, lam_im,
           log_dt, ssm_b_re, ssm_b_im, ssm_c_re, ssm_c_im, ssm_d, w_glu, w_ssm_proj, w_out,
           norm_ffn, w_up, w_conv, b_conv, w_down, norm_final):
    depth = w_ada.shape[0]
    nb, t, d = x_prompt.shape
    ns_seq, ts, _ = x_sample.shape
    q_w = w_attn_proj.shape[1]
    n_heads = q_w // HEAD_DIM
    kv_heads = cache_k.shape[3]
    kv_w = kv_heads * HEAD_DIM
    s_w = ssm_d.shape[1]
    n_groups = lam_re.shape[1]
    n_state = n_groups * S5_STATE
    d_ff = w_down.shape[1]
    n_pages = page_table.shape[1]
    page = cache_k.shape[2]
    past = n_pages * page
    assert ns_seq == N_SEG and t % (N_SEG * N_SEG) == 0 and n_groups % GROUPS_PER_CHUNK == 0
    n_col = kv_heads * 2 * ts
    assert n_col <= V7X_LANES and n_heads == 2 * kv_heads
    assert ssm_b_re.shape[2:] == (S5_STATE, S5_GROUP)
    u_col = q_w + 2 * kv_w
    g_col = u_col + s_w

    n_c = nb + ns_seq
    c_rows = -(-n_c // V7X_SUBLANES) * V7X_SUBLANES
    c_all = jnp.concatenate([c_prompt, c_sample, jnp.zeros((c_rows - n_c, d), F32)], axis=0)
    mod = _adaln(c_all, w_ada, b_ada)

    a_re_all, a_im_all, bbr_all, bbi_all = _ssm_prep(lam_re, lam_im, log_dt, ssm_b_re, ssm_b_im)
    slopes = 2.0 ** (-8.0 * jnp.arange(1, n_heads + 1, dtype=F32) / n_heads)

    col = jnp.arange(V7X_LANES)
    col_head = jnp.minimum(2 * (col // (2 * ts)) + (col // ts) % 2, n_heads - 1)
    slope_col = slopes[col_head].reshape(1, V7X_LANES)
    t_col = (col % ts).astype(F32).reshape(1, V7X_LANES)

    xp = x_prompt
    xs = x_sample.reshape(1, ns_seq * ts, d)
    seg = t // N_SEG
    keep = w_conv.shape[1] - 1
    assert keep == 2 and past % MOBA_BLOCK == 0 and ts <= MOBA_BLOCK
    npg = _pick(n_pages, PAGES_PER_STEP, 2)
    outs ={k: [] for k in ("kp", "vp", "ks", "vs", "hpr", "hpi", "hsr", "hsi", "cp", "cs")}

    w_attn_bf = w_attn_proj.astype(BF16)
    w_glu_bf = w_glu.astype(BF16)
    w_ssm_bf = w_ssm_proj.astype(BF16)
    w_out_bf = w_out.astype(BF16)
    in_cols = ((0, q_w), (q_w, kv_w), (q_w + kv_w, kv_w), (u_col, s_w), (g_col, 2 * d))

    for l in range(depth):
        a_re = a_re_all[l].reshape(1, n_state)
        a_im = a_im_all[l].reshape(1, n_state)
        bre = _block_diag_in(bbr_all[l])
        bim = _block_diag_in(bbi_all[l])
        cre = _block_diag_out(ssm_c_re[l])
        cim = _block_diag_out(ssm_c_im[l])
        d_skip = ssm_d[l].reshape(1, s_w)
        mod_l = mod[l].reshape(c_rows, 6, d)

        def mods_for(lo, n, rep):
            m = mod_l[lo:lo + n]
            if rep == 1:
                return [m[:, i].reshape(n, 1, d) for i in range(6)]
            return [jnp.repeat(m[:, i], rep, axis=0).reshape(1, n * rep, d) for i in range(6)]

        s_sh1, s_sc1, s_g1, s_sh2, s_sc2, s_g2 = mods_for(nb, ns_seq, ts)
        h_s = _rmsnorm(xs, norm_attn[l], s_sc1, s_sh1)
        proj_s = [_mm_plain(h_s, w_in, l, c0, nc, F32, round_weights=True) for c0, nc in in_cols]
        w_in_bf = [p[1][None] for p in proj_s]

        sh1, sc1, g1, sh2, sc2, g2 = mods_for(0, nb, 1)
        h = _rmsnorm(xp, norm_attn[l], sc1, sh1)
        q, k, v, u, gates = [_mm_plain(h, wb, 0, 0, nc, F32) for wb, (_, nc) in zip(w_in_bf, in_cols)]
        attn = _attn_fresh(q, k, v, slopes)
        u_perm = u.reshape(nb, N_SEG, seg, s_w).swapaxes(1, 2).reshape(nb, t, s_w)
        zero_st = jnp.zeros((nb, N_SEG, n_state), F32)
        end_re, end_im = _ssm_scan(u_perm, zero_st, zero_st, a_re, a_im, bre, bim, cre, cim, d_skip, False)
        ini_re, ini_im, fin_re, fin_im = _ssm_carry(end_re, end_im, a_re, a_im, seg)
        y_perm, _, _ = _ssm_scan(u_perm, ini_re, ini_im, a_re, a_im, bre, bim, cre, cim, d_skip, True)
        y = y_perm.reshape(nb, seg, N_SEG, s_w).swapaxes(1, 2).reshape(nb, t, s_w)
        ssm = _mm_glu(y, w_glu_bf, l)
        merged = _mm_merge(attn, ssm, w_attn_bf, w_ssm_bf, l, gates, d)
        xp = _mm_resid(merged, w_out_bf, l, xp, g1, 1024, 512)
        h2 = _rmsnorm(xp, norm_ffn[l], sc2, sh2)
        act_p, tail_a, tail_g, w_up_a_bf, w_up_g_bf = _ffn_up_fresh(h2, w_up, w_conv, b_conv, l, d_ff)
        outs["kp"].append(k.reshape(nb, t, kv_heads, HEAD_DIM))
        outs["vp"].append(v.reshape(nb, t, kv_heads, HEAD_DIM))
        outs["hpr"].append(fin_re.reshape(nb, n_groups, S5_STATE))
        outs["hpi"].append(fin_im.reshape(nb, n_groups, S5_STATE))
        keep = w_conv.shape[1] - 1
        outs["cp"].append(jnp.concatenate([tail_a[:, -1], tail_g[:, -1]], axis=-1)[:, V7X_SUBLANES - keep:, :])

        q, k, v, u, gates = [p[0] for p in proj_s]
        q_s = q.reshape(ns_seq, ts, kv_heads, 2, HEAD_DIM)
        k_new = k.reshape(ns_seq, ts, kv_w)
        v_new = v.reshape(ns_seq, ts, kv_w)
        eye = jnp.eye(kv_heads, dtype=F32)
        wq = (jnp.transpose(q_s, (0, 2, 4, 3, 1))[:, :, :, None, :, :]
              * eye[None, :, None, :, None, None]).reshape(ns_seq, kv_w, n_col)
        wq = jnp.pad(wq, ((0, 0), (0, 0), (0, V7X_LANES - n_col)))
        wq_bf = wq.astype(BF16)
        scores, kmean = _paged_scores(page_table, wq_bf, cache_k, l, npg)
        o_s = _paged_attend(page_table, wq, wq_bf, scores, kmean, k_new, v_new,
                            slope_col, t_col, cache_v, l, npg)
        attn = jnp.transpose(o_s.reshape(ns_seq, kv_heads, 2, ts, HEAD_DIM), (0, 3, 1, 2, 4))
        attn = attn.reshape(1, ns_seq * ts, q_w)
        st_re = state_ssm_re[l].reshape(1, ns_seq, n_state)
        st_im = state_ssm_im[l].reshape(1, ns_seq, n_state)
        u_perm = jnp.swapaxes(u.reshape(ns_seq, ts, s_w), 0, 1).reshape(1, ts * ns_seq, s_w)
        y_perm, e_re, e_im = _ssm_scan(u_perm, st_re, st_im, a_re, a_im, bre, bim, cre, cim, d_skip, True)
        y = jnp.swapaxes(y_perm.reshape(ts, ns_seq, s_w), 0, 1).reshape(1, ns_seq * ts, s_w)
        ssm = _mm_glu(y, w_glu_bf, l)
        merged = _mm_merge(attn, ssm, w_attn_bf, w_ssm_bf, l, gates, d)
        xs = _mm_resid(merged, w_out_bf, l, xs, s_g1, 1024, 1024)
        h2 = _rmsnorm(xs, norm_ffn[l], s_sc2, s_sh2)
        keep = w_conv.shape[1] - 1
        cprev = state_conv[l]
        prev1 = jnp.concatenate([cprev[:, keep - 1:keep], jnp.zeros((ns_seq, ts - 1, 2 * d_ff), F32)], axis=1)
        prev2 = jnp.concatenate([cprev[:, keep - 2:keep], jnp.zeros((ns_seq, ts - 2, 2 * d_ff), F32)], axis=1)
        prev1 = prev1.reshape(1, ns_seq * ts, 2 * d_ff)
        prev2 = prev2.reshape(1, ns_seq * ts, 2 * d_ff)
        act, up_a, up_g = _ffn_up_carried(h2, w_up_a_bf, w_up_g_bf, w_conv, b_conv, l, d_ff,
                                          (prev1, prev2), ts)
        xs, w_down_bf = _mm_resid(act, w_down, l, xs, s_g2, 512, 256, round_weights=True)
        xp = _mm_resid(act_p, w_down_bf[None], 0, xp, g2, 1024, 256, lhs_buffers=1)
        outs["ks"].append(k_new.reshape(ns_seq, ts, kv_heads, HEAD_DIM))
        outs["vs"].append(v_new.reshape(ns_seq, ts, kv_heads, HEAD_DIM))
        outs["hsr"].append(e_re.reshape(ns_seq, n_groups, S5_STATE))
        outs["hsi"].append(e_im.reshape(ns_seq, n_groups, S5_STATE))
        up_full = jnp.concatenate([up_a, up_g], axis=-1).reshape(ns_seq, ts, 2 * d_ff)
        outs["cs"].append(up_full[:, ts - keep:, :])

    y_prompt = _rmsnorm(xp, norm_final, out_dtype=F32)
    y_sample = _rmsnorm(xs, norm_final, out_dtype=F32).reshape(ns_seq, ts, d)
    st = lambda k: jnp.stack(outs[k])
    return (y_prompt, y_sample, st("kp"), st("vp"), st("ks"), st("vs"),
            st("hpr"), st("hpi"), st("hsr"), st("hsi"), st("cp"), st("cs"))
```

```python
import functools
import math

import jax
import jax.numpy as jnp
from jax import lax
from jax.experimental import pallas as pl
from jax.experimental.pallas import tpu as pltpu

F32 = jnp.float32
BF16 = jnp.bfloat16

V7X_LANES = 128
V7X_SUBLANES = 8
V7X_VMEM_BYTES = 64 * 2**20
VMEM_BUDGET = V7X_VMEM_BYTES - 8 * 2**20

HEAD_DIM = 128
MOBA_BLOCK = 256
MOBA_TOPK = 3
S5_GROUP = 16
S5_STATE = 64
GROUPS_PER_CHUNK = 8
CHUNK_IN = GROUPS_PER_CHUNK * S5_GROUP
CHUNK_ST = GROUPS_PER_CHUNK * S5_STATE
N_SEG = V7X_SUBLANES
PAGES_PER_STEP = 16
EPS = 1e-6
NEG_INF = float("-inf")
LOG2_E = math.log2(math.e)
MASK_BIAS = -1e30


def _pick(dim, pref, align):
    t = min(pref, dim)
    t -= t % align
    while t >= align:
        if dim % t == 0:
            return t
        t -= align
    return dim


def _sigmoid(x):
    return 0.5 * jnp.tanh(0.5 * x) + 0.5


def _params(sem, vmem_bytes):
    limit = int(min(max(vmem_bytes * 5 // 4 + (4 << 20), 32 << 20), VMEM_BUDGET))
    return pltpu.CompilerParams(dimension_semantics=sem, vmem_limit_bytes=limit)


def _ada_kernel(c_ref, w_ref, b_ref, o_ref):
    s = jax.nn.silu(c_ref[...]).astype(BF16)
    w = w_ref[...].astype(BF16)
    o_ref[...] = jnp.dot(s, w, preferred_element_type=F32) + b_ref[...]


def _adaln(c_all, w_ada, b_ada):
    depth, d, n = w_ada.shape
    rows = c_all.shape[0]
    tn = _pick(n, 512, V7X_LANES)
    vmem = 2 * d * tn * 4 + d * tn * 2 + 4 * rows * (d + tn) * 4
    return pl.pallas_call(
        _ada_kernel,
        grid=(depth, n // tn),
        in_specs=[
            pl.BlockSpec((rows, d), lambda l, j: (0, 0)),
            pl.BlockSpec((None, d, tn), lambda l, j: (l, 0, j)),
            pl.BlockSpec((None, 1, tn), lambda l, j: (l, 0, j)),
        ],
        out_specs=pl.BlockSpec((None, rows, tn), lambda l, j: (l, 0, j)),
        out_shape=jax.ShapeDtypeStruct((depth, rows, n), F32),
        compiler_params=_params(("arbitrary", "arbitrary"), vmem),
        name="adaln",
    )(c_all, w_ada, b_ada.reshape(depth, 1, n))


def _norm_kernel(x_ref, g_ref, *rest, modulate):
    if modulate:
        sc_ref, sh_ref, o_ref = rest
    else:
        (o_ref,) = rest
    x = x_ref[...]
    y = x * lax.rsqrt(jnp.mean(x * x, axis=-1, keepdims=True) + EPS)
    y = y * g_ref[...]
    if modulate:
        y = y * (1.0 + sc_ref[...]) + sh_ref[...]
    o_ref[...] = y.astype(o_ref.dtype)


def _row_param_spec(p, tr, tn, col_blocked):
    shared = p.shape[1] == 1
    rows = 1 if shared else tr
    if col_blocked:
        if shared:
            return pl.BlockSpec((None, rows, tn), lambda b, i, j: (b, 0, j))
        return pl.BlockSpec((None, rows, tn), lambda b, i, j: (b, i, j))
    if shared:
        return pl.BlockSpec((None, rows, tn), lambda b, i: (b, 0, 0))
    return pl.BlockSpec((None, rows, tn), lambda b, i: (b, i, 0))


def _rmsnorm(x3, gamma, scale=None, shift=None, out_dtype=BF16):
    nblk, rb, d = x3.shape
    tr = _pick(rb, 512, V7X_SUBLANES)
    modulate = scale is not None
    in_specs = [
        pl.BlockSpec((None, tr, d), lambda b, i: (b, i, 0)),
        pl.BlockSpec((1, d), lambda b, i: (0, 0)),
    ]
    args = [x3, gamma.reshape(1, d)]
    if modulate:
        in_specs += [_row_param_spec(scale, tr, d, False), _row_param_spec(shift, tr, d, False)]
        args += [scale, shift]
    vmem = 2 * tr * d * (4 + 4) + 6 * tr * d * 4
    return pl.pallas_call(
        functools.partial(_norm_kernel, modulate=modulate),
        grid=(nblk, rb // tr),
        in_specs=in_specs,
        out_specs=pl.BlockSpec((None, tr, d), lambda b, i: (b, i, 0)),
        out_shape=jax.ShapeDtypeStruct((nblk, rb, d), out_dtype),
        compiler_params=_params(("arbitrary", "arbitrary"), vmem),
        name="rmsnorm_mod",
    )(*args)


def _mm_plain_kernel(a_ref, w_ref, o_ref):
    o_ref[...] = jnp.dot(a_ref[...], w_ref[...], preferred_element_type=F32).astype(o_ref.dtype)


def _w_spec(k, tn, layer, col_block0=0):
    return pl.BlockSpec((None, k, tn), lambda b, i, j: (layer, 0, j + col_block0))


def _mm_plain_round_kernel(a_ref, w_ref, o_ref, wbf_ref):
    wb = w_ref[...].astype(BF16)
    wbf_ref[...] = wb
    o_ref[...] = jnp.dot(a_ref[...], wb, preferred_element_type=F32).astype(o_ref.dtype)


def _mm_plain(a3, w, layer, col0, ncols, out_dtype, tm_pref=1024, tn_pref=1024, round_weights=False):
    nblk, rb, k = a3.shape
    tm = _pick(rb, tm_pref, V7X_SUBLANES)
    tn = _pick(math.gcd(ncols, col0) if col0 else ncols, 512 if round_weights else tn_pref, V7X_LANES)
    cb = col0 // tn
    osz = jnp.dtype(out_dtype).itemsize
    wsz = 4 if round_weights else 2
    vmem = 2 * (tm * k * 2 + k * tn * (wsz + 2) + tm * tn * osz) + tm * tn * 4
    out_specs = [pl.BlockSpec((None, tm, tn), lambda b, i, j: (b, i, j))]
    out_shape = [jax.ShapeDtypeStruct((nblk, rb, ncols), out_dtype)]
    if round_weights:
        assert nblk == 1 and tm == rb
        out_specs.append(pl.BlockSpec((k, tn), lambda b, i, j: (0, j)))
        out_shape.append(jax.ShapeDtypeStruct((k, ncols), BF16))
    res = pl.pallas_call(
        _mm_plain_round_kernel if round_weights else _mm_plain_kernel,
        grid=(nblk, rb // tm, ncols // tn),
        in_specs=[
            pl.BlockSpec((None, tm, k), lambda b, i, j: (b, i, 0)),
            _w_spec(k, tn, layer, cb),
        ],
        out_specs=out_specs,
        out_shape=out_shape,
        compiler_params=_params(("arbitrary", "arbitrary", "arbitrary"), vmem),
        name="proj_round" if round_weights else "proj",
    )(a3, w)
    return res if round_weights else res[0]


def _mm_glu_kernel(y_ref, w_ref, yt_ref, o_ref, ybf_ref):
    @pl.when(pl.program_id(2) == 0)
    def _():
        ybf_ref[...] = y_ref[...].astype(BF16)

    acc = jnp.dot(ybf_ref[...], w_ref[...], preferred_element_type=F32)
    o_ref[...] = (yt_ref[...] * _sigmoid(acc)).astype(o_ref.dtype)


def _mm_glu(y3, w, layer):
    nblk, rb, k = y3.shape
    n = w.shape[2]
    tm = _pick(rb, 1024, V7X_SUBLANES)
    tn = _pick(n, 512, V7X_LANES)
    vmem = 2 * (tm * k * 4 + k * tn * 2 + tm * tn * 4 + tm * tn * 2) + tm * k * 2 + tm * tn * 4
    return pl.pallas_call(
        _mm_glu_kernel,
        grid=(nblk, rb // tm, n // tn),
        in_specs=[
            pl.BlockSpec((None, tm, k), lambda b, i, j: (b, i, 0)),
            _w_spec(k, tn, layer),
            pl.BlockSpec((None, tm, tn), lambda b, i, j: (b, i, j)),
        ],
        out_specs=pl.BlockSpec((None, tm, tn), lambda b, i, j: (b, i, j)),
        out_shape=jax.ShapeDtypeStruct((nblk, rb, n), BF16),
        scratch_shapes=[pltpu.VMEM((tm, k), BF16)],
        compiler_params=_params(("arbitrary", "arbitrary", "arbitrary"), vmem),
        name="ssm_glu",
    )(y3, w, y3)


def _mm_merge_kernel(a_ref, s_ref, wa_ref, ws_ref, ga_ref, gs_ref, o_ref):
    pa = jnp.dot(a_ref[...], wa_ref[...], preferred_element_type=F32)
    ps = jnp.dot(s_ref[...], ws_ref[...], preferred_element_type=F32)
    o = _sigmoid(ga_ref[...]) * pa + _sigmoid(gs_ref[...]) * ps
    o_ref[...] = o.astype(o_ref.dtype)


def _mm_merge(attn3, ssm3, w_attn, w_ssm, layer, gates3, d):
    nblk, rb, ka = attn3.shape
    ks = ssm3.shape[2]
    tm = _pick(rb, 1024, V7X_SUBLANES)
    tn = _pick(d, 512, V7X_LANES)
    gs_off = d // tn
    vmem = 2 * (tm * (ka + ks) * 2 + (ka + ks) * tn * 2 + 2 * tm * tn * 4 + tm * tn * 2) + 3 * tm * tn * 4
    return pl.pallas_call(
        _mm_merge_kernel,
        grid=(nblk, rb // tm, d // tn),
        in_specs=[
            pl.BlockSpec((None, tm, ka), lambda b, i, j: (b, i, 0)),
            pl.BlockSpec((None, tm, ks), lambda b, i, j: (b, i, 0)),
            _w_spec(ka, tn, layer),
            _w_spec(ks, tn, layer),
            pl.BlockSpec((None, tm, tn), lambda b, i, j: (b, i, j)),
            pl.BlockSpec((None, tm, tn), lambda b, i, j: (b, i, j + gs_off)),
        ],
        out_specs=pl.BlockSpec((None, tm, tn), lambda b, i, j: (b, i, j)),
        out_shape=jax.ShapeDtypeStruct((nblk, rb, d), BF16),
        compiler_params=_params(("arbitrary", "arbitrary", "arbitrary"), vmem),
        name="mixer_merge",
    )(attn3, ssm3, w_attn, w_ssm, gates3, gates3)


def _mm_resid_kernel(a_ref, w_ref, x_ref, g_ref, o_ref):
    acc = jnp.dot(a_ref[...], w_ref[...], preferred_element_type=F32)
    o_ref[...] = x_ref[...] + g_ref[...] * acc


def _mm_resid_round_kernel(a_ref, w_ref, x_ref, g_ref, o_ref, wbf_ref):
    wb = w_ref[...].astype(BF16)
    wbf_ref[...] = wb
    o_ref[...] = x_ref[...] + g_ref[...] * jnp.dot(a_ref[...], wb, preferred_element_type=F32)


def _mm_resid(a3, w, layer, x3, gate, tm_pref, tn_pref, lhs_buffers=2, round_weights=False):
    nblk, rb, k = a3.shape
    n = w.shape[2]
    tm = _pick(rb, tm_pref, V7X_SUBLANES)
    tn = _pick(n, tn_pref, V7X_LANES)
    wsz = 4 if round_weights else 2
    vmem = lhs_buffers * tm * k * 2 + 2 * (k * tn * (wsz + 2) + 2 * tm * tn * 4) + tm * tn * 4
    out_specs = [pl.BlockSpec((None, tm, tn), lambda b, i, j: (b, i, j))]
    out_shape = [jax.ShapeDtypeStruct((nblk, rb, n), F32)]
    if round_weights:
        assert nblk == 1 and tm == rb
        out_specs.append(pl.BlockSpec((k, tn), lambda b, i, j: (0, j)))
        out_shape.append(jax.ShapeDtypeStruct((k, n), BF16))
    res = pl.pallas_call(
        _mm_resid_round_kernel if round_weights else _mm_resid_kernel,
        grid=(nblk, rb // tm, n // tn),
        in_specs=[
            pl.BlockSpec((None, tm, k), lambda b, i, j: (b, i, 0), pipeline_mode=pl.Buffered(lhs_buffers)),
            _w_spec(k, tn, layer),
            pl.BlockSpec((None, tm, tn), lambda b, i, j: (b, i, j)),
            _row_param_spec(gate, tm, tn, True),
        ],
        out_specs=out_specs,
        out_shape=out_shape,
        compiler_params=_params(("arbitrary", "arbitrary", "arbitrary"), vmem),
        name="proj_residual_round" if round_weights else "proj_residual",
    )(a3, w, x3, gate)
    return res if round_weights else res[0]


def _conv3(up, r1, r2, cw_ref, cb_ref):
    cw = cw_ref[...]
    return cb_ref[...] + (cw[0:1] * r2 + cw[1:2] * r1 + cw[2:3] * up)


def _ffn_up_fresh_kernel(h_ref, wa_ref, wg_ref, cwa_ref, cwg_ref, cba_ref, cbg_ref,
                         act_ref, sa_ref, sg_ref, wbfa_ref, wbfg_ref, wsc_ref, tail_ref, *, tm):
    tail_rows = V7X_SUBLANES

    @pl.when((pl.program_id(1) == 0) & (pl.program_id(2) == 0))
    def _():
        for p, (w_ref, wbf_ref) in enumerate(((wa_ref, wbfa_ref), (wg_ref, wbfg_ref))):
            wb = w_ref[...].astype(BF16)
            wsc_ref[p] = wb
            wbf_ref[...] = wb

    @pl.when(pl.program_id(2) == 0)
    def _():
        tail_ref[...] = jnp.zeros_like(tail_ref)

    lhs = h_ref[...]
    parts = []
    for p, (cw_ref, cb_ref, s_ref) in enumerate(((cwa_ref, cba_ref, sa_ref), (cwg_ref, cbg_ref, sg_ref))):
        up = jnp.dot(lhs, wsc_ref[p], preferred_element_type=F32)
        ext = jnp.concatenate([tail_ref[p], up], axis=0)
        new_tail = ext[tm:, :]
        tail_ref[p] = new_tail
        s_ref[...] = new_tail
        r1 = pltpu.roll(ext, 1, 0)[tail_rows:]
        r2 = pltpu.roll(ext, 2, 0)[tail_rows:]
        parts.append(_conv3(up, r1, r2, cw_ref, cb_ref))
    act_ref[...] = (jax.nn.silu(parts[1]) * parts[0]).astype(act_ref.dtype)


def _ffn_up_carried_kernel(h_ref, wa_ref, wg_ref, cwa_ref, cwg_ref, cba_ref, cbg_ref,
                           p1a_ref, p2a_ref, p1g_ref, p2g_ref, act_ref, sa_ref, sg_ref, *, seq):
    lhs = h_ref[...]
    parts = []
    for w_ref, cw_ref, cb_ref, p1_ref, p2_ref, s_ref in (
            (wa_ref, cwa_ref, cba_ref, p1a_ref, p2a_ref, sa_ref),
            (wg_ref, cwg_ref, cbg_ref, p1g_ref, p2g_ref, sg_ref)):
        up = jnp.dot(lhs, w_ref[...], preferred_element_type=F32)
        s_ref[...] = up
        tloc = lax.broadcasted_iota(jnp.int32, up.shape, 0) % seq
        r1 = jnp.where(tloc < 1, p1_ref[...], pltpu.roll(up, 1, 0))
        r2 = jnp.where(tloc < 2, p2_ref[...], pltpu.roll(up, 2, 0))
        parts.append(_conv3(up, r1, r2, cw_ref, cb_ref))
    act_ref[...] = (jax.nn.silu(parts[1]) * parts[0]).astype(act_ref.dtype)


def _ffn_up_fresh(h3, w_up, w_conv, b_conv, layer, d_ff):
    nblk, rb, k = h3.shape
    tn = _pick(d_ff, 512, V7X_LANES)
    goff = d_ff // tn
    tm = _pick(rb, 1024, V7X_SUBLANES)
    cb = b_conv.reshape(b_conv.shape[0], 1, 2 * d_ff)
    wspec = lambda rows, off: pl.BlockSpec((None, rows, tn), lambda j, b, i: (layer, 0, j + off))
    tail_spec = pl.BlockSpec((None, None, V7X_SUBLANES, tn), lambda j, b, i: (b, i, 0, j))
    wbf_spec = pl.BlockSpec((k, tn), lambda j, b, i: (0, j))
    vmem = (2 * (tm * k * 2 + 2 * k * tn * 4 + 2 * k * tn * 2 + tm * tn * 2) + 2 * k * tn * 2
            + 10 * (tm + 8) * tn * 4)
    return pl.pallas_call(
        functools.partial(_ffn_up_fresh_kernel, tm=tm),
        grid=(goff, nblk, rb // tm),
        in_specs=[
            pl.BlockSpec((None, tm, k), lambda j, b, i: (b, i, 0)),
            wspec(k, 0), wspec(k, goff),
            wspec(w_conv.shape[1], 0), wspec(w_conv.shape[1], goff),
            wspec(1, 0), wspec(1, goff),
        ],
        out_specs=[
            pl.BlockSpec((None, tm, tn), lambda j, b, i: (b, i, j)),
            tail_spec, tail_spec, wbf_spec, wbf_spec,
        ],
        out_shape=[
            jax.ShapeDtypeStruct((nblk, rb, d_ff), BF16),
            jax.ShapeDtypeStruct((nblk, rb // tm, V7X_SUBLANES, d_ff), F32),
            jax.ShapeDtypeStruct((nblk, rb // tm, V7X_SUBLANES, d_ff), F32),
            jax.ShapeDtypeStruct((k, d_ff), BF16),
            jax.ShapeDtypeStruct((k, d_ff), BF16),
        ],
        scratch_shapes=[pltpu.VMEM((2, k, tn), BF16), pltpu.VMEM((2, V7X_SUBLANES, tn), F32)],
        compiler_params=_params(("arbitrary", "arbitrary", "arbitrary"), vmem),
        name="convffn_up",
    )(h3, w_up, w_up, w_conv, w_conv, cb, cb)


def _ffn_up_carried(h3, w_a, w_g, w_conv, b_conv, layer, d_ff, prev, seq):
    nblk, rb, k = h3.shape
    tn = _pick(d_ff, 512, V7X_LANES)
    goff = d_ff // tn
    cb = b_conv.reshape(b_conv.shape[0], 1, 2 * d_ff)
    p1, p2 = prev
    cspec = lambda rows, off: pl.BlockSpec((None, rows, tn), lambda b, j: (layer, 0, j + off))
    wspec = pl.BlockSpec((k, tn), lambda b, j: (0, j))
    row_spec = lambda off: pl.BlockSpec((None, rb, tn), lambda b, j: (b, 0, j + off))
    vmem = 2 * (rb * k * 2 + 2 * k * tn * 2 + 8 * rb * tn * 4) + 10 * rb * tn * 4
    return pl.pallas_call(
        functools.partial(_ffn_up_carried_kernel, seq=seq),
        grid=(nblk, goff),
        in_specs=[
            pl.BlockSpec((None, rb, k), lambda b, j: (b, 0, 0)),
            wspec, wspec,
            cspec(w_conv.shape[1], 0), cspec(w_conv.shape[1], goff),
            cspec(1, 0), cspec(1, goff),
            row_spec(0), row_spec(0), row_spec(goff), row_spec(goff),
        ],
        out_specs=[row_spec(0), row_spec(0), row_spec(0)],
        out_shape=[
            jax.ShapeDtypeStruct((nblk, rb, d_ff), BF16),
            jax.ShapeDtypeStruct((nblk, rb, d_ff), F32),
            jax.ShapeDtypeStruct((nblk, rb, d_ff), F32),
        ],
        compiler_params=_params(("arbitrary", "arbitrary"), vmem),
        name="convffn_up_carried",
    )(h3, w_a, w_g, w_conv, w_conv, cb, cb, p1, p2, p1, p2)


def _beats(other, gate, other_is_lower):
    return jnp.where(other > gate, 1.0, 0.0) + jnp.where(other == gate, 1.0, 0.0) * other_is_lower


def _attn_fresh_body(cc, q_ref, o_ref, kaug_ref, vbf_ref, kmean_ref, sd_ref, *, nblk):
    blk = MOBA_BLOCK
    scale = HEAD_DIM ** -0.5
    w = (cc + 1) * blk
    q2 = q_ref[...]
    qs = jnp.concatenate([q2[:, :HEAD_DIM], q2[:, HEAD_DIM:]], axis=0)
    if cc > 0:
        gate_t = lax.dot_general(kmean_ref[...], qs, (((1,), (1,)), ((), ())),
                                 precision=lax.Precision.HIGHEST, preferred_element_type=F32)
        riota = lax.broadcasted_iota(jnp.int32, gate_t.shape, 0)
        cnt = jnp.zeros_like(gate_t)
        for m in range(cc):
            cnt = cnt + _beats(gate_t[m:m + 1, :], gate_t, jnp.where(riota > m, 1.0, 0.0))
        keep = jnp.where(cnt < MOBA_TOPK, 1.0, 0.0) + jnp.where(riota >= cc, 1.0, 0.0)
        bias_t = jnp.where(keep > 0.5, 0.0, MASK_BIAS)
        bias_t = jnp.concatenate(
            [bias_t, jnp.zeros((HEAD_DIM - nblk, 2 * blk), F32)], axis=0)
        bias = bias_t.T.astype(BF16)
    else:
        bias = jnp.zeros((2 * blk, HEAD_DIM), BF16)
    q_aug = jnp.concatenate([qs.astype(BF16), bias], axis=1)
    s = lax.dot_general(q_aug, kaug_ref[0:w, :], (((1,), (1,)), ((), ())),
                        preferred_element_type=F32)
    row = lax.broadcasted_iota(jnp.int32, (blk, blk), 0)
    col = lax.broadcasted_iota(jnp.int32, (blk, blk), 1)
    causal = row >= col
    for hh in range(2):
        sh = s[hh * blk:(hh + 1) * blk, :] * (scale * LOG2_E) - sd_ref[hh, :, 0:w]
        diag = jnp.where(causal, sh[:, cc * blk:], NEG_INF)
        sh = jnp.concatenate([sh[:, :cc * blk], diag], axis=1) if cc > 0 else diag
        m = jnp.max(sh, axis=-1, keepdims=True)
        p = jnp.exp2(sh - m)
        l = jnp.sum(p, axis=-1, keepdims=True)
        o = jnp.dot(p.astype(BF16), vbf_ref[0:w, :], preferred_element_type=F32) / l
        o_ref[:, hh * HEAD_DIM:(hh + 1) * HEAD_DIM] = o.astype(o_ref.dtype)


def _attn_fresh_kernel(slope_ref, q_ref, k_ref, v_ref, o_ref,
                       kaug_ref, vbf_ref, kmean_ref, sd_ref, *, nblk):
    g = pl.program_id(1)
    c = pl.program_id(2)
    blk = MOBA_BLOCK
    t = nblk * blk

    @pl.when(c == 0)
    def _():
        k = k_ref[...]
        kaug_ref[:, 0:HEAD_DIM] = k.astype(BF16)
        key_blk = lax.broadcasted_iota(jnp.int32, (t, HEAD_DIM), 0) // blk
        lane = lax.broadcasted_iota(jnp.int32, (t, HEAD_DIM), 1)
        kaug_ref[:, HEAD_DIM:] = jnp.where(key_blk == lane, 1.0, 0.0).astype(BF16)
        vbf_ref[...] = v_ref[...].astype(BF16)
        for n in range(nblk):
            kmean_ref[n:n + 1, :] = jnp.mean(k[n * blk:(n + 1) * blk, :], axis=0, keepdims=True)
        d0 = (lax.broadcasted_iota(jnp.int32, (blk, t), 0)
              - lax.broadcasted_iota(jnp.int32, (blk, t), 1)).astype(F32)
        for hh in range(2):
            sd_ref[hh] = (slope_ref[2 * g + hh] * LOG2_E) * d0

    for cc in range(nblk):
        pl.when(c == cc)(functools.partial(
            _attn_fresh_body, cc, q_ref, o_ref, kaug_ref, vbf_ref, kmean_ref, sd_ref, nblk=nblk))


def _attn_fresh(q3, k3, v3, slopes):
    b, t, q_w = q3.shape
    kv_heads = k3.shape[2] // HEAD_DIM
    assert q_w == 2 * kv_heads * HEAD_DIM and t % MOBA_BLOCK == 0
    nblk = t // MOBA_BLOCK
    assert nblk <= HEAD_DIM
    blk = MOBA_BLOCK
    vmem = (2 * (blk * 2 * HEAD_DIM * 4 + 2 * t * HEAD_DIM * 4 + blk * 2 * HEAD_DIM * 2)
            + 3 * t * HEAD_DIM * 2 + 2 * blk * t * 4 + 8 * 2 * blk * t * 4)
    return pl.pallas_call(
        functools.partial(_attn_fresh_kernel, nblk=nblk),
        grid=(b, kv_heads, nblk),
        in_specs=[
            pl.BlockSpec(memory_space=pltpu.SMEM),
            pl.BlockSpec((None, blk, 2 * HEAD_DIM), lambda bb, g, c: (bb, c, g)),
            pl.BlockSpec((None, t, HEAD_DIM), lambda bb, g, c: (bb, 0, g)),
            pl.BlockSpec((None, t, HEAD_DIM), lambda bb, g, c: (bb, 0, g)),
        ],
        out_specs=pl.BlockSpec((None, blk, 2 * HEAD_DIM), lambda bb, g, c: (bb, c, g)),
        out_shape=jax.ShapeDtypeStruct((b, t, q_w), BF16),
        scratch_shapes=[
            pltpu.VMEM((t, 2 * HEAD_DIM), BF16),
            pltpu.VMEM((t, HEAD_DIM), BF16),
            pltpu.VMEM((nblk, HEAD_DIM), F32),
            pltpu.VMEM((2, blk, t), F32),
        ],
        compiler_params=_params(("arbitrary", "arbitrary", "arbitrary"), vmem),
        name="moba_fresh",
    )(slopes, q3, k3, v3)


def _page_heads_on_lanes(p_ref, page, kvh):
    return jnp.concatenate([p_ref[pl.ds(g, page, stride=kvh), :] for g in range(kvh)], axis=-1)


def _page_specs(layer, npg, page_rows, hd):
    def spec(j):
        return pl.BlockSpec((None, None, page_rows, hd),
                            lambda i, n, pt: (layer, pt[i, n * npg + j], 0, 0))
    return [spec(j) for j in range(npg)]


def _paged_scores_kernel(pt_ref, wq_ref, *refs, page, kvh, npg):
    pages = refs[:npg]
    sc_ref, km_ref = refs[npg:]
    wq = wq_ref[...]
    hd = pages[0].shape[1]
    sums = []
    for j, p_ref in enumerate(pages):
        kcat = _page_heads_on_lanes(p_ref, page, kvh)
        sc_ref[j * page:(j + 1) * page, :] = jnp.dot(kcat.astype(BF16), wq, preferred_element_type=F32)
        sums.append(jnp.sum(p_ref[...].reshape(page, kvh, hd), axis=0))
    for b in range(npg // 2):
        km_ref[b] = (sums[2 * b] + sums[2 * b + 1]) * (1.0 / (2 * page))


def _paged_scores(page_table, wq_bf, cache, layer, npg):
    s, n_pages = page_table.shape
    depth, n_pool, page, kvh, hd = cache.shape
    assert 2 * page == MOBA_BLOCK and n_pages % npg == 0 and npg % 2 == 0
    kv_w = kvh * hd
    cache_rows = cache.reshape(depth, n_pool, page * kvh, hd)
    nblk = n_pages // 2
    past = n_pages * page
    vmem = 2 * (kv_w * V7X_LANES * 2 + npg * page * kv_w * 4 + npg * page * V7X_LANES * 4) + 6 * page * kv_w * 4
    grid_spec = pltpu.PrefetchScalarGridSpec(
        num_scalar_prefetch=1,
        grid=(s, n_pages // npg),
        in_specs=[pl.BlockSpec((None, kv_w, V7X_LANES), lambda i, n, pt: (i, 0, 0))]
        + _page_specs(layer, npg, page * kvh, hd),
        out_specs=[
            pl.BlockSpec((None, npg * page, V7X_LANES), lambda i, n, pt: (i, n, 0)),
            pl.BlockSpec((None, npg // 2, kvh, hd), lambda i, n, pt: (i, n, 0, 0)),
        ],
    )
    return pl.pallas_call(
        functools.partial(_paged_scores_kernel, page=page, kvh=kvh, npg=npg),
        grid_spec=grid_spec,
        out_shape=[
            jax.ShapeDtypeStruct((s, past, V7X_LANES), F32),
            jax.ShapeDtypeStruct((s, nblk, kvh, hd), F32),
        ],
        compiler_params=_params(("arbitrary", "arbitrary"), vmem),
        name="moba_paged_scores",
    )(page_table, wq_bf, *([cache_rows] * npg))


def _paged_attend_kernel(pt_ref, wq_ref, wqbf_ref, sc_ref, km_ref, knew_ref, vnew_ref,
                         slope_ref, tcol_ref, *refs, page, kvh, npg, nblk, tnew, past):
    pages = refs[:npg]
    o_ref, sel_ref, prob_ref, linv_ref, sq_ref, vpad_ref, acc_ref = refs[npg:]
    n = pl.program_id(1)
    blk = MOBA_BLOCK
    scale = HEAD_DIM ** -0.5
    lanes = V7X_LANES
    slope = slope_ref[...]
    tcol = tcol_ref[...]
    koff = lax.broadcasted_iota(jnp.int32, (blk, lanes), 0).astype(F32)

    def block_scores(i):
        r0 = pl.multiple_of(i * blk, blk)
        raw = sc_ref[pl.ds(r0, blk), :]
        dist = (tcol + lax.convert_element_type(past - i * blk, F32)) - koff
        s = raw * scale - slope * dist
        return jnp.where(sel_ref[pl.ds(i, 1), :] > 0.5, s, NEG_INF)

    @pl.when(n == 0)
    def _():
        gate = jnp.zeros((nblk, lanes), F32)
        for g in range(kvh):
            gate = gate + jnp.dot(km_ref[pl.ds(g, nblk, stride=kvh), :],
                                  wq_ref[g * HEAD_DIM:(g + 1) * HEAD_DIM, :],
                                  precision=lax.Precision.HIGHEST, preferred_element_type=F32)
        riota = lax.broadcasted_iota(jnp.int32, (nblk, lanes), 0)
        cnt = jnp.zeros((nblk, lanes), F32)
        for m in range(nblk):
            cnt = cnt + _beats(gate[m:m + 1, :], gate, jnp.where(riota > m, 1.0, 0.0))
        sel_ref[...] = jnp.where(cnt < MOBA_TOPK, 1.0, 0.0)

        s_cur = jnp.dot(knew_ref[...].astype(BF16), wqbf_ref[...], preferred_element_type=F32)
        off = lax.broadcasted_iota(jnp.int32, (tnew, lanes), 0).astype(F32)
        s_cur = s_cur * scale - slope * (tcol - off)
        s_cur = jnp.where(off <= tcol, s_cur, NEG_INF)
        m0 = jnp.max(s_cur, axis=0, keepdims=True)

        def max_body(i, m):
            return jnp.maximum(m, jnp.max(block_scores(i), axis=0, keepdims=True))

        mx = lax.fori_loop(0, nblk, max_body, m0)
        p_cur = jnp.exp(s_cur - mx)

        def sum_body(i, l):
            p = jnp.exp(block_scores(i) - mx)
            prob_ref[pl.ds(pl.multiple_of(i * blk, blk), blk), :] = p
            return l + jnp.sum(p, axis=0, keepdims=True)

        l = lax.fori_loop(0, nblk, sum_body, jnp.sum(p_cur, axis=0, keepdims=True))
        linv = 1.0 / l
        linv_ref[...] = linv
        sq_ref[...] = jnp.zeros_like(sq_ref)
        sq_ref[0:tnew, :] = p_cur * linv
        vpad_ref[...] = jnp.zeros_like(vpad_ref)
        vpad_ref[0:tnew, :] = vnew_ref[...].astype(BF16)
        acc_ref[...] = jnp.dot(sq_ref[...].T.astype(BF16), vpad_ref[...], preferred_element_type=F32)

    linv = linv_ref[...]
    acc = acc_ref[...]
    for j, p_ref in enumerate(pages):
        r0 = pl.multiple_of((n * npg + j) * page, page)
        pt_j = (prob_ref[pl.ds(r0, page), :] * linv).T.astype(BF16)
        vcat = _page_heads_on_lanes(p_ref, page, kvh).astype(BF16)
        acc = acc + jnp.dot(pt_j, vcat, preferred_element_type=F32)
    acc_ref[...] = acc

    @pl.when(n == pl.num_programs(1) - 1)
    def _():
        rows = o_ref.shape[1]
        for g in range(o_ref.shape[0]):
            o_ref[g] = acc_ref[g * rows:(g + 1) * rows, g * HEAD_DIM:(g + 1) * HEAD_DIM].astype(o_ref.dtype)


def _paged_attend(page_table, wq, wq_bf, scores, kmean, k_new, v_new, slope_col, t_col, cache, layer, npg):
    s, n_pages = page_table.shape
    depth, n_pool, page, kvh, hd = cache.shape
    kv_w = kvh * hd
    cache_rows = cache.reshape(depth, n_pool, page * kvh, hd)
    nblk = n_pages // 2
    past = n_pages * page
    tnew = k_new.shape[1]
    rows = 2 * tnew
    kmean_rows = kmean.reshape(s, nblk * kvh, hd)
    vmem = (2 * (kv_w * V7X_LANES * 6 + past * V7X_LANES * 4 + nblk * kv_w * 4 + npg * page * kv_w * 4)
            + past * V7X_LANES * 4 + 3 * V7X_LANES * kv_w * 4 + 16 * MOBA_BLOCK * V7X_LANES * 4
            + 4 * page * kv_w * 4)
    grid_spec = pltpu.PrefetchScalarGridSpec(
        num_scalar_prefetch=1,
        grid=(s, n_pages // npg),
        in_specs=[
            pl.BlockSpec((None, kv_w, V7X_LANES), lambda i, n, pt: (i, 0, 0)),
            pl.BlockSpec((None, kv_w, V7X_LANES), lambda i, n, pt: (i, 0, 0)),
            pl.BlockSpec((None, past, V7X_LANES), lambda i, n, pt: (i, 0, 0)),
            pl.BlockSpec((None, nblk * kvh, hd), lambda i, n, pt: (i, 0, 0)),
            pl.BlockSpec((None, tnew, kv_w), lambda i, n, pt: (i, 0, 0)),
            pl.BlockSpec((None, tnew, kv_w), lambda i, n, pt: (i, 0, 0)),
            pl.BlockSpec((1, V7X_LANES), lambda i, n, pt: (0, 0)),
            pl.BlockSpec((1, V7X_LANES), lambda i, n, pt: (0, 0)),
        ] + _page_specs(layer, npg, page * kvh, hd),
        out_specs=pl.BlockSpec((None, kvh, rows, hd), lambda i, n, pt: (i, 0, 0, 0)),
        scratch_shapes=[
            pltpu.VMEM((nblk, V7X_LANES), F32),
            pltpu.VMEM((past, V7X_LANES), F32),
            pltpu.VMEM((1, V7X_LANES), F32),
            pltpu.VMEM((V7X_LANES, V7X_LANES), F32),
            pltpu.VMEM((V7X_LANES, kv_w), BF16),
            pltpu.VMEM((V7X_LANES, kv_w), F32),
        ],
    )
    return pl.pallas_call(
        functools.partial(_paged_attend_kernel, page=page, kvh=kvh, npg=npg, nblk=nblk, tnew=tnew, past=past),
        grid_spec=grid_spec,
        out_shape=jax.ShapeDtypeStruct((s, kvh, rows, hd), BF16),
        compiler_params=_params(("arbitrary", "arbitrary"), vmem),
        name="moba_paged_attend",
    )(page_table, wq, wq_bf, scores, kmean_rows, k_new, v_new, slope_col, t_col, *([cache_rows] * npg))


def _ssm_prep_kernel(lr_ref, li_ref, ldt_ref, brt_ref, bit_ref, are_ref, aim_ref, bbr_ref, bbi_ref):
    lr = lr_ref[...]
    li = li_ref[...]
    dt = jnp.exp(ldt_ref[...])
    mag = jnp.exp(lr * dt)
    a_re = mag * jnp.cos(li * dt)
    a_im = mag * jnp.sin(li * dt)
    den = lr * lr + li * li
    q_re = ((a_re - 1.0) * lr + a_im * li) / den
    q_im = (a_im * lr - (a_re - 1.0) * li) / den
    are_ref[...] = a_re
    aim_ref[...] = a_im
    br = brt_ref[...]
    bi = bit_ref[...]
    bbr_ref[...] = q_re * br - q_im * bi
    bbi_ref[...] = q_re * bi + q_im * br


def _ssm_prep(lam_re, lam_im, log_dt, b_re, b_im):
    depth, g, p = lam_re.shape
    i = b_re.shape[3]
    brt = jnp.swapaxes(b_re, 2, 3)
    bit = jnp.swapaxes(b_im, 2, 3)
    vec = pl.BlockSpec((None, g, 1, p), lambda l: (l, 0, 0, 0))
    mat = pl.BlockSpec((None, g, i, p), lambda l: (l, 0, 0, 0))
    return pl.pallas_call(
        _ssm_prep_kernel,
        grid=(depth,),
        in_specs=[vec, vec, pl.BlockSpec((None, g, 1, 1), lambda l: (l, 0, 0, 0)), mat, mat],
        out_specs=[vec, vec, mat, mat],
        out_shape=[
            jax.ShapeDtypeStruct((depth, g, 1, p), F32),
            jax.ShapeDtypeStruct((depth, g, 1, p), F32),
            jax.ShapeDtypeStruct((depth, g, i, p), F32),
            jax.ShapeDtypeStruct((depth, g, i, p), F32),
        ],
        compiler_params=_params(("arbitrary",), 32 << 20),
        name="s5_discretise",
    )(lam_re.reshape(depth, g, 1, p), lam_im.reshape(depth, g, 1, p),
      log_dt.reshape(depth, g, 1, 1), brt, bit)


def _ssm_scan_kernel(u_ref, ire_ref, iim_ref, are_ref, aim_ref, bre_ref, bim_ref,
                     cre_ref, cim_ref, d_ref, *rest, with_y, n_chunks, tau_b):
    if with_y:
        y_ref, ere_ref, eim_ref, hre_ref, him_ref, bure_ref, buim_ref = rest
    else:
        ere_ref, eim_ref, hre_ref, him_ref, bure_ref, buim_ref = rest
    k = pl.program_id(1)

    @pl.when(k == 0)
    def _():
        hre_ref[...] = ire_ref[...]
        him_ref[...] = iim_ref[...]

    def chunk(c, carry):
        cu = pl.multiple_of(c * CHUNK_IN, CHUNK_IN)
        cs = pl.multiple_of(c * CHUNK_ST, CHUNK_ST)
        u_c = u_ref[:, pl.ds(cu, CHUNK_IN)]
        ub = u_c.astype(BF16)
        bure_ref[...] = jnp.dot(ub, bre_ref[c], preferred_element_type=F32)
        buim_ref[...] = jnp.dot(ub, bim_ref[c], preferred_element_type=F32)
        ar = jnp.broadcast_to(are_ref[:, pl.ds(cs, CHUNK_ST)], (N_SEG, CHUNK_ST))
        ai = jnp.broadcast_to(aim_ref[:, pl.ds(cs, CHUNK_ST)], (N_SEG, CHUNK_ST))
        hr = hre_ref[:, pl.ds(cs, CHUNK_ST)]
        hi = him_ref[:, pl.ds(cs, CHUNK_ST)]
        for t in range(tau_b):
            rs = slice(t * N_SEG, (t + 1) * N_SEG)
            nr = ar * hr - ai * hi + bure_ref[rs, :]
            ni = ar * hi + ai * hr + buim_ref[rs, :]
            hr, hi = nr, ni
            if with_y:
                bure_ref[rs, :] = hr
                buim_ref[rs, :] = hi
        hre_ref[:, pl.ds(cs, CHUNK_ST)] = hr
        him_ref[:, pl.ds(cs, CHUNK_ST)] = hi
        if with_y:
            ych = (jnp.dot(bure_ref[...].astype(BF16), cre_ref[c], preferred_element_type=F32)
                   - jnp.dot(buim_ref[...].astype(BF16), cim_ref[c], preferred_element_type=F32))
            y = ych + d_ref[:, pl.ds(cu, CHUNK_IN)] * u_c
            y_ref[:, pl.ds(cu, CHUNK_IN)] = jax.nn.gelu(y)
        return carry

    lax.fori_loop(0, n_chunks, chunk, 0)

    @pl.when(k == pl.num_programs(1) - 1)
    def _():
        ere_ref[...] = hre_ref[...]
        eim_ref[...] = him_ref[...]


def _ssm_scan(u3, init_re, init_im, a_re, a_im, bre, bim, cre, cim, d_skip, with_y):
    nb, lr, s_w = u3.shape
    ns = a_re.shape[1]
    n_chunks = bre.shape[0]
    steps = lr // N_SEG
    tau_b = _pick(steps, 32, 1)
    rb = tau_b * N_SEG
    st_spec = pl.BlockSpec((None, N_SEG, ns), lambda b, k: (b, 0, 0))
    full = lambda shape: pl.BlockSpec(shape, lambda b, k: (0,) * len(shape))
    u_spec = pl.BlockSpec((None, rb, s_w), lambda b, k: (b, k, 0))
    out_specs = [st_spec, st_spec]
    out_shape = [jax.ShapeDtypeStruct((nb, N_SEG, ns), F32)] * 2
    if with_y:
        out_specs = [u_spec] + out_specs
        out_shape = [jax.ShapeDtypeStruct((nb, lr, s_w), F32)] + out_shape
    vmem = (4 * rb * s_w * 4 + 8 * N_SEG * ns * 4 + 4 * ns * 4
            + 2 * 4 * n_chunks * CHUNK_IN * CHUNK_ST * 2 + 8 * rb * CHUNK_ST * 4)
    return pl.pallas_call(
        functools.partial(_ssm_scan_kernel, with_y=with_y, n_chunks=n_chunks, tau_b=tau_b),
        grid=(nb, steps // tau_b),
        in_specs=[u_spec, st_spec, st_spec, full((1, ns)), full((1, ns)),
                  full(bre.shape), full(bim.shape), full(cre.shape), full(cim.shape),
                  full((1, s_w))],
        out_specs=out_specs,
        out_shape=out_shape,
        scratch_shapes=[
            pltpu.VMEM((N_SEG, ns), F32),
            pltpu.VMEM((N_SEG, ns), F32),
            pltpu.VMEM((rb, CHUNK_ST), F32),
            pltpu.VMEM((rb, CHUNK_ST), F32),
        ],
        compiler_params=_params(("arbitrary", "arbitrary"), vmem),
        name="s5_scan" if with_y else "s5_segment_ends",
    )(u3, init_re, init_im, a_re, a_im, bre, bim, cre, cim, d_skip)


def _ssm_carry_kernel(ere_ref, eim_ref, are_ref, aim_ref, ire_ref, iim_ref, fre_ref, fim_ref, *, seg_len):
    br = are_ref[...]
    bi = aim_ref[...]
    pr = jnp.ones_like(br)
    pi = jnp.zeros_like(br)
    e = seg_len
    while e:
        if e & 1:
            pr, pi = pr * br - pi * bi, pr * bi + pi * br
        br, bi = br * br - bi * bi, 2.0 * br * bi
        e >>= 1
    hr = jnp.zeros_like(pr)
    hi = jnp.zeros_like(pr)
    for j in range(N_SEG):
        ire_ref[j:j + 1, :] = hr
        iim_ref[j:j + 1, :] = hi
        er = ere_ref[j:j + 1, :]
        ei = eim_ref[j:j + 1, :]
        hr, hi = pr * hr - pi * hi + er, pr * hi + pi * hr + ei
    fre_ref[...] = hr
    fim_ref[...] = hi


def _ssm_carry(end_re, end_im, a_re, a_im, seg_len):
    nb, _, ns = end_re.shape
    st = pl.BlockSpec((None, N_SEG, ns), lambda b: (b, 0, 0))
    vec = pl.BlockSpec((1, ns), lambda b: (0, 0))
    fin = pl.BlockSpec((None, 1, ns), lambda b: (b, 0, 0))
    return pl.pallas_call(
        functools.partial(_ssm_carry_kernel, seg_len=seg_len),
        grid=(nb,),
        in_specs=[st, st, vec, vec],
        out_specs=[st, st, fin, fin],
        out_shape=[jax.ShapeDtypeStruct((nb, N_SEG, ns), F32)] * 2
        + [jax.ShapeDtypeStruct((nb, 1, ns), F32)] * 2,
        compiler_params=_params(("arbitrary",), 32 << 20),
        name="s5_segment_carry",
    )(end_re, end_im, a_re, a_im)


def _block_diag_in(bbt):
    g, i, p = bbt.shape
    nc = g // GROUPS_PER_CHUNK
    eye = jnp.eye(GROUPS_PER_CHUNK, dtype=bbt.dtype)
    x = bbt.reshape(nc, GROUPS_PER_CHUNK, i, p)
    out = x[:, :, :, None, :] * eye[None, :, None, :, None]
    return out.reshape(nc, GROUPS_PER_CHUNK * i, GROUPS_PER_CHUNK * p).astype(BF16)


def _block_diag_out(c):
    g, i, p = c.shape
    nc = g // GROUPS_PER_CHUNK
    eye = jnp.eye(GROUPS_PER_CHUNK, dtype=c.dtype)
    x = jnp.swapaxes(c.reshape(nc, GROUPS_PER_CHUNK, i, p), 2, 3)
    out = x[:, :, :, None, :] * eye[None, :, None, :, None]
    return out.reshape(nc, GROUPS_PER_CHUNK * p, GROUPS_PER_CHUNK * i).astype(BF16)


def kernel(x_prompt, x_sample, c_prompt, c_sample, cache_k, cache_v, state_ssm_re, state_ssm_im,
           state_conv, page_table, w_ada, b_ada, norm_attn, w_in, w_attn_proj, lam_re, lam_im,
           log_dt, ssm_b_re, ssm_b_im, ssm_c_re, ssm_c_im, ssm_d, w_glu, w_ssm_proj, w_out,
           norm_ffn, w_up, w_conv, b_conv, w_down, norm_final):
    depth = w_ada.shape[0]
    nb, t, d = x_prompt.shape
    ns_seq, ts, _ = x_sample.shape
    q_w = w_attn_proj.shape[1]
    n_heads = q_w // HEAD_DIM
    kv_heads = cache_k.shape[3]
    kv_w = kv_heads * HEAD_DIM
    s_w = ssm_d.shape[1]
    n_groups = lam_re.shape[1]
    n_state = n_groups * S5_STATE
    d_ff = w_down.shape[1]
    n_pages = page_table.shape[1]
    page = cache_k.shape[2]
    past = n_pages * page
    assert ns_seq == N_SEG and t % (N_SEG * N_SEG) == 0 and n_groups % GROUPS_PER_CHUNK == 0
    n_col = kv_heads * 2 * ts
    assert n_col <= V7X_LANES and n_heads == 2 * kv_heads
    assert ssm_b_re.shape[2:] == (S5_STATE, S5_GROUP)
    u_col = q_w + 2 * kv_w
    g_col = u_col + s_w

    n_c = nb + ns_seq
    c_rows = -(-n_c // V7X_SUBLANES) * V7X_SUBLANES
    c_all = jnp.concatenate([c_prompt, c_sample, jnp.zeros((c_rows - n_c, d), F32)], axis=0)
    mod = _adaln(c_all, w_ada, b_ada)

    a_re_all, a_im_all, bbr_all, bbi_all = _ssm_prep(lam_re, lam_im, log_dt, ssm_b_re, ssm_b_im)
    slopes = 2.0 ** (-8.0 * jnp.arange(1, n_heads + 1, dtype=F32) / n_heads)

    col = jnp.arange(V7X_LANES)
    col_head = jnp.minimum(2 * (col // (2 * ts)) + (col // ts) % 2, n_heads - 1)
    slope_col = slopes[col_head].reshape(1, V7X_LANES)
    t_col = (col % ts).astype(F32).reshape(1, V7X_LANES)

    xp = x_prompt
    xs = x_sample.reshape(1, ns_seq * ts, d)
    seg = t // N_SEG
    keep = w_conv.shape[1] - 1
    assert keep == 2 and past % MOBA_BLOCK == 0 and ts <= MOBA_BLOCK
    npg = _pick(n_pages, PAGES_PER_STEP, 2)
    outs ={k: [] for k in ("kp", "vp", "ks", "vs", "hpr", "hpi", "hsr", "hsi", "cp", "cs")}

    w_attn_bf = w_attn_proj.astype(BF16)
    w_glu_bf = w_glu.astype(BF16)
    w_ssm_bf = w_ssm_proj.astype(BF16)
    w_out_bf = w_out.astype(BF16)
    in_cols = ((0, q_w), (q_w, kv_w), (q_w + kv_w, kv_w), (u_col, s_w), (g_col, 2 * d))

    for l in range(depth):
        a_re = a_re_all[l].reshape(1, n_state)
        a_im = a_im_all[l].reshape(1, n_state)
        bre = _block_diag_in(bbr_all[l])
        bim = _block_diag_in(bbi_all[l])
        cre = _block_diag_out(ssm_c_re[l])
        cim = _block_diag_out(ssm_c_im[l])
        d_skip = ssm_d[l].reshape(1, s_w)
        mod_l = mod[l].reshape(c_rows, 6, d)

        def mods_for(lo, n, rep):
            m = mod_l[lo:lo + n]
            if rep == 1:
                return [m[:, i].reshape(n, 1, d) for i in range(6)]
            return [jnp.repeat(m[:, i], rep, axis=0).reshape(1, n * rep, d) for i in range(6)]

        s_sh1, s_sc1, s_g1, s_sh2, s_sc2, s_g2 = mods_for(nb, ns_seq, ts)
        h_s = _rmsnorm(xs, norm_attn[l], s_sc1, s_sh1)
        proj_s = [_mm_plain(h_s, w_in, l, c0, nc, F32, round_weights=True) for c0, nc in in_cols]
        w_in_bf = [p[1][None] for p in proj_s]

        sh1, sc1, g1, sh2, sc2, g2 = mods_for(0, nb, 1)
        h = _rmsnorm(xp, norm_attn[l], sc1, sh1)
        q, k, v, u, gates = [_mm_plain(h, wb, 0, 0, nc, F32) for wb, (_, nc) in zip(w_in_bf, in_cols)]
        attn = _attn_fresh(q, k, v, slopes)
        u_perm = u.reshape(nb, N_SEG, seg, s_w).swapaxes(1, 2).reshape(nb, t, s_w)
        zero_st = jnp.zeros((nb, N_SEG, n_state), F32)
        end_re, end_im = _ssm_scan(u_perm, zero_st, zero_st, a_re, a_im, bre, bim, cre, cim, d_skip, False)
        ini_re, ini_im, fin_re, fin_im = _ssm_carry(end_re, end_im, a_re, a_im, seg)
        y_perm, _, _ = _ssm_scan(u_perm, ini_re, ini_im, a_re, a_im, bre, bim, cre, cim, d_skip, True)
        y = y_perm.reshape(nb, seg, N_SEG, s_w).swapaxes(1, 2).reshape(nb, t, s_w)
        ssm = _mm_glu(y, w_glu_bf, l)
        merged = _mm_merge(attn, ssm, w_attn_bf, w_ssm_bf, l, gates, d)
        xp = _mm_resid(merged, w_out_bf, l, xp, g1, 1024, 512)
        h2 = _rmsnorm(xp, norm_ffn[l], sc2, sh2)
        act_p, tail_a, tail_g, w_up_a_bf, w_up_g_bf = _ffn_up_fresh(h2, w_up, w_conv, b_conv, l, d_ff)
        outs["kp"].append(k.reshape(nb, t, kv_heads, HEAD_DIM))
        outs["vp"].append(v.reshape(nb, t, kv_heads, HEAD_DIM))
        outs["hpr"].append(fin_re.reshape(nb, n_groups, S5_STATE))
        outs["hpi"].append(fin_im.reshape(nb, n_groups, S5_STATE))
        keep = w_conv.shape[1] - 1
        outs["cp"].append(jnp.concatenate([tail_a[:, -1], tail_g[:, -1]], axis=-1)[:, V7X_SUBLANES - keep:, :])

        q, k, v, u, gates = [p[0] for p in proj_s]
        q_s = q.reshape(ns_seq, ts, kv_heads, 2, HEAD_DIM)
        k_new = k.reshape(ns_seq, ts, kv_w)
        v_new = v.reshape(ns_seq, ts, kv_w)
        eye = jnp.eye(kv_heads, dtype=F32)
        wq = (jnp.transpose(q_s, (0, 2, 4, 3, 1))[:, :, :, None, :, :]
              * eye[None, :, None, :, None, None]).reshape(ns_seq, kv_w, n_col)
        wq = jnp.pad(wq, ((0, 0), (0, 0), (0, V7X_LANES - n_col)))
        wq_bf = wq.astype(BF16)
        scores, kmean = _paged_scores(page_table, wq_bf, cache_k, l, npg)
        o_s = _paged_attend(page_table, wq, wq_bf, scores, kmean, k_new, v_new,
                            slope_col, t_col, cache_v, l, npg)
        attn = jnp.transpose(o_s.reshape(ns_seq, kv_heads, 2, ts, HEAD_DIM), (0, 3, 1, 2, 4))
        attn = attn.reshape(1, ns_seq * ts, q_w)
        st_re = state_ssm_re[l].reshape(1, ns_seq, n_state)
        st_im = state_ssm_im[l].reshape(1, ns_seq, n_state)
        u_perm = jnp.swapaxes(u.reshape(ns_seq, ts, s_w), 0, 1).reshape(1, ts * ns_seq, s_w)
        y_perm, e_re, e_im = _ssm_scan(u_perm, st_re, st_im, a_re, a_im, bre, bim, cre, cim, d_skip, True)
        y = jnp.swapaxes(y_perm.reshape(ts, ns_seq, s_w), 0, 1).reshape(1, ns_seq * ts, s_w)
        ssm = _mm_glu(y, w_glu_bf, l)
        merged = _mm_merge(attn, ssm, w_attn_bf, w_ssm_bf, l, gates, d)
        xs = _mm_resid(merged, w_out_bf, l, xs, s_g1, 1024, 1024)
        h2 = _rmsnorm(xs, norm_ffn[l], s_sc2, s_sh2)
        keep = w_conv.shape[1] - 1
        cprev = state_conv[l]
        prev1 = jnp.concatenate([cprev[:, keep - 1:keep], jnp.zeros((ns_seq, ts - 1, 2 * d_ff), F32)], axis=1)
        prev2 = jnp.concatenate([cprev[:, keep - 2:keep], jnp.zeros((ns_seq, ts - 2, 2 * d_ff), F32)], axis=1)
        prev1 = prev1.reshape(1, ns_seq * ts, 2 * d_ff)
        prev2 = prev2.reshape(1, ns_seq * ts, 2 * d_ff)
        act, up_a, up_g = _ffn_up_carried(h2, w_up_a_bf, w_up_g_bf, w_conv, b_conv, l, d_ff,
                                          (prev1, prev2), ts)
        xs, w_down_bf = _mm_resid(act, w_down, l, xs, s_g2, 512, 256, round_weights=True)
        xp = _mm_resid(act_p, w_down_bf[None], 0, xp, g2, 1024, 256, lhs_buffers=1)
        outs["ks"].append(k_new.reshape(ns_seq, ts, kv_heads, HEAD_DIM))
        outs["vs"].append(v_new.reshape(ns_seq, ts, kv_heads, HEAD_DIM))
        outs["hsr"].append(e_re.reshape(ns_seq, n_groups, S5_STATE))
        outs["hsi"].append(e_im.reshape(ns_seq, n_groups, S5_STATE))
        up_full = jnp.concatenate([up_a, up_g], axis=-1).reshape(ns_seq, ts, 2 * d_ff)
        outs["cs"].append(up_full[:, ts - keep:, :])

    y_prompt = _rmsnorm(xp, norm_final, out_dtype=F32)
    y_sample = _rmsnorm(xs, norm_final, out_dtype=F32).reshape(ns_seq, ts, d)
    st = lambda k: jnp.stack(outs[k])
    return (y_prompt, y_sample, st("kp"), st("vp"), st("ks"), st("vs"),
            st("hpr"), st("hpi"), st("hsr"), st("hsi"), st("cp"), st("cs"))
```
